```python
import math
import jax
import jax.numpy as jnp
from jax import lax
import numpy as np

D_MODEL = 1024
BATCH = 2
SEQ = 16384
DEPTH = 2

GRID_W = 64
CTX_LEN = 256
EPS = 1e-6

POOL_WINDOWS = (2, 4, 8, 16)
POOL_GROUP = 64
POOL_WIDTH = POOL_GROUP * len(POOL_WINDOWS)

MLA_HEADS = 8
MLA_NOPE = 64
MLA_ROPE = 32
MLA_QK = MLA_NOPE + MLA_ROPE
MLA_V = 64
MLA_Q_RANK = 384
MLA_KV_RANK = 256
MLA_WIDTH = MLA_HEADS * MLA_V
MLA_SCALE = MLA_QK ** -0.5
ROPE_BASE = 10000.0
Q_BLOCK = 128

ML_HEADS = 4
ML_DK = 64
ML_DV = 64
ML_WIDTH = ML_HEADS * ML_DV
ML_CHUNK = 64
NEG_INIT = -1e30

MIX_WIDTH = POOL_WIDTH + MLA_WIDTH + ML_WIDTH

IN_SIZES = (POOL_WIDTH, MLA_Q_RANK, MLA_KV_RANK, MLA_ROPE, ML_WIDTH, ML_WIDTH, ML_WIDTH, ML_WIDTH, 4 * ML_HEADS)
IN_WIDTH = 1968

PEER_HEADS = 8
PEER_KEYS = 128
PEER_EXPERTS = PEER_KEYS * PEER_KEYS
PEER_DKEY = 256
PEER_TOPK = 16
PEER_BLOCK = 128

kernel_name = "hybrid_pool_mla_mlstm_peer_dit"


def rms_norm(x, gain):
    xf = x.astype(jnp.float32)
    y = xf * lax.rsqrt(jnp.mean(xf * xf, axis=-1, keepdims=True) + EPS)
    return (y * gain.astype(jnp.float32)).astype(x.dtype)


def modulate(x, gain, shift, scale):
    return rms_norm(x, gain) * (1 + scale) + shift


def ada_mod(cond, w, b):
    m = jax.nn.silu(cond) @ w + b
    return [t[:, None, :] for t in jnp.split(m, 6, axis=-1)]


def split_columns(z):
    parts, start = [], 0
    for size in IN_SIZES:
        parts.append(z[..., start:start + size])
        start += size
    return parts


def axial_rope_tables(n):
    n_rows = n // GRID_W
    row = jnp.repeat(jnp.arange(n_rows), GRID_W, total_repeat_length=n).astype(jnp.float32)
    col = (jnp.arange(n) % GRID_W).astype(jnp.float32)
    per_axis = MLA_ROPE // 2
    freqs = ROPE_BASE ** (-jnp.arange(0, per_axis, 2, dtype=jnp.float32) / per_axis)
    ang = jnp.concatenate([row[:, None] * freqs, col[:, None] * freqs], axis=-1)
    return jnp.cos(ang), jnp.sin(ang)


def apply_rope(x, cos, sin):
    half = MLA_ROPE // 2
    c = cos[None, :, None, :]
    s = sin[None, :, None, :]
    x1, x2 = x[..., :half], x[..., half:]
    return jnp.concatenate([x1 * c - x2 * s, x2 * c + x1 * s], axis=-1).astype(x.dtype)


def centred_pool_minus_self(u):
    L = u.shape[1]
    uf = u.astype(jnp.float32)
    cs = jnp.concatenate([jnp.zeros_like(uf[:, :1]), jnp.cumsum(uf, axis=1)], axis=1)
    t = jnp.arange(L)
    outs = []
    for g, w in enumerate(POOL_WINDOWS):
        lo_c, hi_c = g * POOL_GROUP, (g + 1) * POOL_GROUP
        csg = cs[..., lo_c:hi_c]
        lo = jnp.clip(t - w // 2, 0, L)
        hi = jnp.clip(t + w // 2, 0, L)
        win_sum = jnp.take(csg, hi, axis=1) - jnp.take(csg, lo, axis=1)
        cnt = (hi - lo).astype(jnp.float32)[None, :, None]
        outs.append(win_sum / cnt - uf[..., lo_c:hi_c])
    return jnp.concatenate(outs, axis=-1).astype(u.dtype)


def pool_mixer(u, w_pool, scale):
    B, L, _ = u.shape
    d = centred_pool_minus_self(u).reshape(B, L, len(POOL_WINDOWS), POOL_GROUP)
    y = jnp.einsum('blgc,gce->blge', d, w_pool).reshape(B, L, POOL_WIDTH)
    return y * scale


def mla_qkv(zq, zkv, zkr, q_a_norm, kv_a_norm, w_uq, w_ukv, q_gain, k_gain, rope):
    B, L, _ = zq.shape
    q = (rms_norm(zq, q_a_norm) @ w_uq).reshape(B, L, MLA_HEADS, MLA_QK)
    kv = (rms_norm(zkv, kv_a_norm) @ w_ukv).reshape(B, L, MLA_HEADS, MLA_NOPE + MLA_V)
    k_rope = jnp.broadcast_to(zkr[:, :, None, :], (B, L, MLA_HEADS, MLA_ROPE))
    k = jnp.concatenate([kv[..., :MLA_NOPE], k_rope], axis=-1)
    v = kv[..., MLA_NOPE:]
    q = rms_norm(q, q_gain)
    k = rms_norm(k, k_gain)
    if rope is not None:
        cos, sin = rope
        q = jnp.concatenate([q[..., :MLA_NOPE], apply_rope(q[..., MLA_NOPE:], cos, sin)], axis=-1)
        k = jnp.concatenate([k[..., :MLA_NOPE], apply_rope(k[..., MLA_NOPE:], cos, sin)], axis=-1)
    return q, k, v


def attend(q, k, v):
    s = jnp.einsum('bqhd,bkhd->bhqk', q, k).astype(jnp.float32) * MLA_SCALE
    p = jax.nn.softmax(s, axis=-1).astype(v.dtype)
    return jnp.einsum('bhqk,bkhd->bqhd', p, v)


def blocked_attention(q, k, v):
    B, L, H, D = q.shape
    nb = L // Q_BLOCK
    qb = q.reshape(B, nb, Q_BLOCK, H, D).swapaxes(0, 1)
    out = lax.map(lambda qq: attend(qq, k, v), qb)
    return out.swapaxes(0, 1).reshape(B, L, H, v.shape[-1])


def zero_state(B):
    return (jnp.zeros((B, ML_HEADS, ML_DV, ML_DK), jnp.float32),
            jnp.zeros((B, ML_HEADS, ML_DK), jnp.float32),
            jnp.full((B, ML_HEADS), NEG_INIT, jnp.float32))


def mlstm_scan(q, k, v, log_i, log_f, state):
    B, L, H, _ = q.shape
    nc = L // ML_CHUNK

    def chunks(a):
        a = a.astype(jnp.float32)
        return a.reshape((B, nc, ML_CHUNK) + a.shape[2:]).swapaxes(0, 1)

    causal = jnp.tril(jnp.ones((ML_CHUNK, ML_CHUNK), dtype=bool))

    def step(carry, inp):
        C, n, m = carry
        qc, kc, vc, ic, fc = inp
        b = jnp.cumsum(fc, axis=1).swapaxes(1, 2)
        ih = ic.swapaxes(1, 2)
        dmat = b[..., :, None] - b[..., None, :] + ih[..., None, :]
        dmat = jnp.where(causal, dmat, -jnp.inf)
        inter = b + m[..., None]
        m_j = jnp.maximum(inter, dmat.max(axis=-1))
        w_inter = jnp.exp(inter - m_j)
        qk = jnp.einsum('bjhd,bshd->bhjs', qc, kc) * jnp.exp(dmat - m_j[..., None])
        num = (jnp.einsum('bhjs,bshv->bjhv', qk, vc)
               + jnp.einsum('bhvd,bjhd->bjhv', C, qc) * w_inter.swapaxes(1, 2)[..., None])
        den = qk.sum(axis=-1) + w_inter * jnp.einsum('bhd,bjhd->bhj', n, qc)
        denom = jnp.maximum(jnp.abs(den), jnp.exp(-m_j)).swapaxes(1, 2)[..., None]
        h = num / denom
        b_last = b[..., -1]
        dec = b_last[..., None] - b + ih
        m_new = jnp.maximum(b_last + m, dec.max(axis=-1))
        w_prev = jnp.exp(b_last + m - m_new)
        w_s = jnp.exp(dec - m_new[..., None])
        C_new = w_prev[..., None, None] * C + jnp.einsum('bhs,bshv,bshd->bhvd', w_s, vc, kc)
        n_new = w_prev[..., None] * n + jnp.einsum('bhs,bshd->bhd', w_s, kc)
        return (C_new, n_new, m_new), h

    final, hs = lax.scan(step, state, (chunks(q), chunks(k), chunks(v), chunks(log_i), chunks(log_f)))
    return hs.swapaxes(0, 1).reshape(B, L, H, v.shape[-1]), final


def mlstm_inputs(zq, zk, zv, zg, gate_bias):
    B, L, _ = zq.shape
    q = zq.reshape(B, L, ML_HEADS, ML_DK) * (ML_DK ** -0.5)
    k = zk.reshape(B, L, ML_HEADS, ML_DK)
    v = zv.reshape(B, L, ML_HEADS, ML_DV)
    g = (zg + gate_bias).astype(jnp.float32)
    i_f, f_f, i_b, f_b = jnp.split(g, 4, axis=-1)
    return q, k, v, (i_f, jax.nn.log_sigmoid(f_f)), (i_b, jax.nn.log_sigmoid(f_b))


def mlstm_bidir(q, k, v, gates_f, gates_b, st_f, st_b):
    h_f, st_f = mlstm_scan(q, k, v, gates_f[0], gates_f[1], st_f)
    flip = lambda a: jnp.flip(a, axis=1)
    h_b, st_b = mlstm_scan(flip(q), flip(k), flip(v), flip(gates_b[0]), flip(gates_b[1]), st_b)
    return (h_f + flip(h_b)).astype(q.dtype), st_f, st_b


def mlstm_out(h, o_pre, gain):
    B, L = h.shape[:2]
    o = jax.nn.sigmoid(o_pre).reshape(B, L, ML_HEADS, ML_DV)
    return (rms_norm(h, gain) * o).reshape(B, L, ML_WIDTH)


def peer_ffn(h, w_q, subkeys, u_tab, v_tab):
    B, L, D = h.shape
    hb_all = h.reshape((B * L) // PEER_BLOCK, PEER_BLOCK, D)

    def block(hb):
        P = hb.shape[0]
        q = (hb @ w_q).reshape(P, PEER_HEADS, 2, PEER_DKEY // 2)
        s = jnp.einsum('phcd,hcnd->phcn', q, subkeys).astype(jnp.float32)
        top_v, top_i = lax.top_k(s, PEER_TOPK)
        cand = top_v[:, :, 0, :, None] + top_v[:, :, 1, None, :]
        best, flat = lax.top_k(cand.reshape(P, PEER_HEADS, PEER_TOPK * PEER_TOPK), PEER_TOPK)
        i1 = jnp.take_along_axis(top_i[:, :, 0, :], flat // PEER_TOPK, axis=-1)
        i2 = jnp.take_along_axis(top_i[:, :, 1, :], flat % PEER_TOPK, axis=-1)
        expert = i1 * PEER_KEYS + i2
        g = jax.nn.softmax(best, axis=-1)
        a = jax.nn.gelu(jnp.einsum('pd,phkd->phk', hb, u_tab[expert]).astype(jnp.float32), approximate=False)
        w = (g * a).astype(hb.dtype)
        return jnp.einsum('phk,phkd->pd', w, v_tab[expert])

    return lax.map(block, hb_all).reshape(B, L, D)


def setup_inputs(seed: int = 0) -> dict:
    key = jax.random.key(seed)
    ks = jax.random.split(key, 32)
    nrm = lambda k, shape, s: jax.random.normal(k, shape, jnp.float32) * s
    D = D_MODEL
    i_bias = nrm(ks[20], (DEPTH, 2, ML_HEADS), 0.1)
    f_bias = 3.0 + nrm(ks[21], (DEPTH, 2, ML_HEADS), 0.5)
    gate_bias = jnp.stack([i_bias[:, 0], f_bias[:, 0], i_bias[:, 1], f_bias[:, 1]], axis=1).reshape(DEPTH, 4 * ML_HEADS)
    return {
        "x": nrm(ks[0], (BATCH, SEQ, D), 1.0),
        "c": nrm(ks[1], (BATCH, D), 1.0),
        "ctx": nrm(ks[2], (BATCH, CTX_LEN, D), 1.0),
        "c_ctx": nrm(ks[3], (D,), 1.0),
        "norm1_gain": 1.0 + nrm(ks[4], (DEPTH, D), 0.1),
        "norm2_gain": 1.0 + nrm(ks[5], (DEPTH, D), 0.1),
        "w_ada": nrm(ks[6], (DEPTH, D, 6 * D), 0.5 * D ** -0.5),
        "b_ada": nrm(ks[7], (DEPTH, 6 * D), 0.02),
        "w_in": nrm(ks[8], (DEPTH, D, IN_WIDTH), D ** -0.5),
        "pool_w": nrm(ks[9], (DEPTH, len(POOL_WINDOWS), POOL_GROUP, POOL_GROUP), POOL_GROUP ** -0.5),
        "pool_scale": 1.0 + nrm(ks[10], (DEPTH, POOL_WIDTH), 0.1),
        "mla_q_norm": 1.0 + nrm(ks[11], (DEPTH, MLA_Q_RANK), 0.1),
        "mla_kv_norm": 1.0 + nrm(ks[12], (DEPTH, MLA_KV_RANK), 0.1),
        "w_uq": nrm(ks[13], (DEPTH, MLA_Q_RANK, MLA_HEADS * MLA_QK), MLA_Q_RANK ** -0.5),
        "w_ukv": nrm(ks[14], (DEPTH, MLA_KV_RANK, MLA_HEADS * (MLA_NOPE + MLA_V)), MLA_KV_RANK ** -0.5),
        "q_norm": 1.0 + nrm(ks[15], (DEPTH, MLA_QK), 0.1),
        "k_norm": 1.0 + nrm(ks[16], (DEPTH, MLA_QK), 0.1),
        "ml_gate_bias": gate_bias,
        "ml_out_norm": 1.0 + nrm(ks[17], (DEPTH, ML_HEADS, ML_DV), 0.1),
        "w_out": nrm(ks[18], (DEPTH, MIX_WIDTH, D), MIX_WIDTH ** -0.5),
        "peer_wq": nrm(ks[19], (DEPTH, D, PEER_HEADS * PEER_DKEY), D ** -0.5),
        "peer_subkeys": nrm(ks[22], (DEPTH, PEER_HEADS, 2, PEER_KEYS, PEER_DKEY // 2), (PEER_DKEY // 2) ** -0.5),
        "peer_u": nrm(ks[23], (DEPTH, PEER_EXPERTS, D), D ** -0.5),
        "peer_v": nrm(ks[24], (DEPTH, PEER_EXPERTS, D), 1.0),
    }


def reference(x, c, ctx, c_ctx, norm1_gain, norm2_gain, w_ada, b_ada, w_in, pool_w, pool_scale,
              mla_q_norm, mla_kv_norm, w_uq, w_ukv, q_norm, k_norm, ml_gate_bias, ml_out_norm, w_out,
              peer_wq, peer_subkeys, peer_u, peer_v):
    B, S, _ = x.shape
    rope = axial_rope_tables(S)
    x_ctx = ctx
    for l in range(DEPTH):
        last = l == DEPTH - 1
        sh_a, sc_a, g_a, sh_f, sc_f, g_f = ada_mod(c, w_ada[l], b_ada[l])
        csh_a, csc_a, cg_a, csh_f, csc_f, cg_f = ada_mod(c_ctx[None], w_ada[l], b_ada[l])

        zc = split_columns(modulate(x_ctx, norm1_gain[l], csh_a, csc_a) @ w_in[l])
        qc, kc, vc = mla_qkv(zc[1], zc[2], zc[3], mla_q_norm[l], mla_kv_norm[l], w_uq[l], w_ukv[l],
                             q_norm[l], k_norm[l], None)
        mq, mk, mv, gf, gb = mlstm_inputs(zc[4], zc[5], zc[6], zc[8], ml_gate_bias[l])
        h_mc, st_f, st_b = mlstm_bidir(mq, mk, mv, gf, gb, zero_state(B), zero_state(B))
        if not last:
            pool_c = pool_mixer(zc[0], pool_w[l], pool_scale[l])
            attn_c = attend(qc, kc, vc).reshape(B, CTX_LEN, MLA_WIDTH)
            ml_c = mlstm_out(h_mc, zc[7], ml_out_norm[l])
            mix_c = jnp.concatenate([pool_c, attn_c, ml_c], axis=-1) @ w_out[l]
            x_ctx_new = x_ctx + cg_a * mix_c
            x_ctx_new = x_ctx_new + cg_f * peer_ffn(modulate(x_ctx_new, norm2_gain[l], csh_f, csc_f),
                                                    peer_wq[l], peer_subkeys[l], peer_u[l], peer_v[l])

        z = split_columns(modulate(x, norm1_gain[l], sh_a, sc_a) @ w_in[l])
        pool_x = pool_mixer(z[0], pool_w[l], pool_scale[l])
        q, k, v = mla_qkv(z[1], z[2], z[3], mla_q_norm[l], mla_kv_norm[l], w_uq[l], w_ukv[l],
                          q_norm[l], k_norm[l], rope)
        k_all = jnp.concatenate([k, kc], axis=1)
        v_all = jnp.concatenate([v, vc], axis=1)
        attn_x = blocked_attention(q, k_all, v_all).reshape(B, S, MLA_WIDTH)
        mq, mk, mv, gf, gb = mlstm_inputs(z[4], z[5], z[6], z[8], ml_gate_bias[l])
        h_m, _, _ = mlstm_bidir(mq, mk, mv, gf, gb, st_f, st_b)
        ml_x = mlstm_out(h_m, z[7], ml_out_norm[l])
        mix = jnp.concatenate([pool_x, attn_x, ml_x], axis=-1) @ w_out[l]
        x = x + g_a * mix
        x = x + g_f * peer_ffn(modulate(x, norm2_gain[l], sh_f, sc_f),
                               peer_wq[l], peer_subkeys[l], peer_u[l], peer_v[l])
        if not last:
            x_ctx = x_ctx_new
    return x
```

```python
import functools
import math

import jax
import jax.numpy as jnp
from jax import lax
from jax.experimental import pallas as pl
from jax.experimental.pallas import tpu as pltpu

F32 = jnp.float32
BF16 = jnp.bfloat16
HIGHEST = lax.Precision.HIGHEST

EPS = 1e-6
D_MODEL = 1024
GRID_W = 64
ROPE_BASE = 10000.0
POOL_WINDOWS = (2, 4, 8, 16)
POOL_GROUP = 64
POOL_WIDTH = 256
MLA_HEADS = 8
MLA_NOPE = 64
MLA_ROPE = 32
MLA_QK = 96
MLA_V = 64
MLA_Q_RANK = 384
MLA_KV_RANK = 256
MLA_SCALE = MLA_QK ** -0.5
HEAD_PAD = 128
ML_HEADS = 4
ML_DK = 64
ML_WIDTH = 256
ML_CHUNK = 128
NEG_INIT = -1e30
IN_SIZES = (256, 384, 256, 32, 256, 256, 256, 256, 16)
PEER_HEADS = 8
PEER_KEYS = 128
PEER_TOPK = 16
PEER_SLOTS = PEER_HEADS * PEER_TOPK
PEER_TB = 64
ROW_SUB = 4

Z_POOL, Z_MLA, Z_ML = 256, 768, 1152
VMEM_LIMIT = 56 * 1024 * 1024


def _cp(*sem, vmem=VMEM_LIMIT):
    return pltpu.CompilerParams(dimension_semantics=sem, vmem_limit_bytes=vmem)


def _ada_kernel(c_ref, w_ref, b_ref, o_ref):
    c = c_ref[...]
    s = c * (1.0 / (1.0 + jnp.exp(-c)))
    o_ref[...] = jnp.dot(s, w_ref[...], precision=HIGHEST, preferred_element_type=F32) + b_ref[...]


def ada_mod(cond8, w, b):
    n = w.shape[1]
    tn = n // 4
    return pl.pallas_call(
        _ada_kernel,
        grid=(n // tn,),
        in_specs=[pl.BlockSpec((8, D_MODEL), lambda j: (0, 0)),
                  pl.BlockSpec((D_MODEL, tn), lambda j: (0, j)),
                  pl.BlockSpec((1, tn), lambda j: (0, j))],
        out_specs=pl.BlockSpec((8, tn), lambda j: (0, j)),
        out_shape=jax.ShapeDtypeStruct((8, n), F32),
        compiler_params=_cp("arbitrary"),
        name="ada_mod",
    )(cond8, w, b.reshape(1, n))


def _modulated(x, gain, shift, scale):
    ms = jnp.mean(x * x, axis=-1, keepdims=True)
    y = x * lax.rsqrt(ms + EPS) * gain
    return y * (1.0 + scale) + shift


def _inproj_kernel(x_ref, gain_ref, shift_ref, scale_ref, w_ref, zp_ref, zm_ref, zl_ref):
    h = _modulated(x_ref[0], gain_ref[...], shift_ref[0], scale_ref[0])
    res = jnp.dot(h.astype(BF16), w_ref[...], preferred_element_type=F32)
    zp_ref[0] = res[:, :Z_POOL]
    zm_ref[0] = res[:, Z_POOL:Z_POOL + Z_MLA]
    zl_ref[0] = res[:, Z_POOL + Z_MLA:]


def in_proj(x, gain, shift, scale, w):
    B, L, D = x.shape
    tm = min(512, L)
    n = w.shape[1]
    vec = pl.BlockSpec((1, 1, D), lambda b, i: (b, 0, 0))
    return pl.pallas_call(
        _inproj_kernel,
        grid=(B, L // tm),
        in_specs=[pl.BlockSpec((1, tm, D), lambda b, i: (b, i, 0)),
                  pl.BlockSpec((1, D), lambda b, i: (0, 0)),
                  vec, vec,
                  pl.BlockSpec((D, n), lambda b, i: (0, 0))],
        out_specs=[pl.BlockSpec((1, tm, Z_POOL), lambda b, i: (b, i, 0)),
                   pl.BlockSpec((1, tm, Z_MLA), lambda b, i: (b, i, 0)),
                   pl.BlockSpec((1, tm, Z_ML), lambda b, i: (b, i, 0))],
        out_shape=[jax.ShapeDtypeStruct((B, L, Z_POOL), F32),
                   jax.ShapeDtypeStruct((B, L, Z_MLA), F32),
                   jax.ShapeDtypeStruct((B, L, Z_ML), F32)],
        compiler_params=_cp("parallel", "parallel"),
        name="in_proj",
    )(x, gain.reshape(1, D), shift, scale, w)


def _peerq_kernel(x_ref, gain_ref, shift_ref, scale_ref, w_ref, qp_ref, h_ref):
    h = _modulated(x_ref[0], gain_ref[...], shift_ref[0], scale_ref[0])
    hb = h.astype(BF16)
    h_ref[...] = hb
    res = jnp.dot(hb, w_ref[...], preferred_element_type=F32)
    for g in range(2 * PEER_HEADS):
        qp_ref[g] = res[:, g * PEER_KEYS:(g + 1) * PEER_KEYS]


def peer_query(x, gain, shift, scale, w):
    B, L, D = x.shape
    tm = min(512, L)
    nb = L // tm
    n = w.shape[1]
    vec = pl.BlockSpec((1, 1, D), lambda b, i: (b, 0, 0))
    return pl.pallas_call(
        _peerq_kernel,
        grid=(B, nb),
        in_specs=[pl.BlockSpec((1, tm, D), lambda b, i: (b, i, 0)),
                  pl.BlockSpec((1, D), lambda b, i: (0, 0)),
                  vec, vec,
                  pl.BlockSpec((D, n), lambda b, i: (0, 0))],
        out_specs=[pl.BlockSpec((2 * PEER_HEADS, tm, PEER_KEYS), lambda b, i: (0, b * nb + i, 0)),
                   pl.BlockSpec((tm, D), lambda b, i: (b * nb + i, 0))],
        out_shape=[jax.ShapeDtypeStruct((2 * PEER_HEADS, B * L, PEER_KEYS), F32),
                   jax.ShapeDtypeStruct((B * L, D), BF16)],
        compiler_params=_cp("parallel", "parallel"),
        name="peer_query",
    )(x, gain.reshape(1, D), shift, scale, w)


def _pool_kernel(p_ref, c_ref, n_ref, wbd_ref, sc_ref, o_ref, *, L, T):
    i = pl.program_id(1)
    cur = c_ref[0]
    u3 = jnp.concatenate([p_ref[0], cur, n_ref[0]], axis=0).astype(BF16)
    t = i * T + lax.broadcasted_iota(jnp.int32, (T, 3 * T), 0)
    s = (i - 1) * T + lax.broadcasted_iota(jnp.int32, (T, 3 * T), 1)
    lane = lax.broadcasted_iota(jnp.int32, (T, POOL_WIDTH), 1)
    trow = i * T + lax.broadcasted_iota(jnp.int32, (T, POOL_WIDTH), 0)
    win = jnp.zeros((T, POOL_WIDTH), F32)
    for g, w in enumerate(POOL_WINDOWS):
        lo = jnp.maximum(t - w // 2, 0)
        hi = jnp.minimum(t + w // 2, L)
        band = jnp.where((s >= lo) & (s < hi), 1.0, 0.0).astype(BF16)
        ws = jnp.dot(band, u3, preferred_element_type=F32)
        cnt = (jnp.minimum(trow + w // 2, L) - jnp.maximum(trow - w // 2, 0)).astype(F32)
        in_group = (lane >= g * POOL_GROUP) & (lane < (g + 1) * POOL_GROUP)
        win = jnp.where(in_group, ws / cnt, win)
    d = win - cur
    y = jnp.dot(d.astype(BF16), wbd_ref[...], preferred_element_type=F32)
    o_ref[0] = y * sc_ref[...]


def pool_mixer(z_pool, wbd, scale):
    B, L, C = z_pool.shape
    T = 256
    nb = L // T
    return pl.pallas_call(
        functools.partial(_pool_kernel, L=L, T=T),
        grid=(B, nb),
        in_specs=[pl.BlockSpec((1, T, C), lambda b, i: (b, jnp.maximum(i - 1, 0), 0)),
                  pl.BlockSpec((1, T, C), lambda b, i: (b, i, 0)),
                  pl.BlockSpec((1, T, C), lambda b, i: (b, jnp.minimum(i + 1, nb - 1), 0)),
                  pl.BlockSpec((C, C), lambda b, i: (0, 0)),
                  pl.BlockSpec((1, C), lambda b, i: (0, 0))],
        out_specs=pl.BlockSpec((1, T, C), lambda b, i: (b, i, 0)),
        out_shape=jax.ShapeDtypeStruct((B, L, C), F32),
        compiler_params=_cp("parallel", "parallel"),
        name="pool_mixer",
    )(z_pool, z_pool, z_pool, wbd, scale.reshape(1, C))


def _rms(x, gain, n):
    ss = jnp.sum(x * x, axis=-1, keepdims=True) * (1.0 / n)
    return x * lax.rsqrt(ss + EPS) * gain


def _mla_kernel(z_ref, qan_ref, kvan_ref, wq_ref, wk_ref, wv_ref, pk_ref, qg_ref, kg_ref,
                cos_ref, sin_ref, q_out, k_out, v_out):
    z = z_ref[0]
    zq = z[:, :MLA_Q_RANK]
    zkv = z[:, MLA_Q_RANK:MLA_Q_RANK + MLA_KV_RANK]
    zkr = z[:, MLA_Q_RANK + MLA_KV_RANK:]
    nq = _rms(zq, qan_ref[...], MLA_Q_RANK).astype(BF16)
    nkv = _rms(zkv, kvan_ref[...], MLA_KV_RANK).astype(BF16)
    qp = jnp.dot(nq, wq_ref[...], preferred_element_type=F32)
    kp = jnp.dot(nkv, wk_ref[...], preferred_element_type=F32)
    kp = kp + jnp.dot(zkr, pk_ref[...], precision=HIGHEST, preferred_element_type=F32)
    vp = jnp.dot(nkv, wv_ref[...], preferred_element_type=F32)
    cos = cos_ref[...]
    sin = sin_ref[...]
    lane = lax.broadcasted_iota(jnp.int32, cos.shape, 1)
    first_half = lane < MLA_NOPE + MLA_ROPE // 2
    for h in range(MLA_HEADS):
        sl = slice(h * HEAD_PAD, (h + 1) * HEAD_PAD)
        for src, gain_ref, out, mult in ((qp, qg_ref, q_out, MLA_SCALE), (kp, kg_ref, k_out, 1.0)):
            xn = _rms(src[:, sl], gain_ref[...], MLA_QK)
            partner = jnp.where(first_half, pltpu.roll(xn, HEAD_PAD - MLA_ROPE // 2, 1),
                                pltpu.roll(xn, MLA_ROPE // 2, 1))
            xr = xn * cos + partner * sin
            out[0, h] = (xr * mult).astype(BF16)
        v_out[0, h] = vp[:, sl].astype(BF16)


def mla_qkv(z_mla, qan, kvan, wq, wk, wv, pk, qg, kg, cos, sin):
    B, L, _ = z_mla.shape
    T = 256
    full = lambda shape: pl.BlockSpec(shape, lambda b, i: (0,) * len(shape))
    head_out = pl.BlockSpec((1, MLA_HEADS, T, HEAD_PAD), lambda b, i: (b, 0, i, 0))
    out_sds = jax.ShapeDtypeStruct((B, MLA_HEADS, L, HEAD_PAD), BF16)
    return pl.pallas_call(
        _mla_kernel,
        grid=(B, L // T),
        in_specs=[pl.BlockSpec((1, T, Z_MLA), lambda b, i: (b, i, 0)),
                  full((1, MLA_Q_RANK)), full((1, MLA_KV_RANK)),
                  full(wq.shape), full(wk.shape), full(wv.shape), full(pk.shape),
                  full((1, HEAD_PAD)), full((1, HEAD_PAD)),
                  pl.BlockSpec((T, HEAD_PAD), lambda b, i: (i, 0)),
                  pl.BlockSpec((T, HEAD_PAD), lambda b, i: (i, 0))],
        out_specs=[head_out, head_out, head_out],
        out_shape=[out_sds, out_sds, out_sds],
        compiler_params=_cp("parallel", "parallel"),
        name="mla_qkv",
    )(z_mla, qan.reshape(1, -1), kvan.reshape(1, -1), wq, wk, wv, pk, qg, kg, cos, sin)


def _flash_kernel(q_ref, k_ref, v_ref, o_ref, *, tc, nchunks):
    q = q_ref[0, 0]
    tq = q.shape[0]

    def body(c, carry):
        m, l, acc = carry
        off = pl.multiple_of(c * tc, tc)
        kc = k_ref[0, 0, pl.ds(off, tc), :]
        vc = v_ref[0, 0, pl.ds(off, tc), :]
        s = lax.dot_general(q, kc, (((1,), (1,)), ((), ())), preferred_element_type=F32)
        m_new = jnp.maximum(m, jnp.max(s, axis=-1, keepdims=True))
        alpha = jnp.exp(m - m_new)
        p = jnp.exp(s - m_new)
        l = alpha * l + jnp.sum(p, axis=-1, keepdims=True)
        acc = alpha * acc + jnp.dot(p.astype(BF16), vc, preferred_element_type=F32)
        return m_new, l, acc

    init = (jnp.full((tq, 1), -jnp.inf, F32), jnp.zeros((tq, 1), F32), jnp.zeros((tq, HEAD_PAD), F32))
    _, l, acc = lax.fori_loop(0, nchunks, body, init)
    o_ref[0] = (acc / l).astype(BF16)


def attention(q, k, v, tc):
    B, H, Lq, _ = q.shape
    Lk = k.shape[2]
    tq = 256
    return pl.pallas_call(
        functools.partial(_flash_kernel, tc=tc, nchunks=Lk // tc),
        grid=(B, H, Lq // tq),
        in_specs=[pl.BlockSpec((1, 1, tq, HEAD_PAD), lambda b, h, i: (b, h, i, 0)),
                  pl.BlockSpec((1, 1, Lk, HEAD_PAD), lambda b, h, i: (b, h, 0, 0)),
                  pl.BlockSpec((1, 1, Lk, HEAD_PAD), lambda b, h, i: (b, h, 0, 0))],
        out_specs=pl.BlockSpec((1, tq, HEAD_PAD), lambda b, h, i: (b, i, h)),
        out_shape=jax.ShapeDtypeStruct((B, Lq, H * HEAD_PAD), BF16),
        compiler_params=_cp("parallel", "parallel", "arbitrary"),
        name="attention",
    )(q, k, v)


def _log_sigmoid(x):
    return jnp.minimum(x, 0.0) - jnp.log(1.0 + jnp.exp(-jnp.abs(x)))


def _mlstm_direction(d, q, k, v, g, C_s, n_s, m_s):
    Lc = q.shape[0]
    row = lax.broadcasted_iota(jnp.int32, (Lc, Lc), 0)
    col = lax.broadcasted_iota(jnp.int32, (Lc, Lc), 1)
    tri = (col <= row) if d == 0 else (col >= row)
    logf = _log_sigmoid(g)
    bcol = jnp.dot(jnp.where(tri, 1.0, 0.0), logf, precision=HIGHEST, preferred_element_type=F32)
    bT = bcol.T
    gT = g.T
    lane = lax.broadcasted_iota(jnp.int32, (1, ML_WIDTH), 1)
    kb = k.astype(BF16)
    vb = v.astype(BF16)
    Cst = C_s[d]
    nst = n_s[d]
    mst = m_s[d]
    qc = jnp.dot(q.astype(BF16), Cst.astype(BF16), preferred_element_type=F32)
    out = jnp.zeros((Lc, ML_WIDTH), F32)
    ws_all = jnp.zeros((Lc, ML_WIDTH), F32)
    wprev_all = jnp.zeros((1, ML_WIDTH), F32)
    mnew_all = jnp.zeros((1, ML_WIDTH), F32)
    for h in range(ML_HEADS):
        il = 8 * d + h
        fl = 8 * d + 4 + h
        head = (lane >= h * ML_DK) & (lane < (h + 1) * ML_DK)
        bc = bcol[:, fl:fl + 1]
        br = bT[fl:fl + 1, :]
        ir = gT[il:il + 1, :]
        ic = g[:, il:il + 1]
        mprev = mst[:, h * ML_DK:h * ML_DK + 1]
        dmat = jnp.where(tri, bc - br + ir, -jnp.inf)
        inter = bc + mprev
        mj = jnp.maximum(inter, jnp.max(dmat, axis=-1, keepdims=True))
        w_inter = jnp.exp(inter - mj)
        qh = jnp.where(head, q, 0.0)
        s = lax.dot_general(qh.astype(BF16), kb, (((1,), (1,)), ((), ())), preferred_element_type=F32)
        qk = s * jnp.exp(dmat - mj)
        pv = jnp.dot(qk.astype(BF16), vb, preferred_element_type=F32)
        qn = jnp.sum(qh * nst, axis=-1, keepdims=True)
        den = jnp.sum(qk, axis=-1, keepdims=True) + w_inter * qn
        denom = jnp.maximum(jnp.abs(den), jnp.exp(-mj))
        out = jnp.where(head, (pv + qc * w_inter) / denom, out)
        blast = bc[Lc - 1:Lc, :] if d == 0 else bc[0:1, :]
        dec = blast - bc + ic
        mnew = jnp.maximum(blast + mprev, jnp.max(dec, axis=0, keepdims=True))
        wprev = jnp.exp(blast + mprev - mnew)
        ws = jnp.exp(dec - mnew)
        ws_all = jnp.where(head, ws, ws_all)
        wprev_all = jnp.where(head, wprev, wprev_all)
        mnew_all = jnp.where(head, mnew, mnew_all)
    kw = k * ws_all
    upd = jnp.dot(kw.T.astype(BF16), vb, preferred_element_type=F32)
    r2 = lax.broadcasted_iota(jnp.int32, (ML_WIDTH, ML_WIDTH), 0) // ML_DK
    c2 = lax.broadcasted_iota(jnp.int32, (ML_WIDTH, ML_WIDTH), 1) // ML_DK
    C_s[d] = Cst * wprev_all + jnp.where(r2 == c2, upd, 0.0)
    n_s[d] = nst * wprev_all + jnp.sum(kw, axis=0, keepdims=True)
    m_s[d] = mnew_all
    return out


def _mlstm_kernel(qf_ref, kf_ref, vf_ref, gf_ref, qb_ref, kb_ref, vb_ref, gb_ref, bias_ref,
                  c0_ref, n0_ref, m0_ref, hf_ref, hb_ref, cT_ref, nT_ref, mT_ref, C_s, n_s, m_s):
    c = pl.program_id(1)

    @pl.when(c == 0)
    def _():
        C_s[...] = c0_ref[0]
        n_s[...] = n0_ref[0]
        m_s[...] = m0_ref[0]

    scale = ML_DK ** -0.5
    hf_ref[0] = _mlstm_direction(0, qf_ref[0] * scale, kf_ref[0], vf_ref[0],
                                 gf_ref[0] + bias_ref[...], C_s, n_s, m_s)
    hb_ref[0] = _mlstm_direction(1, qb_ref[0] * scale, kb_ref[0], vb_ref[0],
                                 gb_ref[0] + bias_ref[...], C_s, n_s, m_s)

    @pl.when(c == pl.num_programs(1) - 1)
    def _():
        cT_ref[0] = C_s[...]
        nT_ref[0] = n_s[...]
        mT_ref[0] = m_s[...]


def mlstm(z_ml, bias, state):
    B, L, _ = z_ml.shape
    Lc = ML_CHUNK
    nc = L // Lc
    c0, n0, m0 = state
    W = ML_WIDTH
    fwd = lambda j: pl.BlockSpec((1, Lc, W), lambda b, c: (b, c, j))
    bwd = lambda j: pl.BlockSpec((1, Lc, W), lambda b, c: (b, nc - 1 - c, j))
    gcol = 4 * W // 128
    st_c = pl.BlockSpec((1, 2, W, W), lambda b, c: (b, 0, 0, 0))
    st_v = pl.BlockSpec((1, 2, 1, W), lambda b, c: (b, 0, 0, 0))
    return pl.pallas_call(
        _mlstm_kernel,
        grid=(B, nc),
        in_specs=[fwd(0), fwd(1), fwd(2), pl.BlockSpec((1, Lc, 128), lambda b, c: (b, c, gcol)),
                  bwd(0), bwd(1), bwd(2), pl.BlockSpec((1, Lc, 128), lambda b, c: (b, nc - 1 - c, gcol)),
                  pl.BlockSpec((1, 128), lambda b, c: (0, 0)),
                  st_c, st_v, st_v],
        out_specs=[pl.BlockSpec((1, Lc, W), lambda b, c: (b, c, 0)),
                   pl.BlockSpec((1, Lc, W), lambda b, c: (b, nc - 1 - c, 0)),
                   st_c, st_v, st_v],
        out_shape=[jax.ShapeDtypeStruct((B, L, W), F32), jax.ShapeDtypeStruct((B, L, W), F32),
                   jax.ShapeDtypeStruct((B, 2, W, W), F32), jax.ShapeDtypeStruct((B, 2, 1, W), F32),
                   jax.ShapeDtypeStruct((B, 2, 1, W), F32)],
        scratch_shapes=[pltpu.VMEM((2, W, W), F32), pltpu.VMEM((2, 1, W), F32), pltpu.VMEM((2, 1, W), F32)],
        compiler_params=_cp("parallel", "arbitrary"),
        name="mlstm",
    )(z_ml, z_ml, z_ml, z_ml, z_ml, z_ml, z_ml, z_ml, bias, c0, n0, m0)


def _outproj_kernel(x_ref, pool_ref, attn_ref, hf_ref, hb_ref, op_ref, mlg_ref, hs_ref, hst_ref,
                    wp_ref, wa_ref, wm_ref, ga_ref, o_ref):
    h = hf_ref[0] + hb_ref[0]
    ss = jnp.dot(h * h, hs_ref[...], precision=HIGHEST, preferred_element_type=F32) * (1.0 / ML_DK)
    inv = jnp.dot(lax.rsqrt(ss + EPS), hst_ref[...], precision=HIGHEST, preferred_element_type=F32)
    op = op_ref[0]
    ml = h * inv * mlg_ref[...] * (1.0 / (1.0 + jnp.exp(-op)))
    mix = jnp.dot(pool_ref[0].astype(BF16), wp_ref[...], preferred_element_type=F32)
    mix += jnp.dot(attn_ref[0], wa_ref[...], preferred_element_type=F32)
    mix += jnp.dot(ml.astype(BF16), wm_ref[...], preferred_element_type=F32)
    o_ref[0] = x_ref[0] + ga_ref[0] * mix


def out_proj(x, pool, attn, hf, hb, z_ml, mlg, hs, hst, wp, wa, wm, gate):
    B, L, D = x.shape
    T = min(512, L)
    W = ML_WIDTH
    tile = lambda w, j=0: pl.BlockSpec((1, T, w), lambda b, i: (b, i, j))
    full = lambda shape: pl.BlockSpec(shape, lambda b, i: (0,) * len(shape))
    return pl.pallas_call(
        _outproj_kernel,
        grid=(B, L // T),
        in_specs=[tile(D), tile(W), tile(MLA_HEADS * HEAD_PAD), tile(W), tile(W), tile(W, 3),
                  full((1, W)), full(hs.shape), full(hst.shape),
                  full(wp.shape), full(wa.shape), full(wm.shape),
                  pl.BlockSpec((1, 1, D), lambda b, i: (b, 0, 0))],
        out_specs=tile(D),
        out_shape=jax.ShapeDtypeStruct((B, L, D), F32),
        compiler_params=_cp("parallel", "parallel"),
        name="out_proj",
    )(x, pool, attn, hf, hb, z_ml, mlg, hs, hst, wp, wa, wm, gate)


def _top16(s, lane):
    T = s.shape[0]
    out_lane = lax.broadcasted_iota(jnp.int32, (T, 128), 1)
    vals = jnp.full((T, 128), -jnp.inf, F32)
    idxs = jnp.zeros((T, 128), F32)
    big = float(s.shape[1])
    for k in range(PEER_TOPK):
        m = jnp.max(s, axis=-1, keepdims=True)
        idx = jnp.min(jnp.where(s == m, lane, big), axis=-1, keepdims=True)
        vals = jnp.where(out_lane == k, m, vals)
        idxs = jnp.where(out_lane == k, idx, idxs)
        s = jnp.where(lane == idx, -jnp.inf, s)
    return vals, idxs


def _topk_kernel(qp_ref, sk_ref, r1_ref, r2_ref, ids_ref, gate_ref):
    h = pl.program_id(1)
    T = qp_ref.shape[1]
    lane = lax.broadcasted_iota(jnp.int32, (T, PEER_KEYS), 1).astype(F32)
    valid = lane < PEER_TOPK
    tops = []
    for c in range(2):
        s = jnp.dot(qp_ref[c], sk_ref[c], precision=HIGHEST, preferred_element_type=F32)
        v, i = _top16(s, lane)
        tops.append((jnp.where(valid, v, 0.0), i))
    expand = lambda a, r_ref: jnp.dot(a, r_ref[...], precision=HIGHEST, preferred_element_type=F32)
    cand = expand(tops[0][0], r1_ref) + expand(tops[1][0], r2_ref)
    expert = expand(tops[0][1], r1_ref) * float(PEER_KEYS) + expand(tops[1][1], r2_ref)
    lane2 = lax.broadcasted_iota(jnp.int32, (T, 2 * PEER_KEYS), 1).astype(F32)
    best = jnp.full((T, PEER_KEYS), -jnp.inf, F32)
    eid = jnp.zeros((T, PEER_KEYS), F32)
    for k in range(PEER_TOPK):
        m = jnp.max(cand, axis=-1, keepdims=True)
        ci = jnp.min(jnp.where(cand == m, lane2, 2.0 * PEER_KEYS), axis=-1, keepdims=True)
        sel = lane2 == ci
        e = jnp.sum(jnp.where(sel, expert, 0.0), axis=-1, keepdims=True)
        best = jnp.where(lane == k, m, best)
        eid = jnp.where(lane == k, e, eid)
        cand = jnp.where(sel, -jnp.inf, cand)
    p = jnp.exp(best - jnp.max(best, axis=-1, keepdims=True))
    gate = p / jnp.sum(p, axis=-1, keepdims=True)

    @pl.when(h == 0)
    def _():
        ids_ref[...] = jnp.zeros_like(ids_ref)
        gate_ref[...] = jnp.zeros_like(gate_ref)

    shift = h * PEER_TOPK
    ids_ref[...] += pltpu.roll(eid.astype(jnp.int32), shift, 1)
    gate_ref[...] += pltpu.roll(gate, shift, 1)


def peer_topk(qp, skT, r1, r2):
    _, N, _ = qp.shape
    T = 256
    return pl.pallas_call(
        _topk_kernel,
        grid=(N // T, PEER_HEADS),
        in_specs=[pl.BlockSpec((2, T, PEER_KEYS), lambda i, h: (h, i, 0)),
                  pl.BlockSpec((2, PEER_KEYS, PEER_KEYS), lambda i, h: (h, 0, 0)),
                  pl.BlockSpec(r1.shape, lambda i, h: (0, 0)),
                  pl.BlockSpec(r2.shape, lambda i, h: (0, 0))],
        out_specs=[pl.BlockSpec((T, PEER_SLOTS), lambda i, h: (i, 0)),
                   pl.BlockSpec((T, PEER_SLOTS), lambda i, h: (i, 0))],
        out_shape=[jax.ShapeDtypeStruct((N, PEER_SLOTS), jnp.int32),
                   jax.ShapeDtypeStruct((N, PEER_SLOTS), F32)],
        compiler_params=_cp("parallel", "arbitrary"),
        name="peer_topk",
    )(qp, skT, r1, r2)


def _gather_rows(t, ids_s, tab_ref, stage):
    base = t * PEER_SLOTS
    for j in range(PEER_SLOTS):
        e = ids_s[base + j]
        stage[pl.ds(ROW_SUB * j, ROW_SUB), :] = tab_ref[e]


def _unpack_rows(x):
    lo = pltpu.bitcast(x << 16, F32)
    hi = pltpu.bitcast(x & jnp.uint32(0xFFFF0000), F32)
    return jnp.concatenate([lo, hi], axis=-1).astype(BF16)


def _load_ids(ids_hbm, ids_s, sem):
    n = ids_s.shape[0]
    off = pl.multiple_of(pl.program_id(0) * n, n)
    cp = pltpu.make_async_copy(ids_hbm.at[pl.ds(off, n)], ids_s, sem)
    cp.start()
    cp.wait()


def _peer_u_kernel(ids_hbm, h_ref, tab_ref, fold_ref, group_ref, a_ref, ids_s, stage, acc, sem):
    TB = PEER_TB
    R = PEER_SLOTS * ROW_SUB
    _load_ids(ids_hbm, ids_s, sem)
    hb = h_ref[...]
    hr = jnp.concatenate([hb[:, s * 256:(s + 1) * 256] for s in range(ROW_SUB)], axis=0)
    col_t = lax.broadcasted_iota(jnp.int32, (1, 8, ROW_SUB * TB), 2) % TB
    acc[...] = jnp.zeros_like(acc)

    def body(t, _):
        _gather_rows(t, ids_s, tab_ref, stage)
        x = _unpack_rows(stage[...])
        r = lax.dot_general(x, hr, (((1,), (1,)), ((), ())), preferred_element_type=F32)
        r3 = r.reshape(R // 8, 8, ROW_SUB * TB)
        acc[...] = jnp.where(col_t == t, r3, acc[...])
        return 0

    lax.fori_loop(0, TB, body, 0)
    a = acc[...].reshape(R, ROW_SUB * TB)
    row_s = lax.broadcasted_iota(jnp.int32, a.shape, 0) % ROW_SUB
    col_s = lax.broadcasted_iota(jnp.int32, a.shape, 1) // TB
    a = jnp.where(row_s == col_s, a, 0.0)
    a2t = lax.dot_general(fold_ref[...], a, (((1,), (1,)), ((), ())), precision=HIGHEST,
                          preferred_element_type=F32)
    a_ref[...] = jnp.dot(a2t, group_ref[...], precision=HIGHEST, preferred_element_type=F32)


def peer_scores(ids_flat, hmod, tab, fold, group):
    N, D = hmod.shape
    TB = PEER_TB
    R = PEER_SLOTS * ROW_SUB
    return pl.pallas_call(
        _peer_u_kernel,
        grid=(N // TB,),
        in_specs=[pl.BlockSpec(memory_space=pl.ANY),
                  pl.BlockSpec((TB, D), lambda i: (i, 0)),
                  pl.BlockSpec(memory_space=pltpu.VMEM),
                  pl.BlockSpec(fold.shape, lambda i: (0, 0)),
                  pl.BlockSpec(group.shape, lambda i: (0, 0))],
        out_specs=pl.BlockSpec((TB, PEER_SLOTS), lambda i: (i, 0)),
        out_shape=jax.ShapeDtypeStruct((N, PEER_SLOTS), F32),
        scratch_shapes=[pltpu.SMEM((TB * PEER_SLOTS,), jnp.int32),
                        pltpu.VMEM((R, 128), jnp.uint32),
                        pltpu.VMEM((R // 8, 8, ROW_SUB * TB), F32),
                        pltpu.SemaphoreType.DMA],
        compiler_params=_cp("arbitrary"),
        name="peer_scores",
    )(ids_flat, hmod, tab, fold, group)


def _peer_v_kernel(ids_hbm, a_ref, g_ref, tab_ref, rep_ref, o_ref, ids_s, stage, wexp, sem):
    _load_ids(ids_hbm, ids_s, sem)
    a = a_ref[...]
    w = g_ref[...] * (0.5 * a * (1.0 + lax.erf(a * (1.0 / math.sqrt(2.0)))))
    wexp[...] = jnp.dot(w, rep_ref[...], precision=HIGHEST, preferred_element_type=F32)
    R = PEER_SLOTS * ROW_SUB
    own = (lax.broadcasted_iota(jnp.int32, (8, R), 0) ==
           lax.broadcasted_iota(jnp.int32, (8, R), 1) % ROW_SUB)

    def body(t, _):
        _gather_rows(t, ids_s, tab_ref, stage)
        x = _unpack_rows(stage[...])
        wsel = jnp.where(own, wexp[pl.ds(t, 1), :], 0.0).astype(BF16)
        y = jnp.dot(wsel, x, preferred_element_type=F32)
        o_ref[t] = y[:ROW_SUB, :]
        return 0

    lax.fori_loop(0, PEER_TB, body, 0)


def peer_combine(ids_flat, a, gate, tab, rep):
    N = a.shape[0]
    TB = PEER_TB
    R = PEER_SLOTS * ROW_SUB
    return pl.pallas_call(
        _peer_v_kernel,
        grid=(N // TB,),
        in_specs=[pl.BlockSpec(memory_space=pl.ANY),
                  pl.BlockSpec((TB, PEER_SLOTS), lambda i: (i, 0)),
                  pl.BlockSpec((TB, PEER_SLOTS), lambda i: (i, 0)),
                  pl.BlockSpec(memory_space=pltpu.VMEM),
                  pl.BlockSpec(rep.shape, lambda i: (0, 0))],
        out_specs=pl.BlockSpec((TB, ROW_SUB, 256), lambda i: (i, 0, 0)),
        out_shape=jax.ShapeDtypeStruct((N, ROW_SUB, 256), F32),
        scratch_shapes=[pltpu.SMEM((TB * PEER_SLOTS,), jnp.int32),
                        pltpu.VMEM((R, 128), jnp.uint32),
                        pltpu.VMEM((TB, R), F32),
                        pltpu.SemaphoreType.DMA],
        compiler_params=_cp("arbitrary"),
        name="peer_combine",
    )(ids_flat, a, gate, tab, rep)


def _residual_kernel(x_ref, y_ref, g_ref, o_ref):
    o_ref[0] = x_ref[0] + g_ref[0] * y_ref[0]


def gated_residual(x, y, gate):
    B, L, D = x.shape
    T = min(1024, L)
    tile = pl.BlockSpec((1, T, D), lambda b, i: (b, i, 0))
    return pl.pallas_call(
        _residual_kernel,
        grid=(B, L // T),
        in_specs=[tile, tile, pl.BlockSpec((1, 1, D), lambda b, i: (b, 0, 0))],
        out_specs=tile,
        out_shape=jax.ShapeDtypeStruct((B, L, D), F32),
        compiler_params=_cp("parallel", "parallel"),
        name="gated_residual",
    )(x, y, gate)


def _pad_last(a, n):
    return jnp.pad(a, [(0, 0)] * (a.ndim - 1) + [(0, n - a.shape[-1])])


def _pack_table(tab):
    bits = lax.bitcast_convert_type(tab.astype(BF16), jnp.uint16).astype(jnp.uint32)
    bits = bits.reshape(tab.shape[0], ROW_SUB, 2, 128)
    return bits[:, :, 0, :] | (bits[:, :, 1, :] << 16)


def _layer_weights(l, norm1_gain, norm2_gain, w_in, pool_w, pool_scale, mla_q_norm, mla_kv_norm, w_uq, w_ukv,
                   q_norm, k_norm, ml_gate_bias, ml_out_norm, w_out, peer_wq, peer_subkeys, peer_u, peer_v):
    parts, start = [], 0
    for size in IN_SIZES:
        parts.append(w_in[l][:, start:start + size])
        start += size
    w_in_r = jnp.concatenate([parts[0], parts[1], parts[2], _pad_last(parts[3], 128), parts[4], parts[5],
                              parts[6], parts[7], _pad_last(parts[8], 128)], axis=1).astype(BF16)
    wbd = jnp.zeros((POOL_WIDTH, POOL_WIDTH), F32)
    for g in range(len(POOL_WINDOWS)):
        sl = slice(g * POOL_GROUP, (g + 1) * POOL_GROUP)
        wbd = wbd.at[sl, sl].set(pool_w[l, g])
    wq = _pad_last(w_uq[l].reshape(MLA_Q_RANK, MLA_HEADS, MLA_QK), HEAD_PAD).reshape(MLA_Q_RANK, -1)
    ukv = w_ukv[l].reshape(MLA_KV_RANK, MLA_HEADS, MLA_NOPE + MLA_V)
    wk = _pad_last(ukv[..., :MLA_NOPE], HEAD_PAD).reshape(MLA_KV_RANK, -1)
    wv = _pad_last(ukv[..., MLA_NOPE:], HEAD_PAD).reshape(MLA_KV_RANK, -1)
    r = jnp.arange(128)[:, None]
    cidx = jnp.arange(MLA_HEADS * HEAD_PAD)[None, :]
    pk = ((r < MLA_ROPE) & (cidx % HEAD_PAD == MLA_NOPE + r)).astype(F32)
    lane = jnp.arange(ML_WIDTH)
    hs = (lane[:, None] // ML_DK == jnp.arange(128)[None, :]).astype(F32)
    wo = w_out[l]
    wa = jnp.pad(wo[POOL_WIDTH:POOL_WIDTH + MLA_HEADS * MLA_V].reshape(MLA_HEADS, MLA_V, D_MODEL),
                 ((0, 0), (0, HEAD_PAD - MLA_V), (0, 0))).reshape(MLA_HEADS * HEAD_PAD, D_MODEL)
    return dict(
        n1=norm1_gain[l], n2=norm2_gain[l], w_in=w_in_r,
        wbd=wbd.astype(BF16), pool_scale=pool_scale[l],
        qan=mla_q_norm[l], kvan=mla_kv_norm[l],
        wq=wq.astype(BF16), wk=wk.astype(BF16), wv=wv.astype(BF16), pk=pk,
        qg=_pad_last(q_norm[l], HEAD_PAD).reshape(1, HEAD_PAD), kg=_pad_last(k_norm[l], HEAD_PAD).reshape(1, HEAD_PAD),
        gate_bias=_pad_last(ml_gate_bias[l], 128).reshape(1, 128),
        mlg=ml_out_norm[l].reshape(1, ML_WIDTH), hs=hs, hst=hs.T,
        wp=wo[:POOL_WIDTH].astype(BF16), wa=wa.astype(BF16),
        wm=wo[POOL_WIDTH + MLA_HEADS * MLA_V:].astype(BF16),
        peer_wq=peer_wq[l].astype(BF16),
        skT=peer_subkeys[l].reshape(2 * PEER_HEADS, PEER_KEYS, PEER_KEYS).transpose(0, 2, 1),
        u_tab=_pack_table(peer_u[l]), v_tab=_pack_table(peer_v[l]),
    )


def _peer_constants():
    TB = PEER_TB
    R = PEER_SLOTS * ROW_SUB
    k = jnp.arange(PEER_KEYS)[:, None]
    c = jnp.arange(2 * PEER_KEYS)[None, :]
    r1 = ((k < PEER_TOPK) & (c // PEER_TOPK == k)).astype(F32)
    r2 = ((k < PEER_TOPK) & (c % PEER_TOPK == k)).astype(F32)
    fold = (jnp.arange(ROW_SUB * TB)[None, :] % TB == jnp.arange(TB)[:, None]).astype(F32)
    group = (jnp.arange(R)[:, None] // ROW_SUB == jnp.arange(PEER_SLOTS)[None, :]).astype(F32)
    return r1, r2, fold, group, group.T


def _rope_tables(n):
    n_rows = n // GRID_W
    row = jnp.repeat(jnp.arange(n_rows), GRID_W, total_repeat_length=n).astype(F32)
    col = (jnp.arange(n) % GRID_W).astype(F32)
    per_axis = MLA_ROPE // 2
    freqs = ROPE_BASE ** (-jnp.arange(0, per_axis, 2, dtype=F32) / per_axis)
    ang = jnp.concatenate([row[:, None] * freqs, col[:, None] * freqs], axis=-1)
    c, s = jnp.cos(ang), jnp.sin(ang)
    cos = jnp.concatenate([jnp.ones((n, MLA_NOPE), F32), c, c, jnp.ones((n, HEAD_PAD - MLA_QK), F32)], axis=-1)
    sin = jnp.concatenate([jnp.zeros((n, MLA_NOPE), F32), -s, s, jnp.zeros((n, HEAD_PAD - MLA_QK), F32)], axis=-1)
    return cos, sin


def _peer_ffn(x, w, shift, scale, gate, consts):
    B, L, D = x.shape
    r1, r2, fold, group, rep = consts
    qp, hmod = peer_query(x, w["n2"], shift, scale, w["peer_wq"])
    ids, gates = peer_topk(qp, w["skT"], r1, r2)
    ids_flat = ids.reshape(-1)
    a = peer_scores(ids_flat, hmod, w["u_tab"], fold, group)
    y = peer_combine(ids_flat, a, gates, w["v_tab"], rep)
    return gated_residual(x, y.reshape(B, L, D), gate)


def kernel(x, c, ctx, c_ctx, norm1_gain, norm2_gain, w_ada, b_ada, w_in, pool_w, pool_scale, mla_q_norm,
           mla_kv_norm, w_uq, w_ukv, q_norm, k_norm, ml_gate_bias, ml_out_norm, w_out, peer_wq, peer_subkeys,
           peer_u, peer_v):
    B, S, D = x.shape
    Lc_ctx = ctx.shape[1]
    depth = w_in.shape[0]
    cos_x, sin_x = _rope_tables(S)
    cos_c = jnp.ones((Lc_ctx, HEAD_PAD), F32)
    sin_c = jnp.zeros((Lc_ctx, HEAD_PAD), F32)
    consts = _peer_constants()
    cond8 = jnp.zeros((8, D), F32).at[:B].set(c).at[B].set(c_ctx)
    W = ML_WIDTH
    zero_state = (jnp.zeros((B, 2, W, W), F32), jnp.zeros((B, 2, 1, W), F32),
                  jnp.full((B, 2, 1, W), NEG_INIT, F32))
    lk = S + Lc_ctx
    tc_x = next(t for t in (640, 512, 384, 256, 128) if lk % t == 0)
    x_ctx = ctx
    for l in range(depth):
        last = l == depth - 1
        w = _layer_weights(l, norm1_gain, norm2_gain, w_in, pool_w, pool_scale, mla_q_norm, mla_kv_norm, w_uq,
                           w_ukv, q_norm, k_norm, ml_gate_bias, ml_out_norm, w_out, peer_wq, peer_subkeys,
                           peer_u, peer_v)
        mods = ada_mod(cond8, w_ada[l], b_ada[l])
        m_x = [mods[:B, i * D:(i + 1) * D].reshape(B, 1, D) for i in range(6)]
        m_c = [jnp.broadcast_to(mods[B, i * D:(i + 1) * D].reshape(1, 1, D), (B, 1, D)) for i in range(6)]

        zp_c, zm_c, zl_c = in_proj(x_ctx, w["n1"], m_c[0], m_c[1], w["w_in"])
        qc, kc, vc = mla_qkv(zm_c, w["qan"], w["kvan"], w["wq"], w["wk"], w["wv"], w["pk"], w["qg"], w["kg"],
                             cos_c, sin_c)
        hf_c, hb_c, cT, nT, mT = mlstm(zl_c, w["gate_bias"], zero_state)
        if not last:
            pool_c = pool_mixer(zp_c, w["wbd"], w["pool_scale"])
            attn_c = attention(qc, kc, vc, Lc_ctx)
            xc = out_proj(x_ctx, pool_c, attn_c, hf_c, hb_c, zl_c, w["mlg"], w["hs"], w["hst"],
                          w["wp"], w["wa"], w["wm"], m_c[2])
            x_ctx_new = _peer_ffn(xc, w, m_c[3], m_c[4], m_c[5], consts)

        zp, zm, zl = in_proj(x, w["n1"], m_x[0], m_x[1], w["w_in"])
        pool_x = pool_mixer(zp, w["wbd"], w["pool_scale"])
        q, k, v = mla_qkv(zm, w["qan"], w["kvan"], w["wq"], w["wk"], w["wv"], w["pk"], w["qg"], w["kg"],
                          cos_x, sin_x)
        k_all = jnp.concatenate([k, kc], axis=2)
        v_all = jnp.concatenate([v, vc], axis=2)
        attn_x = attention(q, k_all, v_all, tc_x)
        hf, hb, _, _, _ = mlstm(zl, w["gate_bias"], (cT, nT, mT))
        x = out_proj(x, pool_x, attn_x, hf, hb, zl, w["mlg"], w["hs"], w["hst"],
                     w["wp"], w["wa"], w["wm"], m_x[2])
        x = _peer_ffn(x, w, m_x[3], m_x[4], m_x[5], consts)
        if not last:
            x_ctx = x_ctx_new
    return x
```

```python
import functools
import math

import jax
import jax.numpy as jnp
from jax import lax
from jax.experimental import pallas as pl
from jax.experimental.pallas import tpu as pltpu

F32 = jnp.float32
BF16 = jnp.bfloat16
HIGHEST = lax.Precision.HIGHEST

EPS = 1e-6
D_MODEL = 1024
GRID_W = 64
ROPE_BASE = 10000.0
POOL_WINDOWS = (2, 4, 8, 16)
POOL_GROUP = 64
POOL_WIDTH = 256
MLA_HEADS = 8
MLA_NOPE = 64
MLA_ROPE = 32
MLA_QK = 96
MLA_V = 64
MLA_Q_RANK = 384
MLA_KV_RANK = 256
MLA_SCALE = MLA_QK ** -0.5
Q_PRESCALE = MLA_SCALE * math.log2(math.e)
HEAD_PAD = 128
ATT_TQ = 512
ATT_TC = 512
VT_ROWS = 80
ML_HEADS = 4
ML_DK = 64
ML_WIDTH = 256
ML_CHUNK = 128
NEG_INIT = -1e30
IN_SIZES = (256, 384, 256, 32, 256, 256, 256, 256, 16)
PEER_HEADS = 8
PEER_KEYS = 128
PEER_TOPK = 16
PEER_SLOTS = PEER_HEADS * PEER_TOPK
PEER_TB = 128
ROW_SUB = 4

Z_POOL, Z_MLA, Z_ML = 256, 768, 1152
VMEM_LIMIT = 56 * 1024 * 1024


def _cp(*sem, vmem=VMEM_LIMIT):
    return pltpu.CompilerParams(dimension_semantics=sem, vmem_limit_bytes=vmem)


def _ada_kernel(c_ref, w_ref, b_ref, o_ref):
    c = c_ref[...]
    s = c * (1.0 / (1.0 + jnp.exp(-c)))
    o_ref[...] = jnp.dot(s, w_ref[...], precision=HIGHEST, preferred_element_type=F32) + b_ref[...]


def ada_mod(cond8, w, b):
    n = w.shape[1]
    tn = n // 4
    return pl.pallas_call(
        _ada_kernel,
        grid=(n // tn,),
        in_specs=[pl.BlockSpec((8, D_MODEL), lambda j: (0, 0)),
                  pl.BlockSpec((D_MODEL, tn), lambda j: (0, j)),
                  pl.BlockSpec((1, tn), lambda j: (0, j))],
        out_specs=pl.BlockSpec((8, tn), lambda j: (0, j)),
        out_shape=jax.ShapeDtypeStruct((8, n), F32),
        compiler_params=_cp("arbitrary"),
        name="ada_mod",
    )(cond8, w, b.reshape(1, n))


def _modulated(x, gain, shift, scale):
    ms = jnp.mean(x * x, axis=-1, keepdims=True)
    y = x * lax.rsqrt(ms + EPS) * gain
    return y * (1.0 + scale) + shift


def _inproj_kernel(x_ref, gain_ref, shift_ref, scale_ref, w_ref, zp_ref, zm_ref, zl_ref):
    h = _modulated(x_ref[0], gain_ref[...], shift_ref[0], scale_ref[0])
    res = jnp.dot(h.astype(BF16), w_ref[...], preferred_element_type=F32)
    zp_ref[0] = res[:, :Z_POOL]
    zm_ref[0] = res[:, Z_POOL:Z_POOL + Z_MLA]
    zl_ref[0] = res[:, Z_POOL + Z_MLA:]


def in_proj(x, gain, shift, scale, w):
    B, L, D = x.shape
    tm = min(512, L)
    n = w.shape[1]
    vec = pl.BlockSpec((1, 1, D), lambda b, i: (b, 0, 0))
    return pl.pallas_call(
        _inproj_kernel,
        grid=(B, L // tm),
        in_specs=[pl.BlockSpec((1, tm, D), lambda b, i: (b, i, 0)),
                  pl.BlockSpec((1, D), lambda b, i: (0, 0)),
                  vec, vec,
                  pl.BlockSpec((D, n), lambda b, i: (0, 0))],
        out_specs=[pl.BlockSpec((1, tm, Z_POOL), lambda b, i: (b, i, 0)),
                   pl.BlockSpec((1, tm, Z_MLA), lambda b, i: (b, i, 0)),
                   pl.BlockSpec((1, tm, Z_ML), lambda b, i: (b, i, 0))],
        out_shape=[jax.ShapeDtypeStruct((B, L, Z_POOL), F32),
                   jax.ShapeDtypeStruct((B, L, Z_MLA), F32),
                   jax.ShapeDtypeStruct((B, L, Z_ML), F32)],
        compiler_params=_cp("parallel", "parallel"),
        name="in_proj",
    )(x, gain.reshape(1, D), shift, scale, w)


def _peerq_kernel(x_ref, gain_ref, shift_ref, scale_ref, w_ref, qp_ref, h_ref):
    h = _modulated(x_ref[0], gain_ref[...], shift_ref[0], scale_ref[0])
    h_ref[...] = h.T.astype(BF16)
    res = jnp.dot(h.astype(BF16), w_ref[...], preferred_element_type=F32)
    for g in range(2 * PEER_HEADS):
        qp_ref[g] = res[:, g * PEER_KEYS:(g + 1) * PEER_KEYS].T


def peer_query(x, gain, shift, scale, w):
    B, L, D = x.shape
    tm = min(512, L)
    nb = L // tm
    n = w.shape[1]
    vec = pl.BlockSpec((1, 1, D), lambda b, i: (b, 0, 0))
    return pl.pallas_call(
        _peerq_kernel,
        grid=(B, nb),
        in_specs=[pl.BlockSpec((1, tm, D), lambda b, i: (b, i, 0)),
                  pl.BlockSpec((1, D), lambda b, i: (0, 0)),
                  vec, vec,
                  pl.BlockSpec((D, n), lambda b, i: (0, 0))],
        out_specs=[pl.BlockSpec((2 * PEER_HEADS, PEER_KEYS, tm), lambda b, i: (0, 0, b * nb + i)),
                   pl.BlockSpec((D, tm), lambda b, i: (0, b * nb + i))],
        out_shape=[jax.ShapeDtypeStruct((2 * PEER_HEADS, PEER_KEYS, B * L), F32),
                   jax.ShapeDtypeStruct((D, B * L), BF16)],
        compiler_params=_cp("parallel", "parallel"),
        name="peer_query",
    )(x, gain.reshape(1, D), shift, scale, w)


def _pool_kernel(p_ref, c_ref, n_ref, wbd_ref, sc_ref, o_ref, *, L, T):
    i = pl.program_id(1)
    cur = c_ref[0]
    u3 = jnp.concatenate([p_ref[0], cur, n_ref[0]], axis=0).astype(BF16)
    t = i * T + lax.broadcasted_iota(jnp.int32, (T, 3 * T), 0)
    s = (i - 1) * T + lax.broadcasted_iota(jnp.int32, (T, 3 * T), 1)
    lane = lax.broadcasted_iota(jnp.int32, (T, POOL_WIDTH), 1)
    trow = i * T + lax.broadcasted_iota(jnp.int32, (T, POOL_WIDTH), 0)
    win = jnp.zeros((T, POOL_WIDTH), F32)
    for g, w in enumerate(POOL_WINDOWS):
        lo = jnp.maximum(t - w // 2, 0)
        hi = jnp.minimum(t + w // 2, L)
        band = jnp.where((s >= lo) & (s < hi), 1.0, 0.0).astype(BF16)
        ws = jnp.dot(band, u3, preferred_element_type=F32)
        cnt = (jnp.minimum(trow + w // 2, L) - jnp.maximum(trow - w // 2, 0)).astype(F32)
        in_group = (lane >= g * POOL_GROUP) & (lane < (g + 1) * POOL_GROUP)
        win = jnp.where(in_group, ws / cnt, win)
    d = win - cur
    y = jnp.dot(d.astype(BF16), wbd_ref[...], preferred_element_type=F32)
    o_ref[0] = y * sc_ref[...]


def pool_mixer(z_pool, wbd, scale):
    B, L, C = z_pool.shape
    T = 256
    nb = L // T
    return pl.pallas_call(
        functools.partial(_pool_kernel, L=L, T=T),
        grid=(B, nb),
        in_specs=[pl.BlockSpec((1, T, C), lambda b, i: (b, jnp.maximum(i - 1, 0), 0)),
                  pl.BlockSpec((1, T, C), lambda b, i: (b, i, 0)),
                  pl.BlockSpec((1, T, C), lambda b, i: (b, jnp.minimum(i + 1, nb - 1), 0)),
                  pl.BlockSpec((C, C), lambda b, i: (0, 0)),
                  pl.BlockSpec((1, C), lambda b, i: (0, 0))],
        out_specs=pl.BlockSpec((1, T, C), lambda b, i: (b, i, 0)),
        out_shape=jax.ShapeDtypeStruct((B, L, C), F32),
        compiler_params=_cp("parallel", "parallel"),
        name="pool_mixer",
    )(z_pool, z_pool, z_pool, wbd, scale.reshape(1, C))


def _rms(x, gain, n):
    ss = jnp.sum(x * x, axis=-1, keepdims=True) * (1.0 / n)
    return x * lax.rsqrt(ss + EPS) * gain


def _mla_kernel(z_ref, qan_ref, kvan_ref, wq_ref, wk_ref, wv_ref, pk_ref, qg_ref, kg_ref,
                cos_ref, sin_ref, q_out, k_out, v_out):
    z = z_ref[0]
    zq = z[:, :MLA_Q_RANK]
    zkv = z[:, MLA_Q_RANK:MLA_Q_RANK + MLA_KV_RANK]
    zkr = z[:, MLA_Q_RANK + MLA_KV_RANK:]
    nq = _rms(zq, qan_ref[...], MLA_Q_RANK).astype(BF16)
    nkv = _rms(zkv, kvan_ref[...], MLA_KV_RANK).astype(BF16)
    qp = jnp.dot(nq, wq_ref[...], preferred_element_type=F32)
    kp = jnp.dot(nkv, wk_ref[...], preferred_element_type=F32)
    kp = kp + jnp.dot(zkr, pk_ref[...], precision=HIGHEST, preferred_element_type=F32)
    vp = jnp.dot(nkv, wv_ref[...], preferred_element_type=F32)
    cos = cos_ref[...]
    sin = sin_ref[...]
    lane = lax.broadcasted_iota(jnp.int32, cos.shape, 1)
    first_half = lane < MLA_NOPE + MLA_ROPE // 2
    extra = (VT_ROWS - MLA_V, z.shape[0])
    ones_rows = jnp.where(lax.broadcasted_iota(jnp.int32, extra, 0) == 0, 1.0, 0.0)
    for h in range(MLA_HEADS):
        sl = slice(h * HEAD_PAD, (h + 1) * HEAD_PAD)
        for src, gain_ref, out, mult in ((qp, qg_ref, q_out, Q_PRESCALE), (kp, kg_ref, k_out, 1.0)):
            xn = _rms(src[:, sl], gain_ref[...], MLA_QK)
            partner = jnp.where(first_half, pltpu.roll(xn, HEAD_PAD - MLA_ROPE // 2, 1),
                                pltpu.roll(xn, MLA_ROPE // 2, 1))
            xr = xn * cos + partner * sin
            out[0, h] = (xr * mult).astype(BF16)
        v_out[0, h] = jnp.concatenate([vp[:, sl].T[:MLA_V, :], ones_rows], axis=0).astype(BF16)


def mla_qkv(z_mla, qan, kvan, wq, wk, wv, pk, qg, kg, cos, sin):
    B, L, _ = z_mla.shape
    T = 256
    full = lambda shape: pl.BlockSpec(shape, lambda b, i: (0,) * len(shape))
    head_out = pl.BlockSpec((1, MLA_HEADS, T, HEAD_PAD), lambda b, i: (b, 0, i, 0))
    out_sds = jax.ShapeDtypeStruct((B, MLA_HEADS, L, HEAD_PAD), BF16)
    vt_out = pl.BlockSpec((1, MLA_HEADS, VT_ROWS, T), lambda b, i: (b, 0, 0, i))
    vt_sds = jax.ShapeDtypeStruct((B, MLA_HEADS, VT_ROWS, L), BF16)
    return pl.pallas_call(
        _mla_kernel,
        grid=(B, L // T),
        in_specs=[pl.BlockSpec((1, T, Z_MLA), lambda b, i: (b, i, 0)),
                  full((1, MLA_Q_RANK)), full((1, MLA_KV_RANK)),
                  full(wq.shape), full(wk.shape), full(wv.shape), full(pk.shape),
                  full((1, HEAD_PAD)), full((1, HEAD_PAD)),
                  pl.BlockSpec((T, HEAD_PAD), lambda b, i: (i, 0)),
                  pl.BlockSpec((T, HEAD_PAD), lambda b, i: (i, 0))],
        out_specs=[head_out, head_out, vt_out],
        out_shape=[out_sds, out_sds, vt_sds],
        compiler_params=_cp("parallel", "parallel"),
        name="mla_qkv",
    )(z_mla, qan.reshape(1, -1), kvan.reshape(1, -1), wq, wk, wv, pk, qg, kg, cos, sin)


def _flash_kernel(q_ref, kc_ref, vct_ref, *rest, nchunks):
    if nchunks:
        k_ref, vt_ref, o_ref, s_s, m_s, acc_s = rest
    else:
        o_ref, s_s, m_s, acc_s = rest
    nc = kc_ref.shape[2]

    def scores(h, kblk):
        n = kblk.shape[0]
        s_s[h, :n] = lax.dot_general(kblk, q_ref[0, h], (((1,), (1,)), ((), ())), preferred_element_type=F32)

    def update(h, n, vtblk):
        st = s_s[h, :n]
        m = m_s[h]
        m_new = jnp.maximum(m, jnp.max(st, axis=0, keepdims=True))
        p = jnp.exp2(st - m_new).astype(BF16)
        m_s[h] = m_new
        acc_s[h] = jnp.exp2(m - m_new) * acc_s[h] + jnp.dot(vtblk, p, preferred_element_type=F32)

    m_s[...] = jnp.full(m_s.shape, -jnp.inf, F32)
    acc_s[...] = jnp.zeros(acc_s.shape, F32)
    scores(0, kc_ref[0, 0])
    scores(1, kc_ref[0, 1])
    update(0, nc, vct_ref[0, 0])
    if nchunks:
        scores(0, k_ref[0, 0, pl.ds(0, ATT_TC), :])
    update(1, nc, vct_ref[0, 1])
    if nchunks:
        def body(c, _):
            off = pl.multiple_of(c * ATT_TC, ATT_TC)
            nxt = pl.multiple_of(jnp.minimum(c + 1, nchunks - 1) * ATT_TC, ATT_TC)
            scores(1, k_ref[0, 1, pl.ds(off, ATT_TC), :])
            update(0, ATT_TC, vt_ref[0, 0, :, pl.ds(off, ATT_TC)])
            scores(0, k_ref[0, 0, pl.ds(nxt, ATT_TC), :])
            update(1, ATT_TC, vt_ref[0, 1, :, pl.ds(off, ATT_TC)])
            return 0
        lax.fori_loop(0, nchunks, body, 0)
    o = jnp.concatenate([acc_s[h, :MLA_V] / acc_s[h, MLA_V:MLA_V + 1] for h in range(2)], axis=0)
    o_ref[0] = o.T.astype(BF16)


def attention(q, kc, vct, k=None, vt=None):
    B, H, Lq, _ = q.shape
    Lc = kc.shape[2]
    tq = min(ATT_TQ, Lq)
    pair4 = lambda n, d: pl.BlockSpec((1, 2, n, d), lambda b, h, i: (b, h, 0, 0))
    in_specs = [pl.BlockSpec((1, 2, tq, HEAD_PAD), lambda b, h, i: (b, h, i, 0)),
                pair4(Lc, HEAD_PAD), pair4(VT_ROWS, Lc)]
    args = [q, kc, vct]
    nchunks = 0
    if k is not None:
        Lk = k.shape[2]
        nchunks = Lk // ATT_TC
        in_specs += [pair4(Lk, HEAD_PAD), pair4(VT_ROWS, Lk)]
        args += [k, vt]
    return pl.pallas_call(
        functools.partial(_flash_kernel, nchunks=nchunks),
        grid=(B, H // 2, Lq // tq),
        in_specs=in_specs,
        out_specs=pl.BlockSpec((1, tq, 2 * MLA_V), lambda b, h, i: (b, i, h)),
        out_shape=jax.ShapeDtypeStruct((B, Lq, H * MLA_V), BF16),
        scratch_shapes=[pltpu.VMEM((2, max(ATT_TC, Lc), tq), F32), pltpu.VMEM((2, 1, tq), F32),
                        pltpu.VMEM((2, VT_ROWS, tq), F32)],
        compiler_params=_cp("parallel", "parallel", "arbitrary"),
        name="attention",
    )(*args)


def _log_sigmoid(x):
    return jnp.minimum(x, 0.0) - jnp.log(1.0 + jnp.exp(-jnp.abs(x)))


def _mlstm_direction(d, q, k, v, g, C_s, n_s, m_s):
    Lc = q.shape[0]
    row = lax.broadcasted_iota(jnp.int32, (Lc, Lc), 0)
    col = lax.broadcasted_iota(jnp.int32, (Lc, Lc), 1)
    tri = (col <= row) if d == 0 else (col >= row)
    logf = _log_sigmoid(g)
    bcol = jnp.dot(jnp.where(tri, 1.0, 0.0), logf, precision=HIGHEST, preferred_element_type=F32)
    bT = bcol.T
    gT = g.T
    lane = lax.broadcasted_iota(jnp.int32, (1, ML_WIDTH), 1)
    kb = k.astype(BF16)
    vb = v.astype(BF16)
    Cst = C_s[d]
    nst = n_s[d]
    mst = m_s[d]
    qc = jnp.dot(q.astype(BF16), Cst.astype(BF16), preferred_element_type=F32)
    out = jnp.zeros((Lc, ML_WIDTH), F32)
    ws_all = jnp.zeros((Lc, ML_WIDTH), F32)
    wprev_all = jnp.zeros((1, ML_WIDTH), F32)
    mnew_all = jnp.zeros((1, ML_WIDTH), F32)
    for h in range(ML_HEADS):
        il = 8 * d + h
        fl = 8 * d + 4 + h
        head = (lane >= h * ML_DK) & (lane < (h + 1) * ML_DK)
        bc = bcol[:, fl:fl + 1]
        br = bT[fl:fl + 1, :]
        ir = gT[il:il + 1, :]
        ic = g[:, il:il + 1]
        mprev = mst[:, h * ML_DK:h * ML_DK + 1]
        dmat = jnp.where(tri, bc - br + ir, -jnp.inf)
        inter = bc + mprev
        mj = jnp.maximum(inter, jnp.max(dmat, axis=-1, keepdims=True))
        w_inter = jnp.exp(inter - mj)
        qh = jnp.where(head, q, 0.0)
        s = lax.dot_general(qh.astype(BF16), kb, (((1,), (1,)), ((), ())), preferred_element_type=F32)
        qk = s * jnp.exp(dmat - mj)
        pv = jnp.dot(qk.astype(BF16), vb, preferred_element_type=F32)
        qn = jnp.sum(qh * nst, axis=-1, keepdims=True)
        den = jnp.sum(qk, axis=-1, keepdims=True) + w_inter * qn
        denom = jnp.maximum(jnp.abs(den), jnp.exp(-mj))
        out = jnp.where(head, (pv + qc * w_inter) / denom, out)
        blast = bc[Lc - 1:Lc, :] if d == 0 else bc[0:1, :]
        dec = blast - bc + ic
        mnew = jnp.maximum(blast + mprev, jnp.max(dec, axis=0, keepdims=True))
        wprev = jnp.exp(blast + mprev - mnew)
        ws = jnp.exp(dec - mnew)
        ws_all = jnp.where(head, ws, ws_all)
        wprev_all = jnp.where(head, wprev, wprev_all)
        mnew_all = jnp.where(head, mnew, mnew_all)
    kw = k * ws_all
    upd = jnp.dot(kw.T.astype(BF16), vb, preferred_element_type=F32)
    r2 = lax.broadcasted_iota(jnp.int32, (ML_WIDTH, ML_WIDTH), 0) // ML_DK
    c2 = lax.broadcasted_iota(jnp.int32, (ML_WIDTH, ML_WIDTH), 1) // ML_DK
    C_s[d] = Cst * wprev_all + jnp.where(r2 == c2, upd, 0.0)
    n_s[d] = nst * wprev_all + jnp.sum(kw, axis=0, keepdims=True)
    m_s[d] = mnew_all
    return out


def _mlstm_kernel(qf_ref, kf_ref, vf_ref, gf_ref, qb_ref, kb_ref, vb_ref, gb_ref, bias_ref,
                  c0_ref, n0_ref, m0_ref, hf_ref, hb_ref, cT_ref, nT_ref, mT_ref, C_s, n_s, m_s):
    c = pl.program_id(1)

    @pl.when(c == 0)
    def _():
        C_s[...] = c0_ref[0]
        n_s[...] = n0_ref[0]
        m_s[...] = m0_ref[0]

    scale = ML_DK ** -0.5
    hf_ref[0] = _mlstm_direction(0, qf_ref[0] * scale, kf_ref[0], vf_ref[0],
                                 gf_ref[0] + bias_ref[...], C_s, n_s, m_s)
    hb_ref[0] = _mlstm_direction(1, qb_ref[0] * scale, kb_ref[0], vb_ref[0],
                                 gb_ref[0] + bias_ref[...], C_s, n_s, m_s)

    @pl.when(c == pl.num_programs(1) - 1)
    def _():
        cT_ref[0] = C_s[...]
        nT_ref[0] = n_s[...]
        mT_ref[0] = m_s[...]


def mlstm(z_ml, bias, state):
    B, L, _ = z_ml.shape
    Lc = ML_CHUNK
    nc = L // Lc
    c0, n0, m0 = state
    W = ML_WIDTH
    fwd = lambda j: pl.BlockSpec((1, Lc, W), lambda b, c: (b, c, j))
    bwd = lambda j: pl.BlockSpec((1, Lc, W), lambda b, c: (b, nc - 1 - c, j))
    gcol = 4 * W // 128
    st_c = pl.BlockSpec((1, 2, W, W), lambda b, c: (b, 0, 0, 0))
    st_v = pl.BlockSpec((1, 2, 1, W), lambda b, c: (b, 0, 0, 0))
    return pl.pallas_call(
        _mlstm_kernel,
        grid=(B, nc),
        in_specs=[fwd(0), fwd(1), fwd(2), pl.BlockSpec((1, Lc, 128), lambda b, c: (b, c, gcol)),
                  bwd(0), bwd(1), bwd(2), pl.BlockSpec((1, Lc, 128), lambda b, c: (b, nc - 1 - c, gcol)),
                  pl.BlockSpec((1, 128), lambda b, c: (0, 0)),
                  st_c, st_v, st_v],
        out_specs=[pl.BlockSpec((1, Lc, W), lambda b, c: (b, c, 0)),
                   pl.BlockSpec((1, Lc, W), lambda b, c: (b, nc - 1 - c, 0)),
                   st_c, st_v, st_v],
        out_shape=[jax.ShapeDtypeStruct((B, L, W), F32), jax.ShapeDtypeStruct((B, L, W), F32),
                   jax.ShapeDtypeStruct((B, 2, W, W), F32), jax.ShapeDtypeStruct((B, 2, 1, W), F32),
                   jax.ShapeDtypeStruct((B, 2, 1, W), F32)],
        scratch_shapes=[pltpu.VMEM((2, W, W), F32), pltpu.VMEM((2, 1, W), F32), pltpu.VMEM((2, 1, W), F32)],
        compiler_params=_cp("parallel", "arbitrary"),
        name="mlstm",
    )(z_ml, z_ml, z_ml, z_ml, z_ml, z_ml, z_ml, z_ml, bias, c0, n0, m0)


def _outproj_kernel(x_ref, pool_ref, attn_ref, hf_ref, hb_ref, op_ref, mlg_ref, hs_ref, hst_ref,
                    wp_ref, wa_ref, wm_ref, ga_ref, o_ref):
    h = hf_ref[0] + hb_ref[0]
    ss = jnp.dot(h * h, hs_ref[...], precision=HIGHEST, preferred_element_type=F32) * (1.0 / ML_DK)
    inv = jnp.dot(lax.rsqrt(ss + EPS), hst_ref[...], precision=HIGHEST, preferred_element_type=F32)
    op = op_ref[0]
    ml = h * inv * mlg_ref[...] * (1.0 / (1.0 + jnp.exp(-op)))
    mix = jnp.dot(pool_ref[0].astype(BF16), wp_ref[...], preferred_element_type=F32)
    mix += jnp.dot(attn_ref[0], wa_ref[...], preferred_element_type=F32)
    mix += jnp.dot(ml.astype(BF16), wm_ref[...], preferred_element_type=F32)
    o_ref[0] = x_ref[0] + ga_ref[0] * mix


def out_proj(x, pool, attn, hf, hb, z_ml, mlg, hs, hst, wp, wa, wm, gate):
    B, L, D = x.shape
    T = min(512, L)
    W = ML_WIDTH
    tile = lambda w, j=0: pl.BlockSpec((1, T, w), lambda b, i: (b, i, j))
    full = lambda shape: pl.BlockSpec(shape, lambda b, i: (0,) * len(shape))
    return pl.pallas_call(
        _outproj_kernel,
        grid=(B, L // T),
        in_specs=[tile(D), tile(W), tile(MLA_HEADS * MLA_V), tile(W), tile(W), tile(W, 3),
                  full((1, W)), full(hs.shape), full(hst.shape),
                  full(wp.shape), full(wa.shape), full(wm.shape),
                  pl.BlockSpec((1, 1, D), lambda b, i: (b, 0, 0))],
        out_specs=tile(D),
        out_shape=jax.ShapeDtypeStruct((B, L, D), F32),
        compiler_params=_cp("parallel", "parallel"),
        name="out_proj",
    )(x, pool, attn, hf, hb, z_ml, mlg, hs, hst, wp, wa, wm, gate)


def _top16_rows(s, row):
    big = float(s.shape[0])
    vals, idxs = [], []
    for _ in range(PEER_TOPK):
        m = jnp.max(s, axis=0, keepdims=True)
        idx = jnp.min(jnp.where(s == m, row, big), axis=0, keepdims=True)
        vals.append(m)
        idxs.append(idx)
        s = jnp.where(row == idx, -jnp.inf, s)
    return jnp.concatenate(vals, axis=0), jnp.concatenate(idxs, axis=0)


def _candidate_pairs():
    return [(k1, k2) for k1 in range(PEER_TOPK) for k2 in range(PEER_TOPK) if (k1 + 1) * (k2 + 1) <= PEER_TOPK]


def _topk_kernel(qp_ref, sk_ref, g1_ref, g2_ref, ids_ref, gate_ref):
    T = qp_ref.shape[2]
    row = lax.broadcasted_iota(jnp.int32, (PEER_KEYS, T), 0).astype(F32)
    tops = []
    for c in range(2):
        s = jnp.dot(sk_ref[c], qp_ref[c], precision=HIGHEST, preferred_element_type=F32)
        tops.append(_top16_rows(s, row))
    pick = lambda g_ref, a: jnp.dot(g_ref[...], a, precision=HIGHEST, preferred_element_type=F32)
    ncand = g1_ref.shape[0]
    crow = lax.broadcasted_iota(jnp.int32, (ncand, T), 0).astype(F32)
    cand = pick(g1_ref, tops[0][0]) + pick(g2_ref, tops[1][0])
    cand = jnp.where(crow < float(len(_candidate_pairs())), cand, -jnp.inf)
    expert = pick(g1_ref, tops[0][1]) * float(PEER_KEYS) + pick(g2_ref, tops[1][1])
    best, eids = [], []
    for _ in range(PEER_TOPK):
        m = jnp.max(cand, axis=0, keepdims=True)
        ci = jnp.min(jnp.where(cand == m, crow, float(ncand)), axis=0, keepdims=True)
        sel = crow == ci
        best.append(m)
        eids.append(jnp.sum(jnp.where(sel, expert, 0.0), axis=0, keepdims=True))
        cand = jnp.where(sel, -jnp.inf, cand)
    best = jnp.concatenate(best, axis=0)
    p = jnp.exp(best - best[0:1, :])
    gate_ref[...] = p / jnp.sum(p, axis=0, keepdims=True)
    ids_ref[...] = jnp.concatenate(eids, axis=0).astype(jnp.int32)


def peer_topk(qp_t, sk, g1, g2):
    N = qp_t.shape[2]
    T = 256
    return pl.pallas_call(
        _topk_kernel,
        grid=(N // T, PEER_HEADS),
        in_specs=[pl.BlockSpec((2, PEER_KEYS, T), lambda i, h: (h, 0, i)),
                  pl.BlockSpec((2, PEER_KEYS, PEER_KEYS), lambda i, h: (h, 0, 0)),
                  pl.BlockSpec(g1.shape, lambda i, h: (0, 0)),
                  pl.BlockSpec(g2.shape, lambda i, h: (0, 0))],
        out_specs=[pl.BlockSpec((PEER_TOPK, T), lambda i, h: (h, i)),
                   pl.BlockSpec((PEER_TOPK, T), lambda i, h: (h, i))],
        out_shape=[jax.ShapeDtypeStruct((PEER_SLOTS, N), jnp.int32),
                   jax.ShapeDtypeStruct((PEER_SLOTS, N), F32)],
        compiler_params=_cp("parallel", "parallel"),
        name="peer_topk",
    )(qp_t, sk, g1, g2)


def _gather_rows(t, ids_s, tab_ref, stage):
    for j in range(PEER_SLOTS):
        r = ids_s[t, j]
        stage[pl.ds(ROW_SUB * j, ROW_SUB), :] = tab_ref[pl.ds(r, ROW_SUB), :]


def _unpack_rows(x):
    lo = pltpu.bitcast(x << 16, F32)
    hi = pltpu.bitcast(x & jnp.uint32(0xFFFF0000), F32)
    return jnp.concatenate([lo, hi], axis=-1).astype(BF16)


def _token_pipeline(ids_ref, tab_ref, stage0, stage1, consume):
    TB = PEER_TB
    _gather_rows(0, ids_ref, tab_ref, stage0)

    def body(i, _):
        t = 2 * i
        _gather_rows(t + 1, ids_ref, tab_ref, stage1)
        consume(t, stage0)
        _gather_rows(jnp.minimum(t + 2, TB - 1), ids_ref, tab_ref, stage0)
        consume(t + 1, stage1)
        return 0

    lax.fori_loop(0, TB // 2, body, 0)


def _peer_u_kernel(ids_ref, ht_ref, tab_ref, at_ref, stage0, stage1):
    lane_t = lax.broadcasted_iota(jnp.int32, at_ref.shape, 1)
    at_ref[...] = jnp.zeros(at_ref.shape, F32)

    def consume(t, stage):
        r = jnp.zeros(at_ref.shape, F32)
        for s in range(ROW_SUB):
            xs = _unpack_rows(stage[pl.ds(s, PEER_SLOTS, stride=ROW_SUB), :])
            r = r + jnp.dot(xs, ht_ref[s], preferred_element_type=F32)
        at_ref[...] = jnp.where(lane_t == t, r, at_ref[...])

    _token_pipeline(ids_ref, tab_ref, stage0, stage1, consume)


def _peer_specs(TB):
    R = PEER_SLOTS * ROW_SUB
    ids_spec = pl.BlockSpec((TB, PEER_SLOTS), lambda i: (i, 0), memory_space=pltpu.SMEM)
    table_spec = pl.BlockSpec(memory_space=pltpu.VMEM)
    stages = [pltpu.VMEM((R, 128), jnp.uint32), pltpu.VMEM((R, 128), jnp.uint32)]
    return ids_spec, table_spec, stages


def peer_scores(ids_flat, ht, tab):
    N = ht.shape[2]
    TB = PEER_TB
    ids_spec, table_spec, stages = _peer_specs(TB)
    return pl.pallas_call(
        _peer_u_kernel,
        grid=(N // TB,),
        in_specs=[ids_spec, pl.BlockSpec((ROW_SUB, 256, TB), lambda i: (0, 0, i)), table_spec],
        out_specs=pl.BlockSpec((PEER_SLOTS, TB), lambda i: (0, i)),
        out_shape=jax.ShapeDtypeStruct((PEER_SLOTS, N), F32),
        scratch_shapes=stages,
        compiler_params=_cp("arbitrary"),
        name="peer_scores",
    )(ids_flat, ht, tab)


def _peer_v_kernel(ids_ref, at_ref, gt_ref, tab_ref, rep_ref, o_ref, stage0, stage1, wexp):
    a = at_ref[...]
    w = gt_ref[...] * (0.5 * a * (1.0 + lax.erf(a * (1.0 / math.sqrt(2.0)))))
    wexp[...] = jnp.dot(w.T, rep_ref[...], precision=HIGHEST, preferred_element_type=F32)
    R = PEER_SLOTS * ROW_SUB
    own = (lax.broadcasted_iota(jnp.int32, (8, R), 0) ==
           lax.broadcasted_iota(jnp.int32, (8, R), 1) % ROW_SUB)

    def consume(t, stage):
        x = _unpack_rows(stage[...])
        wsel = jnp.where(own, wexp[pl.ds(t, 1), :], 0.0).astype(BF16)
        y = jnp.dot(wsel, x, preferred_element_type=F32)
        o_ref[t] = y[:ROW_SUB, :]

    _token_pipeline(ids_ref, tab_ref, stage0, stage1, consume)


def peer_combine(ids_flat, at, gt, tab, rep):
    N = at.shape[1]
    TB = PEER_TB
    R = PEER_SLOTS * ROW_SUB
    ids_spec, table_spec, stages = _peer_specs(TB)
    slot_tile = pl.BlockSpec((PEER_SLOTS, TB), lambda i: (0, i))
    return pl.pallas_call(
        _peer_v_kernel,
        grid=(N // TB,),
        in_specs=[ids_spec, slot_tile, slot_tile, table_spec, pl.BlockSpec(rep.shape, lambda i: (0, 0))],
        out_specs=pl.BlockSpec((TB, ROW_SUB, 256), lambda i: (i, 0, 0)),
        out_shape=jax.ShapeDtypeStruct((N, ROW_SUB, 256), F32),
        scratch_shapes=stages + [pltpu.VMEM((TB, R), F32)],
        compiler_params=_cp("arbitrary"),
        name="peer_combine",
    )(ids_flat, at, gt, tab, rep)


def _residual_kernel(x_ref, y_ref, g_ref, o_ref):
    o_ref[0] = x_ref[0] + g_ref[0] * y_ref[0]


def gated_residual(x, y, gate):
    B, L, D = x.shape
    T = min(1024, L)
    tile = pl.BlockSpec((1, T, D), lambda b, i: (b, i, 0))
    return pl.pallas_call(
        _residual_kernel,
        grid=(B, L // T),
        in_specs=[tile, tile, pl.BlockSpec((1, 1, D), lambda b, i: (b, 0, 0))],
        out_specs=tile,
        out_shape=jax.ShapeDtypeStruct((B, L, D), F32),
        compiler_params=_cp("parallel", "parallel"),
        name="gated_residual",
    )(x, y, gate)


def _pad_last(a, n):
    return jnp.pad(a, [(0, 0)] * (a.ndim - 1) + [(0, n - a.shape[-1])])


def _pack_table(tab):
    bits = lax.bitcast_convert_type(tab.astype(BF16), jnp.uint16).astype(jnp.uint32)
    bits = bits.reshape(tab.shape[0], ROW_SUB, 2, 128)
    return (bits[:, :, 0, :] | (bits[:, :, 1, :] << 16)).reshape(tab.shape[0] * ROW_SUB, 128)


def _layer_weights(l, norm1_gain, norm2_gain, w_in, pool_w, pool_scale, mla_q_norm, mla_kv_norm, w_uq, w_ukv,
                   q_norm, k_norm, ml_gate_bias, ml_out_norm, w_out, peer_wq, peer_subkeys, peer_u, peer_v):
    parts, start = [], 0
    for size in IN_SIZES:
        parts.append(w_in[l][:, start:start + size])
        start += size
    w_in_r = jnp.concatenate([parts[0], parts[1], parts[2], _pad_last(parts[3], 128), parts[4], parts[5],
                              parts[6], parts[7], _pad_last(parts[8], 128)], axis=1).astype(BF16)
    wbd = jnp.zeros((POOL_WIDTH, POOL_WIDTH), F32)
    for g in range(len(POOL_WINDOWS)):
        sl = slice(g * POOL_GROUP, (g + 1) * POOL_GROUP)
        wbd = wbd.at[sl, sl].set(pool_w[l, g])
    wq = _pad_last(w_uq[l].reshape(MLA_Q_RANK, MLA_HEADS, MLA_QK), HEAD_PAD).reshape(MLA_Q_RANK, -1)
    ukv = w_ukv[l].reshape(MLA_KV_RANK, MLA_HEADS, MLA_NOPE + MLA_V)
    wk = _pad_last(ukv[..., :MLA_NOPE], HEAD_PAD).reshape(MLA_KV_RANK, -1)
    wv = _pad_last(ukv[..., MLA_NOPE:], HEAD_PAD).reshape(MLA_KV_RANK, -1)
    r = jnp.arange(128)[:, None]
    cidx = jnp.arange(MLA_HEADS * HEAD_PAD)[None, :]
    pk = ((r < MLA_ROPE) & (cidx % HEAD_PAD == MLA_NOPE + r)).astype(F32)
    lane = jnp.arange(ML_WIDTH)
    hs = (lane[:, None] // ML_DK == jnp.arange(128)[None, :]).astype(F32)
    wo = w_out[l]
    wa = wo[POOL_WIDTH:POOL_WIDTH + MLA_HEADS * MLA_V]
    return dict(
        n1=norm1_gain[l], n2=norm2_gain[l], w_in=w_in_r,
        wbd=wbd.astype(BF16), pool_scale=pool_scale[l],
        qan=mla_q_norm[l], kvan=mla_kv_norm[l],
        wq=wq.astype(BF16), wk=wk.astype(BF16), wv=wv.astype(BF16), pk=pk,
        qg=_pad_last(q_norm[l], HEAD_PAD).reshape(1, HEAD_PAD), kg=_pad_last(k_norm[l], HEAD_PAD).reshape(1, HEAD_PAD),
        gate_bias=_pad_last(ml_gate_bias[l], 128).reshape(1, 128),
        mlg=ml_out_norm[l].reshape(1, ML_WIDTH), hs=hs, hst=hs.T,
        wp=wo[:POOL_WIDTH].astype(BF16), wa=wa.astype(BF16),
        wm=wo[POOL_WIDTH + MLA_HEADS * MLA_V:].astype(BF16),
        peer_wq=peer_wq[l].astype(BF16),
        sk=peer_subkeys[l].reshape(2 * PEER_HEADS, PEER_KEYS, PEER_KEYS),
        u_tab=_pack_table(peer_u[l]), v_tab=_pack_table(peer_v[l]),
    )


def _peer_constants():
    R = PEER_SLOTS * ROW_SUB
    pairs = _candidate_pairs()
    ncand = -(-len(pairs) // 8) * 8
    k1 = jnp.array([p[0] for p in pairs] + [-1] * (ncand - len(pairs)))[:, None]
    k2 = jnp.array([p[1] for p in pairs] + [-1] * (ncand - len(pairs)))[:, None]
    rank = jnp.arange(PEER_TOPK)[None, :]
    g1 = (k1 == rank).astype(F32)
    g2 = (k2 == rank).astype(F32)
    rep = (jnp.arange(PEER_SLOTS)[:, None] == jnp.arange(R)[None, :] // ROW_SUB).astype(F32)
    return g1, g2, rep


def _rope_tables(n):
    n_rows = n // GRID_W
    row = jnp.repeat(jnp.arange(n_rows), GRID_W, total_repeat_length=n).astype(F32)
    col = (jnp.arange(n) % GRID_W).astype(F32)
    per_axis = MLA_ROPE // 2
    freqs = ROPE_BASE ** (-jnp.arange(0, per_axis, 2, dtype=F32) / per_axis)
    ang = jnp.concatenate([row[:, None] * freqs, col[:, None] * freqs], axis=-1)
    c, s = jnp.cos(ang), jnp.sin(ang)
    cos = jnp.concatenate([jnp.ones((n, MLA_NOPE), F32), c, c, jnp.ones((n, HEAD_PAD - MLA_QK), F32)], axis=-1)
    sin = jnp.concatenate([jnp.zeros((n, MLA_NOPE), F32), -s, s, jnp.zeros((n, HEAD_PAD - MLA_QK), F32)], axis=-1)
    return cos, sin


def _peer_ffn(x, w, shift, scale, gate, consts):
    B, L, D = x.shape
    g1, g2, rep = consts
    qp_t, h_t = peer_query(x, w["n2"], shift, scale, w["peer_wq"])
    ids_t, gates_t = peer_topk(qp_t, w["sk"], g1, g2)
    ids_flat = ids_t.T * ROW_SUB
    a_t = peer_scores(ids_flat, h_t.reshape(ROW_SUB, 256, B * L), w["u_tab"])
    y = peer_combine(ids_flat, a_t, gates_t, w["v_tab"], rep)
    return gated_residual(x, y.reshape(B, L, D), gate)


def kernel(x, c, ctx, c_ctx, norm1_gain, norm2_gain, w_ada, b_ada, w_in, pool_w, pool_scale, mla_q_norm,
           mla_kv_norm, w_uq, w_ukv, q_norm, k_norm, ml_gate_bias, ml_out_norm, w_out, peer_wq, peer_subkeys,
           peer_u, peer_v):
    B, S, D = x.shape
    Lc_ctx = ctx.shape[1]
    depth = w_in.shape[0]
    cos_x, sin_x = _rope_tables(S)
    cos_c = jnp.ones((Lc_ctx, HEAD_PAD), F32)
    sin_c = jnp.zeros((Lc_ctx, HEAD_PAD), F32)
    consts = _peer_constants()
    cond8 = jnp.zeros((8, D), F32).at[:B].set(c).at[B].set(c_ctx)
    W = ML_WIDTH
    zero_state = (jnp.zeros((B, 2, W, W), F32), jnp.zeros((B, 2, 1, W), F32),
                  jnp.full((B, 2, 1, W), NEG_INIT, F32))
    x_ctx = ctx
    for l in range(depth):
        last = l == depth - 1
        w = _layer_weights(l, norm1_gain, norm2_gain, w_in, pool_w, pool_scale, mla_q_norm, mla_kv_norm, w_uq,
                           w_ukv, q_norm, k_norm, ml_gate_bias, ml_out_norm, w_out, peer_wq, peer_subkeys,
                           peer_u, peer_v)
        mods = ada_mod(cond8, w_ada[l], b_ada[l])
        m_x = [mods[:B, i * D:(i + 1) * D].reshape(B, 1, D) for i in range(6)]
        m_c = [jnp.broadcast_to(mods[B, i * D:(i + 1) * D].reshape(1, 1, D), (B, 1, D)) for i in range(6)]

        zp_c, zm_c, zl_c = in_proj(x_ctx, w["n1"], m_c[0], m_c[1], w["w_in"])
        qc, kc, vc = mla_qkv(zm_c, w["qan"], w["kvan"], w["wq"], w["wk"], w["wv"], w["pk"], w["qg"], w["kg"],
                             cos_c, sin_c)
        hf_c, hb_c, cT, nT, mT = mlstm(zl_c, w["gate_bias"], zero_state)
        if not last:
            pool_c = pool_mixer(zp_c, w["wbd"], w["pool_scale"])
            attn_c = attention(qc, kc, vc)
            xc = out_proj(x_ctx, pool_c, attn_c, hf_c, hb_c, zl_c, w["mlg"], w["hs"], w["hst"],
                          w["wp"], w["wa"], w["wm"], m_c[2])
            x_ctx_new = _peer_ffn(xc, w, m_c[3], m_c[4], m_c[5], consts)

        zp, zm, zl = in_proj(x, w["n1"], m_x[0], m_x[1], w["w_in"])
        pool_x = pool_mixer(zp, w["wbd"], w["pool_scale"])
        q, k, v = mla_qkv(zm, w["qan"], w["kvan"], w["wq"], w["wk"], w["wv"], w["pk"], w["qg"], w["kg"],
                          cos_x, sin_x)
        attn_x = attention(q, kc, vc, k, v)
        hf, hb, _, _, _ = mlstm(zl, w["gate_bias"], (cT, nT, mT))
        x = out_proj(x, pool_x, attn_x, hf, hb, zl, w["mlg"], w["hs"], w["hst"],
                     w["wp"], w["wa"], w["wm"], m_x[2])
        x = _peer_ffn(x, w, m_x[3], m_x[4], m_x[5], consts)
        if not last:
            x_ctx = x_ctx_new
    return x
```

```python
import functools
import math

import jax
import jax.numpy as jnp
from jax import lax
from jax.experimental import pallas as pl
from jax.experimental.pallas import tpu as pltpu

F32 = jnp.float32
BF16 = jnp.bfloat16
HIGHEST = lax.Precision.HIGHEST

EPS = 1e-6
D_MODEL = 1024
GRID_W = 64
ROPE_BASE = 10000.0
POOL_WINDOWS = (2, 4, 8, 16)
POOL_GROUP = 64
POOL_WIDTH = 256
MLA_HEADS = 8
MLA_NOPE = 64
MLA_ROPE = 32
MLA_QK = 96
MLA_V = 64
MLA_Q_RANK = 384
MLA_KV_RANK = 256
MLA_SCALE = MLA_QK ** -0.5
Q_PRESCALE = MLA_SCALE * math.log2(math.e)
HEAD_PAD = 128
ATT_TQ = 512
ATT_TC = 512
VT_ROWS = 80
ML_HEADS = 4
ML_DK = 64
ML_WIDTH = 256
ML_CHUNK = 128
NEG_INIT = -1e30
IN_SIZES = (256, 384, 256, 32, 256, 256, 256, 256, 16)
PEER_HEADS = 8
PEER_KEYS = 128
PEER_TOPK = 16
PEER_SLOTS = PEER_HEADS * PEER_TOPK
PEER_TB = 128
ROW_SUB = 4

Z_POOL, Z_MLA, Z_ML = 256, 768, 1152
VMEM_LIMIT = 56 * 1024 * 1024


def _cp(*sem, vmem=VMEM_LIMIT):
    return pltpu.CompilerParams(dimension_semantics=sem, vmem_limit_bytes=vmem)


def _ada_kernel(c_ref, w_ref, b_ref, o_ref):
    c = c_ref[...]
    s = c * (1.0 / (1.0 + jnp.exp(-c)))
    o_ref[...] = jnp.dot(s, w_ref[...], precision=HIGHEST, preferred_element_type=F32) + b_ref[...]


def ada_mod(cond8, w, b):
    n = w.shape[1]
    tn = n // 4
    return pl.pallas_call(
        _ada_kernel,
        grid=(n // tn,),
        in_specs=[pl.BlockSpec((8, D_MODEL), lambda j: (0, 0)),
                  pl.BlockSpec((D_MODEL, tn), lambda j: (0, j)),
                  pl.BlockSpec((1, tn), lambda j: (0, j))],
        out_specs=pl.BlockSpec((8, tn), lambda j: (0, j)),
        out_shape=jax.ShapeDtypeStruct((8, n), F32),
        compiler_params=_cp("arbitrary"),
        name="ada_mod",
    )(cond8, w, b.reshape(1, n))


def _modulated(x, gain, shift, scale):
    ms = jnp.mean(x * x, axis=-1, keepdims=True)
    y = x * lax.rsqrt(ms + EPS) * gain
    return y * (1.0 + scale) + shift


def _inproj_kernel(x_ref, gain_ref, shift_ref, scale_ref, w_ref, zp_ref, zm_ref, zl_ref):
    h = _modulated(x_ref[0], gain_ref[...], shift_ref[0], scale_ref[0])
    res = jnp.dot(h.astype(BF16), w_ref[...], preferred_element_type=F32)
    zp_ref[0] = res[:, :Z_POOL]
    zm_ref[0] = res[:, Z_POOL:Z_POOL + Z_MLA]
    zl_ref[0] = res[:, Z_POOL + Z_MLA:]


def in_proj(x, gain, shift, scale, w):
    B, L, D = x.shape
    tm = min(512, L)
    n = w.shape[1]
    vec = pl.BlockSpec((1, 1, D), lambda b, i: (b, 0, 0))
    return pl.pallas_call(
        _inproj_kernel,
        grid=(B, L // tm),
        in_specs=[pl.BlockSpec((1, tm, D), lambda b, i: (b, i, 0)),
                  pl.BlockSpec((1, D), lambda b, i: (0, 0)),
                  vec, vec,
                  pl.BlockSpec((D, n), lambda b, i: (0, 0))],
        out_specs=[pl.BlockSpec((1, tm, Z_POOL), lambda b, i: (b, i, 0)),
                   pl.BlockSpec((1, tm, Z_MLA), lambda b, i: (b, i, 0)),
                   pl.BlockSpec((1, tm, Z_ML), lambda b, i: (b, i, 0))],
        out_shape=[jax.ShapeDtypeStruct((B, L, Z_POOL), F32),
                   jax.ShapeDtypeStruct((B, L, Z_MLA), F32),
                   jax.ShapeDtypeStruct((B, L, Z_ML), F32)],
        compiler_params=_cp("parallel", "parallel"),
        name="in_proj",
    )(x, gain.reshape(1, D), shift, scale, w)


def _peerq_kernel(x_ref, gain_ref, shift_ref, scale_ref, w_ref, qp_ref, h_ref):
    h = _modulated(x_ref[0], gain_ref[...], shift_ref[0], scale_ref[0])
    h_ref[...] = h.T.astype(BF16)
    res = jnp.dot(h.astype(BF16), w_ref[...], preferred_element_type=F32)
    for g in range(2 * PEER_HEADS):
        qp_ref[g] = res[:, g * PEER_KEYS:(g + 1) * PEER_KEYS].T


def peer_query(x, gain, shift, scale, w):
    B, L, D = x.shape
    tm = min(512, L)
    nb = L // tm
    n = w.shape[1]
    vec = pl.BlockSpec((1, 1, D), lambda b, i: (b, 0, 0))
    return pl.pallas_call(
        _peerq_kernel,
        grid=(B, nb),
        in_specs=[pl.BlockSpec((1, tm, D), lambda b, i: (b, i, 0)),
                  pl.BlockSpec((1, D), lambda b, i: (0, 0)),
                  vec, vec,
                  pl.BlockSpec((D, n), lambda b, i: (0, 0))],
        out_specs=[pl.BlockSpec((2 * PEER_HEADS, PEER_KEYS, tm), lambda b, i: (0, 0, b * nb + i)),
                   pl.BlockSpec((D, tm), lambda b, i: (0, b * nb + i))],
        out_shape=[jax.ShapeDtypeStruct((2 * PEER_HEADS, PEER_KEYS, B * L), F32),
                   jax.ShapeDtypeStruct((D, B * L), BF16)],
        compiler_params=_cp("parallel", "parallel"),
        name="peer_query",
    )(x, gain.reshape(1, D), shift, scale, w)


def _pool_kernel(p_ref, c_ref, n_ref, wbd_ref, sc_ref, o_ref, *, L, T):
    i = pl.program_id(1)
    cur = c_ref[0]
    u3 = jnp.concatenate([p_ref[0], cur, n_ref[0]], axis=0).astype(BF16)
    t = i * T + lax.broadcasted_iota(jnp.int32, (T, 3 * T), 0)
    s = (i - 1) * T + lax.broadcasted_iota(jnp.int32, (T, 3 * T), 1)
    lane = lax.broadcasted_iota(jnp.int32, (T, POOL_WIDTH), 1)
    trow = i * T + lax.broadcasted_iota(jnp.int32, (T, POOL_WIDTH), 0)
    win = jnp.zeros((T, POOL_WIDTH), F32)
    for g, w in enumerate(POOL_WINDOWS):
        lo = jnp.maximum(t - w // 2, 0)
        hi = jnp.minimum(t + w // 2, L)
        band = jnp.where((s >= lo) & (s < hi), 1.0, 0.0).astype(BF16)
        ws = jnp.dot(band, u3, preferred_element_type=F32)
        cnt = (jnp.minimum(trow + w // 2, L) - jnp.maximum(trow - w // 2, 0)).astype(F32)
        in_group = (lane >= g * POOL_GROUP) & (lane < (g + 1) * POOL_GROUP)
        win = jnp.where(in_group, ws / cnt, win)
    d = win - cur
    y = jnp.dot(d.astype(BF16), wbd_ref[...], preferred_element_type=F32)
    o_ref[0] = y * sc_ref[...]


def pool_mixer(z_pool, wbd, scale):
    B, L, C = z_pool.shape
    T = 256
    nb = L // T
    return pl.pallas_call(
        functools.partial(_pool_kernel, L=L, T=T),
        grid=(B, nb),
        in_specs=[pl.BlockSpec((1, T, C), lambda b, i: (b, jnp.maximum(i - 1, 0), 0)),
                  pl.BlockSpec((1, T, C), lambda b, i: (b, i, 0)),
                  pl.BlockSpec((1, T, C), lambda b, i: (b, jnp.minimum(i + 1, nb - 1), 0)),
                  pl.BlockSpec((C, C), lambda b, i: (0, 0)),
                  pl.BlockSpec((1, C), lambda b, i: (0, 0))],
        out_specs=pl.BlockSpec((1, T, C), lambda b, i: (b, i, 0)),
        out_shape=jax.ShapeDtypeStruct((B, L, C), F32),
        compiler_params=_cp("parallel", "parallel"),
        name="pool_mixer",
    )(z_pool, z_pool, z_pool, wbd, scale.reshape(1, C))


def _rms(x, gain, n):
    ss = jnp.sum(x * x, axis=-1, keepdims=True) * (1.0 / n)
    return x * lax.rsqrt(ss + EPS) * gain


def _mla_kernel(z_ref, qan_ref, kvan_ref, wq_ref, wk_ref, wv_ref, pk_ref, qg_ref, kg_ref,
                cos_ref, sin_ref, q_out, k_out, v_out):
    z = z_ref[0]
    zq = z[:, :MLA_Q_RANK]
    zkv = z[:, MLA_Q_RANK:MLA_Q_RANK + MLA_KV_RANK]
    zkr = z[:, MLA_Q_RANK + MLA_KV_RANK:]
    nq = _rms(zq, qan_ref[...], MLA_Q_RANK).astype(BF16)
    nkv = _rms(zkv, kvan_ref[...], MLA_KV_RANK).astype(BF16)
    qp = jnp.dot(nq, wq_ref[...], preferred_element_type=F32)
    kp = jnp.dot(nkv, wk_ref[...], preferred_element_type=F32)
    kp = kp + jnp.dot(zkr, pk_ref[...], precision=HIGHEST, preferred_element_type=F32)
    vp = jnp.dot(nkv, wv_ref[...], preferred_element_type=F32)
    cos = cos_ref[...]
    sin = sin_ref[...]
    lane = lax.broadcasted_iota(jnp.int32, cos.shape, 1)
    first_half = lane < MLA_NOPE + MLA_ROPE // 2
    extra = (VT_ROWS - MLA_V, z.shape[0])
    ones_rows = jnp.where(lax.broadcasted_iota(jnp.int32, extra, 0) == 0, 1.0, 0.0)
    for h in range(MLA_HEADS):
        sl = slice(h * HEAD_PAD, (h + 1) * HEAD_PAD)
        for src, gain_ref, out, mult in ((qp, qg_ref, q_out, Q_PRESCALE), (kp, kg_ref, k_out, 1.0)):
            xn = _rms(src[:, sl], gain_ref[...], MLA_QK)
            partner = jnp.where(first_half, pltpu.roll(xn, HEAD_PAD - MLA_ROPE // 2, 1),
                                pltpu.roll(xn, MLA_ROPE // 2, 1))
            xr = xn * cos + partner * sin
            out[0, h] = (xr * mult).astype(BF16)
        v_out[0, h] = jnp.concatenate([vp[:, sl].T[:MLA_V, :], ones_rows], axis=0).astype(BF16)


def mla_qkv(z_mla, qan, kvan, wq, wk, wv, pk, qg, kg, cos, sin):
    B, L, _ = z_mla.shape
    T = 256
    full = lambda shape: pl.BlockSpec(shape, lambda b, i: (0,) * len(shape))
    head_out = pl.BlockSpec((1, MLA_HEADS, T, HEAD_PAD), lambda b, i: (b, 0, i, 0))
    out_sds = jax.ShapeDtypeStruct((B, MLA_HEADS, L, HEAD_PAD), BF16)
    vt_out = pl.BlockSpec((1, MLA_HEADS, VT_ROWS, T), lambda b, i: (b, 0, 0, i))
    vt_sds = jax.ShapeDtypeStruct((B, MLA_HEADS, VT_ROWS, L), BF16)
    return pl.pallas_call(
        _mla_kernel,
        grid=(B, L // T),
        in_specs=[pl.BlockSpec((1, T, Z_MLA), lambda b, i: (b, i, 0)),
                  full((1, MLA_Q_RANK)), full((1, MLA_KV_RANK)),
                  full(wq.shape), full(wk.shape), full(wv.shape), full(pk.shape),
                  full((1, HEAD_PAD)), full((1, HEAD_PAD)),
                  pl.BlockSpec((T, HEAD_PAD), lambda b, i: (i, 0)),
                  pl.BlockSpec((T, HEAD_PAD), lambda b, i: (i, 0))],
        out_specs=[head_out, head_out, vt_out],
        out_shape=[out_sds, out_sds, vt_sds],
        compiler_params=_cp("parallel", "parallel"),
        name="mla_qkv",
    )(z_mla, qan.reshape(1, -1), kvan.reshape(1, -1), wq, wk, wv, pk, qg, kg, cos, sin)


def _flash_kernel(q_ref, kc_ref, vct_ref, *rest, nchunks):
    if nchunks:
        k_ref, vt_ref, o_ref, s_s, m_s, acc_s = rest
    else:
        o_ref, s_s, m_s, acc_s = rest
    nc = kc_ref.shape[2]

    def scores(h, kblk):
        n = kblk.shape[0]
        s_s[h, :n] = lax.dot_general(kblk, q_ref[0, h], (((1,), (1,)), ((), ())), preferred_element_type=F32)

    def update(h, n, vtblk):
        st = s_s[h, :n]
        m = m_s[h]
        m_new = jnp.maximum(m, jnp.max(st, axis=0, keepdims=True))
        p = jnp.exp2(st - m_new).astype(BF16)
        m_s[h] = m_new
        acc_s[h] = jnp.exp2(m - m_new) * acc_s[h] + jnp.dot(vtblk, p, preferred_element_type=F32)

    m_s[...] = jnp.full(m_s.shape, -jnp.inf, F32)
    acc_s[...] = jnp.zeros(acc_s.shape, F32)
    scores(0, kc_ref[0, 0])
    scores(1, kc_ref[0, 1])
    update(0, nc, vct_ref[0, 0])
    if nchunks:
        scores(0, k_ref[0, 0, pl.ds(0, ATT_TC), :])
    update(1, nc, vct_ref[0, 1])
    if nchunks:
        def body(c, _):
            off = pl.multiple_of(c * ATT_TC, ATT_TC)
            nxt = pl.multiple_of(jnp.minimum(c + 1, nchunks - 1) * ATT_TC, ATT_TC)
            scores(1, k_ref[0, 1, pl.ds(off, ATT_TC), :])
            update(0, ATT_TC, vt_ref[0, 0, :, pl.ds(off, ATT_TC)])
            scores(0, k_ref[0, 0, pl.ds(nxt, ATT_TC), :])
            update(1, ATT_TC, vt_ref[0, 1, :, pl.ds(off, ATT_TC)])
            return 0
        lax.fori_loop(0, nchunks, body, 0, unroll=4)
    o = jnp.concatenate([acc_s[h, :MLA_V] / acc_s[h, MLA_V:MLA_V + 1] for h in range(2)], axis=0)
    o_ref[0] = o.T.astype(BF16)


def attention(q, kc, vct, k=None, vt=None):
    B, H, Lq, _ = q.shape
    Lc = kc.shape[2]
    tq = min(ATT_TQ, Lq)
    pair4 = lambda n, d: pl.BlockSpec((1, 2, n, d), lambda b, h, i: (b, h, 0, 0))
    in_specs = [pl.BlockSpec((1, 2, tq, HEAD_PAD), lambda b, h, i: (b, h, i, 0)),
                pair4(Lc, HEAD_PAD), pair4(VT_ROWS, Lc)]
    args = [q, kc, vct]
    nchunks = 0
    if k is not None:
        Lk = k.shape[2]
        nchunks = Lk // ATT_TC
        in_specs += [pair4(Lk, HEAD_PAD), pair4(VT_ROWS, Lk)]
        args += [k, vt]
    return pl.pallas_call(
        functools.partial(_flash_kernel, nchunks=nchunks),
        grid=(B, H // 2, Lq // tq),
        in_specs=in_specs,
        out_specs=pl.BlockSpec((1, tq, 2 * MLA_V), lambda b, h, i: (b, i, h)),
        out_shape=jax.ShapeDtypeStruct((B, Lq, H * MLA_V), BF16),
        scratch_shapes=[pltpu.VMEM((2, max(ATT_TC, Lc), tq), F32), pltpu.VMEM((2, 1, tq), F32),
                        pltpu.VMEM((2, VT_ROWS, tq), F32)],
        compiler_params=_cp("parallel", "parallel", "arbitrary"),
        name="attention",
    )(*args)


def _log_sigmoid(x):
    return jnp.minimum(x, 0.0) - jnp.log(1.0 + jnp.exp(-jnp.abs(x)))


def _mlstm_direction(d, q, k, v, g, C_s, n_s, m_s):
    Lc = q.shape[0]
    row = lax.broadcasted_iota(jnp.int32, (Lc, Lc), 0)
    col = lax.broadcasted_iota(jnp.int32, (Lc, Lc), 1)
    tri = (col <= row) if d == 0 else (col >= row)
    logf = _log_sigmoid(g)
    bcol = jnp.dot(jnp.where(tri, 1.0, 0.0), logf, precision=HIGHEST, preferred_element_type=F32)
    bT = bcol.T
    gT = g.T
    lane = lax.broadcasted_iota(jnp.int32, (1, ML_WIDTH), 1)
    kb = k.astype(BF16)
    vb = v.astype(BF16)
    Cst = C_s[d]
    nst = n_s[d]
    mst = m_s[d]
    qc = jnp.dot(q.astype(BF16), Cst.astype(BF16), preferred_element_type=F32)
    out = jnp.zeros((Lc, ML_WIDTH), F32)
    ws_all = jnp.zeros((Lc, ML_WIDTH), F32)
    wprev_all = jnp.zeros((1, ML_WIDTH), F32)
    mnew_all = jnp.zeros((1, ML_WIDTH), F32)
    for h in range(ML_HEADS):
        il = 8 * d + h
        fl = 8 * d + 4 + h
        head = (lane >= h * ML_DK) & (lane < (h + 1) * ML_DK)
        bc = bcol[:, fl:fl + 1]
        br = bT[fl:fl + 1, :]
        ir = gT[il:il + 1, :]
        ic = g[:, il:il + 1]
        mprev = mst[:, h * ML_DK:h * ML_DK + 1]
        dmat = jnp.where(tri, bc - br + ir, -jnp.inf)
        inter = bc + mprev
        mj = jnp.maximum(inter, jnp.max(dmat, axis=-1, keepdims=True))
        w_inter = jnp.exp(inter - mj)
        qh = jnp.where(head, q, 0.0)
        s = lax.dot_general(qh.astype(BF16), kb, (((1,), (1,)), ((), ())), preferred_element_type=F32)
        qk = s * jnp.exp(dmat - mj)
        pv = jnp.dot(qk.astype(BF16), vb, preferred_element_type=F32)
        qn = jnp.sum(qh * nst, axis=-1, keepdims=True)
        den = jnp.sum(qk, axis=-1, keepdims=True) + w_inter * qn
        denom = jnp.maximum(jnp.abs(den), jnp.exp(-mj))
        out = jnp.where(head, (pv + qc * w_inter) / denom, out)
        blast = bc[Lc - 1:Lc, :] if d == 0 else bc[0:1, :]
        dec = blast - bc + ic
        mnew = jnp.maximum(blast + mprev, jnp.max(dec, axis=0, keepdims=True))
        wprev = jnp.exp(blast + mprev - mnew)
        ws = jnp.exp(dec - mnew)
        ws_all = jnp.where(head, ws, ws_all)
        wprev_all = jnp.where(head, wprev, wprev_all)
        mnew_all = jnp.where(head, mnew, mnew_all)
    kw = k * ws_all
    upd = jnp.dot(kw.T.astype(BF16), vb, preferred_element_type=F32)
    r2 = lax.broadcasted_iota(jnp.int32, (ML_WIDTH, ML_WIDTH), 0) // ML_DK
    c2 = lax.broadcasted_iota(jnp.int32, (ML_WIDTH, ML_WIDTH), 1) // ML_DK
    C_s[d] = Cst * wprev_all + jnp.where(r2 == c2, upd, 0.0)
    n_s[d] = nst * wprev_all + jnp.sum(kw, axis=0, keepdims=True)
    m_s[d] = mnew_all
    return out


def _mlstm_kernel(qf_ref, kf_ref, vf_ref, gf_ref, qb_ref, kb_ref, vb_ref, gb_ref, bias_ref,
                  c0_ref, n0_ref, m0_ref, hf_ref, hb_ref, cT_ref, nT_ref, mT_ref, C_s, n_s, m_s):
    c = pl.program_id(1)

    @pl.when(c == 0)
    def _():
        C_s[...] = c0_ref[0]
        n_s[...] = n0_ref[0]
        m_s[...] = m0_ref[0]

    scale = ML_DK ** -0.5
    hf_ref[0] = _mlstm_direction(0, qf_ref[0] * scale, kf_ref[0], vf_ref[0],
                                 gf_ref[0] + bias_ref[...], C_s, n_s, m_s)
    hb_ref[0] = _mlstm_direction(1, qb_ref[0] * scale, kb_ref[0], vb_ref[0],
                                 gb_ref[0] + bias_ref[...], C_s, n_s, m_s)

    @pl.when(c == pl.num_programs(1) - 1)
    def _():
        cT_ref[0] = C_s[...]
        nT_ref[0] = n_s[...]
        mT_ref[0] = m_s[...]


def mlstm(z_ml, bias, state):
    B, L, _ = z_ml.shape
    Lc = ML_CHUNK
    nc = L // Lc
    c0, n0, m0 = state
    W = ML_WIDTH
    fwd = lambda j: pl.BlockSpec((1, Lc, W), lambda b, c: (b, c, j))
    bwd = lambda j: pl.BlockSpec((1, Lc, W), lambda b, c: (b, nc - 1 - c, j))
    gcol = 4 * W // 128
    st_c = pl.BlockSpec((1, 2, W, W), lambda b, c: (b, 0, 0, 0))
    st_v = pl.BlockSpec((1, 2, 1, W), lambda b, c: (b, 0, 0, 0))
    return pl.pallas_call(
        _mlstm_kernel,
        grid=(B, nc),
        in_specs=[fwd(0), fwd(1), fwd(2), pl.BlockSpec((1, Lc, 128), lambda b, c: (b, c, gcol)),
                  bwd(0), bwd(1), bwd(2), pl.BlockSpec((1, Lc, 128), lambda b, c: (b, nc - 1 - c, gcol)),
                  pl.BlockSpec((1, 128), lambda b, c: (0, 0)),
                  st_c, st_v, st_v],
        out_specs=[pl.BlockSpec((1, Lc, W), lambda b, c: (b, c, 0)),
                   pl.BlockSpec((1, Lc, W), lambda b, c: (b, nc - 1 - c, 0)),
                   st_c, st_v, st_v],
        out_shape=[jax.ShapeDtypeStruct((B, L, W), F32), jax.ShapeDtypeStruct((B, L, W), F32),
                   jax.ShapeDtypeStruct((B, 2, W, W), F32), jax.ShapeDtypeStruct((B, 2, 1, W), F32),
                   jax.ShapeDtypeStruct((B, 2, 1, W), F32)],
        scratch_shapes=[pltpu.VMEM((2, W, W), F32), pltpu.VMEM((2, 1, W), F32), pltpu.VMEM((2, 1, W), F32)],
        compiler_params=_cp("parallel", "arbitrary"),
        name="mlstm",
    )(z_ml, z_ml, z_ml, z_ml, z_ml, z_ml, z_ml, z_ml, bias, c0, n0, m0)


def _outproj_kernel(x_ref, pool_ref, attn_ref, hf_ref, hb_ref, op_ref, mlg_ref, hs_ref, hst_ref,
                    wp_ref, wa_ref, wm_ref, ga_ref, o_ref):
    h = hf_ref[0] + hb_ref[0]
    ss = jnp.dot(h * h, hs_ref[...], precision=HIGHEST, preferred_element_type=F32) * (1.0 / ML_DK)
    inv = jnp.dot(lax.rsqrt(ss + EPS), hst_ref[...], precision=HIGHEST, preferred_element_type=F32)
    op = op_ref[0]
    ml = h * inv * mlg_ref[...] * (1.0 / (1.0 + jnp.exp(-op)))
    mix = jnp.dot(pool_ref[0].astype(BF16), wp_ref[...], preferred_element_type=F32)
    mix += jnp.dot(attn_ref[0], wa_ref[...], preferred_element_type=F32)
    mix += jnp.dot(ml.astype(BF16), wm_ref[...], preferred_element_type=F32)
    o_ref[0] = x_ref[0] + ga_ref[0] * mix


def out_proj(x, pool, attn, hf, hb, z_ml, mlg, hs, hst, wp, wa, wm, gate):
    B, L, D = x.shape
    T = min(512, L)
    W = ML_WIDTH
    tile = lambda w, j=0: pl.BlockSpec((1, T, w), lambda b, i: (b, i, j))
    full = lambda shape: pl.BlockSpec(shape, lambda b, i: (0,) * len(shape))
    return pl.pallas_call(
        _outproj_kernel,
        grid=(B, L // T),
        in_specs=[tile(D), tile(W), tile(MLA_HEADS * MLA_V), tile(W), tile(W), tile(W, 3),
                  full((1, W)), full(hs.shape), full(hst.shape),
                  full(wp.shape), full(wa.shape), full(wm.shape),
                  pl.BlockSpec((1, 1, D), lambda b, i: (b, 0, 0))],
        out_specs=tile(D),
        out_shape=jax.ShapeDtypeStruct((B, L, D), F32),
        compiler_params=_cp("parallel", "parallel"),
        name="out_proj",
    )(x, pool, attn, hf, hb, z_ml, mlg, hs, hst, wp, wa, wm, gate)


def _top16_rows(s, row):
    big = float(s.shape[0])
    vals, idxs = [], []
    for _ in range(PEER_TOPK):
        m = jnp.max(s, axis=0, keepdims=True)
        idx = jnp.min(jnp.where(s == m, row, big), axis=0, keepdims=True)
        vals.append(m)
        idxs.append(idx)
        s = jnp.where(row == idx, -jnp.inf, s)
    return jnp.concatenate(vals, axis=0), jnp.concatenate(idxs, axis=0)


def _candidate_pairs():
    return [(k1, k2) for k1 in range(PEER_TOPK) for k2 in range(PEER_TOPK) if (k1 + 1) * (k2 + 1) <= PEER_TOPK]


def _topk_kernel(qp_ref, sk_ref, g1_ref, g2_ref, ids_ref, gate_ref):
    T = 256
    row = lax.broadcasted_iota(jnp.int32, (PEER_KEYS, T), 0).astype(F32)
    pick = lambda g_ref, a: jnp.dot(g_ref[...], a, precision=HIGHEST, preferred_element_type=F32)
    ncand = g1_ref.shape[0]
    crow = lax.broadcasted_iota(jnp.int32, (ncand, T), 0).astype(F32)

    def tile(lt, _):
        cols = pl.ds(pl.multiple_of(lt * T, T), T)
        tops = []
        for c in range(2):
            s = jnp.dot(sk_ref[c], qp_ref[c, :, cols], precision=HIGHEST, preferred_element_type=F32)
            tops.append(_top16_rows(s, row))
        cand = pick(g1_ref, tops[0][0]) + pick(g2_ref, tops[1][0])
        cand = jnp.where(crow < float(len(_candidate_pairs())), cand, -jnp.inf)
        expert = pick(g1_ref, tops[0][1]) * float(PEER_KEYS) + pick(g2_ref, tops[1][1])
        best, eids = [], []
        for _ in range(PEER_TOPK):
            m = jnp.max(cand, axis=0, keepdims=True)
            ci = jnp.min(jnp.where(cand == m, crow, float(ncand)), axis=0, keepdims=True)
            sel = crow == ci
            best.append(m)
            eids.append(jnp.sum(jnp.where(sel, expert, 0.0), axis=0, keepdims=True))
            cand = jnp.where(sel, -jnp.inf, cand)
        best = jnp.concatenate(best, axis=0)
        p = jnp.exp(best - best[0:1, :])
        gate_ref[:, cols] = p / jnp.sum(p, axis=0, keepdims=True)
        ids_ref[:, cols] = jnp.concatenate(eids, axis=0).astype(jnp.int32)
        return 0

    lax.fori_loop(0, qp_ref.shape[2] // T, tile, 0)


def peer_topk(qp_t, sk, g1, g2):
    N = qp_t.shape[2]
    T = 512
    return pl.pallas_call(
        _topk_kernel,
        grid=(N // T, PEER_HEADS),
        in_specs=[pl.BlockSpec((2, PEER_KEYS, T), lambda i, h: (h, 0, i)),
                  pl.BlockSpec((2, PEER_KEYS, PEER_KEYS), lambda i, h: (h, 0, 0)),
                  pl.BlockSpec(g1.shape, lambda i, h: (0, 0)),
                  pl.BlockSpec(g2.shape, lambda i, h: (0, 0))],
        out_specs=[pl.BlockSpec((PEER_TOPK, T), lambda i, h: (h, i)),
                   pl.BlockSpec((PEER_TOPK, T), lambda i, h: (h, i))],
        out_shape=[jax.ShapeDtypeStruct((PEER_SLOTS, N), jnp.int32),
                   jax.ShapeDtypeStruct((PEER_SLOTS, N), F32)],
        compiler_params=_cp("parallel", "parallel"),
        name="peer_topk",
    )(qp_t, sk, g1, g2)


def _gather_rows(t, ids_s, tab_ref, stage):
    for j in range(PEER_SLOTS):
        r = ids_s[t, j]
        stage[pl.ds(ROW_SUB * j, ROW_SUB), :] = tab_ref[pl.ds(r, ROW_SUB), :]


def _unpack_rows(x):
    lo = pltpu.bitcast(x << 16, F32)
    hi = pltpu.bitcast(x & jnp.uint32(0xFFFF0000), F32)
    return jnp.concatenate([lo, hi], axis=-1).astype(BF16)


def _token_pipeline(ids_ref, tab_ref, stage0, stage1, consume):
    TB = PEER_TB
    _gather_rows(0, ids_ref, tab_ref, stage0)
    _gather_rows(1, ids_ref, tab_ref, stage1)

    def body(i, _):
        t = 2 * i
        consume(t, stage0)
        _gather_rows(jnp.minimum(t + 2, TB - 1), ids_ref, tab_ref, stage0)
        consume(t + 1, stage1)
        _gather_rows(jnp.minimum(t + 3, TB - 1), ids_ref, tab_ref, stage1)
        return 0

    lax.fori_loop(0, TB // 2, body, 0)


def _peer_u_kernel(ids_ref, ht_ref, tab_ref, at_ref, stage0, stage1):
    lane_t = lax.broadcasted_iota(jnp.int32, at_ref.shape, 1)
    at_ref[...] = jnp.zeros(at_ref.shape, F32)

    def consume(t, stage):
        r = jnp.zeros(at_ref.shape, F32)
        for s in range(ROW_SUB):
            xs = _unpack_rows(stage[pl.ds(s, PEER_SLOTS, stride=ROW_SUB), :])
            r = r + jnp.dot(xs, ht_ref[s], preferred_element_type=F32)
        at_ref[...] = jnp.where(lane_t == t, r, at_ref[...])

    _token_pipeline(ids_ref, tab_ref, stage0, stage1, consume)


def _peer_specs(TB):
    R = PEER_SLOTS * ROW_SUB
    ids_spec = pl.BlockSpec((TB, PEER_SLOTS), lambda i: (i, 0), memory_space=pltpu.SMEM)
    table_spec = pl.BlockSpec(memory_space=pltpu.VMEM)
    stages = [pltpu.VMEM((R, 128), jnp.uint32), pltpu.VMEM((R, 128), jnp.uint32)]
    return ids_spec, table_spec, stages


def peer_scores(ids_flat, ht, tab):
    N = ht.shape[2]
    TB = PEER_TB
    ids_spec, table_spec, stages = _peer_specs(TB)
    return pl.pallas_call(
        _peer_u_kernel,
        grid=(N // TB,),
        in_specs=[ids_spec, pl.BlockSpec((ROW_SUB, 256, TB), lambda i: (0, 0, i)), table_spec],
        out_specs=pl.BlockSpec((PEER_SLOTS, TB), lambda i: (0, i)),
        out_shape=jax.ShapeDtypeStruct((PEER_SLOTS, N), F32),
        scratch_shapes=stages,
        compiler_params=_cp("arbitrary"),
        name="peer_scores",
    )(ids_flat, ht, tab)


def _peer_v_kernel(ids_ref, at_ref, gt_ref, tab_ref, rep_ref, o_ref, stage0, stage1, wexp):
    a = at_ref[...]
    w = gt_ref[...] * (0.5 * a * (1.0 + lax.erf(a * (1.0 / math.sqrt(2.0)))))
    wexp[...] = jnp.dot(w.T, rep_ref[...], precision=HIGHEST, preferred_element_type=F32)
    R = PEER_SLOTS * ROW_SUB
    own = (lax.broadcasted_iota(jnp.int32, (8, R), 0) ==
           lax.broadcasted_iota(jnp.int32, (8, R), 1) % ROW_SUB)

    def consume(t, stage):
        x = _unpack_rows(stage[...])
        wsel = jnp.where(own, wexp[pl.ds(t, 1), :], 0.0).astype(BF16)
        y = jnp.dot(wsel, x, preferred_element_type=F32)
        o_ref[t] = y[:ROW_SUB, :]

    _token_pipeline(ids_ref, tab_ref, stage0, stage1, consume)


def peer_combine(ids_flat, at, gt, tab, rep):
    N = at.shape[1]
    TB = PEER_TB
    R = PEER_SLOTS * ROW_SUB
    ids_spec, table_spec, stages = _peer_specs(TB)
    slot_tile = pl.BlockSpec((PEER_SLOTS, TB), lambda i: (0, i))
    return pl.pallas_call(
        _peer_v_kernel,
        grid=(N // TB,),
        in_specs=[ids_spec, slot_tile, slot_tile, table_spec, pl.BlockSpec(rep.shape, lambda i: (0, 0))],
        out_specs=pl.BlockSpec((TB, ROW_SUB, 256), lambda i: (i, 0, 0)),
        out_shape=jax.ShapeDtypeStruct((N, ROW_SUB, 256), F32),
        scratch_shapes=stages + [pltpu.VMEM((TB, R), F32)],
        compiler_params=_cp("arbitrary"),
        name="peer_combine",
    )(ids_flat, at, gt, tab, rep)


def _residual_kernel(x_ref, y_ref, g_ref, o_ref):
    o_ref[0] = x_ref[0] + g_ref[0] * y_ref[0]


def gated_residual(x, y, gate):
    B, L, D = x.shape
    T = min(1024, L)
    tile = pl.BlockSpec((1, T, D), lambda b, i: (b, i, 0))
    return pl.pallas_call(
        _residual_kernel,
        grid=(B, L // T),
        in_specs=[tile, tile, pl.BlockSpec((1, 1, D), lambda b, i: (b, 0, 0))],
        out_specs=tile,
        out_shape=jax.ShapeDtypeStruct((B, L, D), F32),
        compiler_params=_cp("parallel", "parallel"),
        name="gated_residual",
    )(x, y, gate)


def _pad_last(a, n):
    return jnp.pad(a, [(0, 0)] * (a.ndim - 1) + [(0, n - a.shape[-1])])


def _pack_table(tab):
    bits = lax.bitcast_convert_type(tab.astype(BF16), jnp.uint16).astype(jnp.uint32)
    bits = bits.reshape(tab.shape[0], ROW_SUB, 2, 128)
    return (bits[:, :, 0, :] | (bits[:, :, 1, :] << 16)).reshape(tab.shape[0] * ROW_SUB, 128)


def _layer_weights(l, norm1_gain, norm2_gain, w_in, pool_w, pool_scale, mla_q_norm, mla_kv_norm, w_uq, w_ukv,
                   q_norm, k_norm, ml_gate_bias, ml_out_norm, w_out, peer_wq, peer_subkeys, peer_u, peer_v):
    parts, start = [], 0
    for size in IN_SIZES:
        parts.append(w_in[l][:, start:start + size])
        start += size
    w_in_r = jnp.concatenate([parts[0], parts[1], parts[2], _pad_last(parts[3], 128), parts[4], parts[5],
                              parts[6], parts[7], _pad_last(parts[8], 128)], axis=1).astype(BF16)
    wbd = jnp.zeros((POOL_WIDTH, POOL_WIDTH), F32)
    for g in range(len(POOL_WINDOWS)):
        sl = slice(g * POOL_GROUP, (g + 1) * POOL_GROUP)
        wbd = wbd.at[sl, sl].set(pool_w[l, g])
    wq = _pad_last(w_uq[l].reshape(MLA_Q_RANK, MLA_HEADS, MLA_QK), HEAD_PAD).reshape(MLA_Q_RANK, -1)
    ukv = w_ukv[l].reshape(MLA_KV_RANK, MLA_HEADS, MLA_NOPE + MLA_V)
    wk = _pad_last(ukv[..., :MLA_NOPE], HEAD_PAD).reshape(MLA_KV_RANK, -1)
    wv = _pad_last(ukv[..., MLA_NOPE:], HEAD_PAD).reshape(MLA_KV_RANK, -1)
    r = jnp.arange(128)[:, None]
    cidx = jnp.arange(MLA_HEADS * HEAD_PAD)[None, :]
    pk = ((r < MLA_ROPE) & (cidx % HEAD_PAD == MLA_NOPE + r)).astype(F32)
    lane = jnp.arange(ML_WIDTH)
    hs = (lane[:, None] // ML_DK == jnp.arange(128)[None, :]).astype(F32)
    wo = w_out[l]
    wa = wo[POOL_WIDTH:POOL_WIDTH + MLA_HEADS * MLA_V]
    return dict(
        n1=norm1_gain[l], n2=norm2_gain[l], w_in=w_in_r,
        wbd=wbd.astype(BF16), pool_scale=pool_scale[l],
        qan=mla_q_norm[l], kvan=mla_kv_norm[l],
        wq=wq.astype(BF16), wk=wk.astype(BF16), wv=wv.astype(BF16), pk=pk,
        qg=_pad_last(q_norm[l], HEAD_PAD).reshape(1, HEAD_PAD), kg=_pad_last(k_norm[l], HEAD_PAD).reshape(1, HEAD_PAD),
        gate_bias=_pad_last(ml_gate_bias[l], 128).reshape(1, 128),
        mlg=ml_out_norm[l].reshape(1, ML_WIDTH), hs=hs, hst=hs.T,
        wp=wo[:POOL_WIDTH].astype(BF16), wa=wa.astype(BF16),
        wm=wo[POOL_WIDTH + MLA_HEADS * MLA_V:].astype(BF16),
        peer_wq=peer_wq[l].astype(BF16),
        sk=peer_subkeys[l].reshape(2 * PEER_HEADS, PEER_KEYS, PEER_KEYS),
        u_tab=_pack_table(peer_u[l]), v_tab=_pack_table(peer_v[l]),
    )


def _peer_constants():
    R = PEER_SLOTS * ROW_SUB
    pairs = _candidate_pairs()
    ncand = -(-len(pairs) // 8) * 8
    k1 = jnp.array([p[0] for p in pairs] + [-1] * (ncand - len(pairs)))[:, None]
    k2 = jnp.array([p[1] for p in pairs] + [-1] * (ncand - len(pairs)))[:, None]
    rank = jnp.arange(PEER_TOPK)[None, :]
    g1 = (k1 == rank).astype(F32)
    g2 = (k2 == rank).astype(F32)
    rep = (jnp.arange(PEER_SLOTS)[:, None] == jnp.arange(R)[None, :] // ROW_SUB).astype(F32)
    return g1, g2, rep


def _rope_tables(n):
    n_rows = n // GRID_W
    row = jnp.repeat(jnp.arange(n_rows), GRID_W, total_repeat_length=n).astype(F32)
    col = (jnp.arange(n) % GRID_W).astype(F32)
    per_axis = MLA_ROPE // 2
    freqs = ROPE_BASE ** (-jnp.arange(0, per_axis, 2, dtype=F32) / per_axis)
    ang = jnp.concatenate([row[:, None] * freqs, col[:, None] * freqs], axis=-1)
    c, s = jnp.cos(ang), jnp.sin(ang)
    cos = jnp.concatenate([jnp.ones((n, MLA_NOPE), F32), c, c, jnp.ones((n, HEAD_PAD - MLA_QK), F32)], axis=-1)
    sin = jnp.concatenate([jnp.zeros((n, MLA_NOPE), F32), -s, s, jnp.zeros((n, HEAD_PAD - MLA_QK), F32)], axis=-1)
    return cos, sin


def _peer_ffn(x, w, shift, scale, gate, consts):
    B, L, D = x.shape
    g1, g2, rep = consts
    qp_t, h_t = peer_query(x, w["n2"], shift, scale, w["peer_wq"])
    ids_t, gates_t = peer_topk(qp_t, w["sk"], g1, g2)
    ids_flat = ids_t.T * ROW_SUB
    a_t = peer_scores(ids_flat, h_t.reshape(ROW_SUB, 256, B * L), w["u_tab"])
    y = peer_combine(ids_flat, a_t, gates_t, w["v_tab"], rep)
    return gated_residual(x, y.reshape(B, L, D), gate)


def kernel(x, c, ctx, c_ctx, norm1_gain, norm2_gain, w_ada, b_ada, w_in, pool_w, pool_scale, mla_q_norm,
           mla_kv_norm, w_uq, w_ukv, q_norm, k_norm, ml_gate_bias, ml_out_norm, w_out, peer_wq, peer_subkeys,
           peer_u, peer_v):
    B, S, D = x.shape
    Lc_ctx = ctx.shape[1]
    depth = w_in.shape[0]
    cos_x, sin_x = _rope_tables(S)
    cos_c = jnp.ones((Lc_ctx, HEAD_PAD), F32)
    sin_c = jnp.zeros((Lc_ctx, HEAD_PAD), F32)
    consts = _peer_constants()
    cond8 = jnp.zeros((8, D), F32).at[:B].set(c).at[B].set(c_ctx)
    W = ML_WIDTH
    zero_state = (jnp.zeros((B, 2, W, W), F32), jnp.zeros((B, 2, 1, W), F32),
                  jnp.full((B, 2, 1, W), NEG_INIT, F32))
    x_ctx = ctx
    for l in range(depth):
        last = l == depth - 1
        w = _layer_weights(l, norm1_gain, norm2_gain, w_in, pool_w, pool_scale, mla_q_norm, mla_kv_norm, w_uq,
                           w_ukv, q_norm, k_norm, ml_gate_bias, ml_out_norm, w_out, peer_wq, peer_subkeys,
                           peer_u, peer_v)
        mods = ada_mod(cond8, w_ada[l], b_ada[l])
        m_x = [mods[:B, i * D:(i + 1) * D].reshape(B, 1, D) for i in range(6)]
        m_c = [jnp.broadcast_to(mods[B, i * D:(i + 1) * D].reshape(1, 1, D), (B, 1, D)) for i in range(6)]

        zp_c, zm_c, zl_c = in_proj(x_ctx, w["n1"], m_c[0], m_c[1], w["w_in"])
        qc, kc, vc = mla_qkv(zm_c, w["qan"], w["kvan"], w["wq"], w["wk"], w["wv"], w["pk"], w["qg"], w["kg"],
                             cos_c, sin_c)
        hf_c, hb_c, cT, nT, mT = mlstm(zl_c, w["gate_bias"], zero_state)
        if not last:
            pool_c = pool_mixer(zp_c, w["wbd"], w["pool_scale"])
            attn_c = attention(qc, kc, vc)
            xc = out_proj(x_ctx, pool_c, attn_c, hf_c, hb_c, zl_c, w["mlg"], w["hs"], w["hst"],
                          w["wp"], w["wa"], w["wm"], m_c[2])
            x_ctx_new = _peer_ffn(xc, w, m_c[3], m_c[4], m_c[5], consts)

        zp, zm, zl = in_proj(x, w["n1"], m_x[0], m_x[1], w["w_in"])
        pool_x = pool_mixer(zp, w["wbd"], w["pool_scale"])
        q, k, v = mla_qkv(zm, w["qan"], w["kvan"], w["wq"], w["wk"], w["wv"], w["pk"], w["qg"], w["kg"],
                          cos_x, sin_x)
        attn_x = attention(q, kc, vc, k, v)
        hf, hb, _, _, _ = mlstm(zl, w["gate_bias"], (cT, nT, mT))
        x = out_proj(x, pool_x, attn_x, hf, hb, zl, w["mlg"], w["hs"], w["hst"],
                     w["wp"], w["wa"], w["wm"], m_x[2])
        x = _peer_ffn(x, w, m_x[3], m_x[4], m_x[5], consts)
        if not last:
            x_ctx = x_ctx_new
    return x
```

```python
import functools
import math

import jax
import jax.numpy as jnp
from jax import lax
from jax.experimental import pallas as pl
from jax.experimental.pallas import tpu as pltpu

F32 = jnp.float32
BF16 = jnp.bfloat16
HIGHEST = lax.Precision.HIGHEST

EPS = 1e-6
D_MODEL = 1024
GRID_W = 64
ROPE_BASE = 10000.0
POOL_WINDOWS = (2, 4, 8, 16)
POOL_GROUP = 64
POOL_WIDTH = 256
MLA_HEADS = 8
MLA_NOPE = 64
MLA_ROPE = 32
MLA_QK = 96
MLA_V = 64
MLA_Q_RANK = 384
MLA_KV_RANK = 256
MLA_SCALE = MLA_QK ** -0.5
Q_PRESCALE = MLA_SCALE * math.log2(math.e)
HEAD_PAD = 128
ATT_TQ = 512
ATT_TC = 512
VT_ROWS = 80
ML_HEADS = 4
ML_DK = 64
ML_WIDTH = 256
ML_CHUNK = 128
NEG_INIT = -1e30
IN_SIZES = (256, 384, 256, 32, 256, 256, 256, 256, 16)
PEER_HEADS = 8
PEER_KEYS = 128
PEER_TOPK = 16
PEER_SLOTS = PEER_HEADS * PEER_TOPK
PEER_TB = 128
PEER_RING = 8
ROW_SUB = 4

Z_POOL, Z_MLA, Z_ML = 256, 768, 1152
VMEM_LIMIT = 56 * 1024 * 1024


def _cp(*sem, vmem=VMEM_LIMIT):
    return pltpu.CompilerParams(dimension_semantics=sem, vmem_limit_bytes=vmem)


def _ada_kernel(c_ref, w_ref, b_ref, o_ref):
    c = c_ref[...]
    s = c * (1.0 / (1.0 + jnp.exp(-c)))
    o_ref[...] = jnp.dot(s, w_ref[...], precision=HIGHEST, preferred_element_type=F32) + b_ref[...]


def ada_mod(cond8, w, b):
    n = w.shape[1]
    tn = n // 4
    return pl.pallas_call(
        _ada_kernel,
        grid=(n // tn,),
        in_specs=[pl.BlockSpec((8, D_MODEL), lambda j: (0, 0)),
                  pl.BlockSpec((D_MODEL, tn), lambda j: (0, j)),
                  pl.BlockSpec((1, tn), lambda j: (0, j))],
        out_specs=pl.BlockSpec((8, tn), lambda j: (0, j)),
        out_shape=jax.ShapeDtypeStruct((8, n), F32),
        compiler_params=_cp("arbitrary"),
        name="ada_mod",
    )(cond8, w, b.reshape(1, n))


def _modulated(x, gain, shift, scale):
    ms = jnp.mean(x * x, axis=-1, keepdims=True)
    y = x * lax.rsqrt(ms + EPS) * gain
    return y * (1.0 + scale) + shift


def _inproj_kernel(x_ref, gain_ref, shift_ref, scale_ref, w_ref, zp_ref, zm_ref, zl_ref):
    h = _modulated(x_ref[0], gain_ref[...], shift_ref[0], scale_ref[0])
    res = jnp.dot(h.astype(BF16), w_ref[...], preferred_element_type=F32)
    zp_ref[0] = res[:, :Z_POOL]
    zm_ref[0] = res[:, Z_POOL:Z_POOL + Z_MLA]
    zl_ref[0] = res[:, Z_POOL + Z_MLA:]


def in_proj(x, gain, shift, scale, w):
    B, L, D = x.shape
    tm = min(512, L)
    n = w.shape[1]
    vec = pl.BlockSpec((1, 1, D), lambda b, i: (b, 0, 0))
    return pl.pallas_call(
        _inproj_kernel,
        grid=(B, L // tm),
        in_specs=[pl.BlockSpec((1, tm, D), lambda b, i: (b, i, 0)),
                  pl.BlockSpec((1, D), lambda b, i: (0, 0)),
                  vec, vec,
                  pl.BlockSpec((D, n), lambda b, i: (0, 0))],
        out_specs=[pl.BlockSpec((1, tm, Z_POOL), lambda b, i: (b, i, 0)),
                   pl.BlockSpec((1, tm, Z_MLA), lambda b, i: (b, i, 0)),
                   pl.BlockSpec((1, tm, Z_ML), lambda b, i: (b, i, 0))],
        out_shape=[jax.ShapeDtypeStruct((B, L, Z_POOL), F32),
                   jax.ShapeDtypeStruct((B, L, Z_MLA), F32),
                   jax.ShapeDtypeStruct((B, L, Z_ML), F32)],
        compiler_params=_cp("parallel", "parallel"),
        name="in_proj",
    )(x, gain.reshape(1, D), shift, scale, w)


def _peerq_kernel(x_ref, gain_ref, shift_ref, scale_ref, w_ref, qp_ref, h_ref):
    h = _modulated(x_ref[0], gain_ref[...], shift_ref[0], scale_ref[0])
    h_ref[...] = h.T.astype(BF16)
    res = jnp.dot(h.astype(BF16), w_ref[...], preferred_element_type=F32)
    for g in range(2 * PEER_HEADS):
        qp_ref[g] = res[:, g * PEER_KEYS:(g + 1) * PEER_KEYS].T


def peer_query(x, gain, shift, scale, w):
    B, L, D = x.shape
    tm = min(512, L)
    nb = L // tm
    n = w.shape[1]
    vec = pl.BlockSpec((1, 1, D), lambda b, i: (b, 0, 0))
    return pl.pallas_call(
        _peerq_kernel,
        grid=(B, nb),
        in_specs=[pl.BlockSpec((1, tm, D), lambda b, i: (b, i, 0)),
                  pl.BlockSpec((1, D), lambda b, i: (0, 0)),
                  vec, vec,
                  pl.BlockSpec((D, n), lambda b, i: (0, 0))],
        out_specs=[pl.BlockSpec((2 * PEER_HEADS, PEER_KEYS, tm), lambda b, i: (0, 0, b * nb + i)),
                   pl.BlockSpec((D, tm), lambda b, i: (0, b * nb + i))],
        out_shape=[jax.ShapeDtypeStruct((2 * PEER_HEADS, PEER_KEYS, B * L), F32),
                   jax.ShapeDtypeStruct((D, B * L), BF16)],
        compiler_params=_cp("parallel", "parallel"),
        name="peer_query",
    )(x, gain.reshape(1, D), shift, scale, w)


def _pool_kernel(p_ref, c_ref, n_ref, wbd_ref, sc_ref, o_ref, *, L, T):
    i = pl.program_id(1)
    cur = c_ref[0]
    u3 = jnp.concatenate([p_ref[0], cur, n_ref[0]], axis=0).astype(BF16)
    t = i * T + lax.broadcasted_iota(jnp.int32, (T, 3 * T), 0)
    s = (i - 1) * T + lax.broadcasted_iota(jnp.int32, (T, 3 * T), 1)
    lane = lax.broadcasted_iota(jnp.int32, (T, POOL_WIDTH), 1)
    trow = i * T + lax.broadcasted_iota(jnp.int32, (T, POOL_WIDTH), 0)
    win = jnp.zeros((T, POOL_WIDTH), F32)
    for g, w in enumerate(POOL_WINDOWS):
        lo = jnp.maximum(t - w // 2, 0)
        hi = jnp.minimum(t + w // 2, L)
        band = jnp.where((s >= lo) & (s < hi), 1.0, 0.0).astype(BF16)
        ws = jnp.dot(band, u3, preferred_element_type=F32)
        cnt = (jnp.minimum(trow + w // 2, L) - jnp.maximum(trow - w // 2, 0)).astype(F32)
        in_group = (lane >= g * POOL_GROUP) & (lane < (g + 1) * POOL_GROUP)
        win = jnp.where(in_group, ws / cnt, win)
    d = win - cur
    y = jnp.dot(d.astype(BF16), wbd_ref[...], preferred_element_type=F32)
    o_ref[0] = y * sc_ref[...]


def pool_mixer(z_pool, wbd, scale):
    B, L, C = z_pool.shape
    T = 256
    nb = L // T
    return pl.pallas_call(
        functools.partial(_pool_kernel, L=L, T=T),
        grid=(B, nb),
        in_specs=[pl.BlockSpec((1, T, C), lambda b, i: (b, jnp.maximum(i - 1, 0), 0)),
                  pl.BlockSpec((1, T, C), lambda b, i: (b, i, 0)),
                  pl.BlockSpec((1, T, C), lambda b, i: (b, jnp.minimum(i + 1, nb - 1), 0)),
                  pl.BlockSpec((C, C), lambda b, i: (0, 0)),
                  pl.BlockSpec((1, C), lambda b, i: (0, 0))],
        out_specs=pl.BlockSpec((1, T, C), lambda b, i: (b, i, 0)),
        out_shape=jax.ShapeDtypeStruct((B, L, C), F32),
        compiler_params=_cp("parallel", "parallel"),
        name="pool_mixer",
    )(z_pool, z_pool, z_pool, wbd, scale.reshape(1, C))


def _rms(x, gain, n):
    ss = jnp.sum(x * x, axis=-1, keepdims=True) * (1.0 / n)
    return x * lax.rsqrt(ss + EPS) * gain


def _mla_kernel(z_ref, qan_ref, kvan_ref, wq_ref, wk_ref, wv_ref, pk_ref, qg_ref, kg_ref,
                cos_ref, sin_ref, q_out, k_out, v_out):
    z = z_ref[0]
    zq = z[:, :MLA_Q_RANK]
    zkv = z[:, MLA_Q_RANK:MLA_Q_RANK + MLA_KV_RANK]
    zkr = z[:, MLA_Q_RANK + MLA_KV_RANK:]
    nq = _rms(zq, qan_ref[...], MLA_Q_RANK).astype(BF16)
    nkv = _rms(zkv, kvan_ref[...], MLA_KV_RANK).astype(BF16)
    qp = jnp.dot(nq, wq_ref[...], preferred_element_type=F32)
    kp = jnp.dot(nkv, wk_ref[...], preferred_element_type=F32)
    kp = kp + jnp.dot(zkr, pk_ref[...], precision=HIGHEST, preferred_element_type=F32)
    vp = jnp.dot(nkv, wv_ref[...], preferred_element_type=F32)
    cos = cos_ref[...]
    sin = sin_ref[...]
    lane = lax.broadcasted_iota(jnp.int32, cos.shape, 1)
    first_half = lane < MLA_NOPE + MLA_ROPE // 2
    extra = (VT_ROWS - MLA_V, z.shape[0])
    ones_rows = jnp.where(lax.broadcasted_iota(jnp.int32, extra, 0) == 0, 1.0, 0.0)
    for h in range(MLA_HEADS):
        sl = slice(h * HEAD_PAD, (h + 1) * HEAD_PAD)
        for src, gain_ref, out, mult in ((qp, qg_ref, q_out, Q_PRESCALE), (kp, kg_ref, k_out, 1.0)):
            xn = _rms(src[:, sl], gain_ref[...], MLA_QK)
            partner = jnp.where(first_half, pltpu.roll(xn, HEAD_PAD - MLA_ROPE // 2, 1),
                                pltpu.roll(xn, MLA_ROPE // 2, 1))
            xr = xn * cos + partner * sin
            out[0, h] = (xr * mult).astype(BF16)
        v_out[0, h] = jnp.concatenate([vp[:, sl].T[:MLA_V, :], ones_rows], axis=0).astype(BF16)


def mla_qkv(z_mla, qan, kvan, wq, wk, wv, pk, qg, kg, cos, sin):
    B, L, _ = z_mla.shape
    T = 256
    full = lambda shape: pl.BlockSpec(shape, lambda b, i: (0,) * len(shape))
    head_out = pl.BlockSpec((1, MLA_HEADS, T, HEAD_PAD), lambda b, i: (b, 0, i, 0))
    out_sds = jax.ShapeDtypeStruct((B, MLA_HEADS, L, HEAD_PAD), BF16)
    vt_out = pl.BlockSpec((1, MLA_HEADS, VT_ROWS, T), lambda b, i: (b, 0, 0, i))
    vt_sds = jax.ShapeDtypeStruct((B, MLA_HEADS, VT_ROWS, L), BF16)
    return pl.pallas_call(
        _mla_kernel,
        grid=(B, L // T),
        in_specs=[pl.BlockSpec((1, T, Z_MLA), lambda b, i: (b, i, 0)),
                  full((1, MLA_Q_RANK)), full((1, MLA_KV_RANK)),
                  full(wq.shape), full(wk.shape), full(wv.shape), full(pk.shape),
                  full((1, HEAD_PAD)), full((1, HEAD_PAD)),
                  pl.BlockSpec((T, HEAD_PAD), lambda b, i: (i, 0)),
                  pl.BlockSpec((T, HEAD_PAD), lambda b, i: (i, 0))],
        out_specs=[head_out, head_out, vt_out],
        out_shape=[out_sds, out_sds, vt_sds],
        compiler_params=_cp("parallel", "parallel"),
        name="mla_qkv",
    )(z_mla, qan.reshape(1, -1), kvan.reshape(1, -1), wq, wk, wv, pk, qg, kg, cos, sin)


def _flash_kernel(q_ref, kc_ref, vct_ref, *rest, nchunks):
    if nchunks:
        k_ref, vt_ref, o_ref, s_s, m_s, acc_s = rest
    else:
        o_ref, s_s, m_s, acc_s = rest
    nc = kc_ref.shape[2]

    def scores(h, kblk):
        n = kblk.shape[0]
        s_s[h, :n] = lax.dot_general(kblk, q_ref[0, h], (((1,), (1,)), ((), ())), preferred_element_type=F32)

    def update(h, n, vtblk):
        st = s_s[h, :n]
        m = m_s[h]
        m_new = jnp.maximum(m, jnp.max(st, axis=0, keepdims=True))
        p = jnp.exp2(st - m_new).astype(BF16)
        m_s[h] = m_new
        acc_s[h] = jnp.exp2(m - m_new) * acc_s[h] + jnp.dot(vtblk, p, preferred_element_type=F32)

    m_s[...] = jnp.full(m_s.shape, -jnp.inf, F32)
    acc_s[...] = jnp.zeros(acc_s.shape, F32)
    scores(0, kc_ref[0, 0])
    scores(1, kc_ref[0, 1])
    update(0, nc, vct_ref[0, 0])
    if nchunks:
        scores(0, k_ref[0, 0, pl.ds(0, ATT_TC), :])
    update(1, nc, vct_ref[0, 1])
    if nchunks:
        def body(c, _):
            off = pl.multiple_of(c * ATT_TC, ATT_TC)
            nxt = pl.multiple_of(jnp.minimum(c + 1, nchunks - 1) * ATT_TC, ATT_TC)
            scores(1, k_ref[0, 1, pl.ds(off, ATT_TC), :])
            update(0, ATT_TC, vt_ref[0, 0, :, pl.ds(off, ATT_TC)])
            scores(0, k_ref[0, 0, pl.ds(nxt, ATT_TC), :])
            update(1, ATT_TC, vt_ref[0, 1, :, pl.ds(off, ATT_TC)])
            return 0
        lax.fori_loop(0, nchunks, body, 0, unroll=8)
    o = jnp.concatenate([acc_s[h, :MLA_V] / acc_s[h, MLA_V:MLA_V + 1] for h in range(2)], axis=0)
    o_ref[0] = o.T.astype(BF16)


def attention(q, kc, vct, k=None, vt=None):
    B, H, Lq, _ = q.shape
    Lc = kc.shape[2]
    tq = min(ATT_TQ, Lq)
    pair4 = lambda n, d: pl.BlockSpec((1, 2, n, d), lambda b, h, i: (b, h, 0, 0))
    in_specs = [pl.BlockSpec((1, 2, tq, HEAD_PAD), lambda b, h, i: (b, h, i, 0)),
                pair4(Lc, HEAD_PAD), pair4(VT_ROWS, Lc)]
    args = [q, kc, vct]
    nchunks = 0
    if k is not None:
        Lk = k.shape[2]
        nchunks = Lk // ATT_TC
        in_specs += [pair4(Lk, HEAD_PAD), pair4(VT_ROWS, Lk)]
        args += [k, vt]
    return pl.pallas_call(
        functools.partial(_flash_kernel, nchunks=nchunks),
        grid=(B, H // 2, Lq // tq),
        in_specs=in_specs,
        out_specs=pl.BlockSpec((1, tq, 2 * MLA_V), lambda b, h, i: (b, i, h)),
        out_shape=jax.ShapeDtypeStruct((B, Lq, H * MLA_V), BF16),
        scratch_shapes=[pltpu.VMEM((2, max(ATT_TC, Lc), tq), F32), pltpu.VMEM((2, 1, tq), F32),
                        pltpu.VMEM((2, VT_ROWS, tq), F32)],
        compiler_params=_cp("parallel", "parallel", "arbitrary"),
        name="attention",
    )(*args)


def _log_sigmoid(x):
    return jnp.minimum(x, 0.0) - jnp.log(1.0 + jnp.exp(-jnp.abs(x)))


def _mlstm_direction(d, q, k, v, g, C_s, n_s, m_s):
    Lc = q.shape[0]
    row = lax.broadcasted_iota(jnp.int32, (Lc, Lc), 0)
    col = lax.broadcasted_iota(jnp.int32, (Lc, Lc), 1)
    tri = (col <= row) if d == 0 else (col >= row)
    logf = _log_sigmoid(g)
    bcol = jnp.dot(jnp.where(tri, 1.0, 0.0), logf, precision=HIGHEST, preferred_element_type=F32)
    bT = bcol.T
    gT = g.T
    lane = lax.broadcasted_iota(jnp.int32, (1, ML_WIDTH), 1)
    kb = k.astype(BF16)
    vb = v.astype(BF16)
    Cst = C_s[d]
    nst = n_s[d]
    mst = m_s[d]
    qc = jnp.dot(q.astype(BF16), Cst.astype(BF16), preferred_element_type=F32)
    out = jnp.zeros((Lc, ML_WIDTH), F32)
    ws_all = jnp.zeros((Lc, ML_WIDTH), F32)
    wprev_all = jnp.zeros((1, ML_WIDTH), F32)
    mnew_all = jnp.zeros((1, ML_WIDTH), F32)
    for h in range(ML_HEADS):
        il = 8 * d + h
        fl = 8 * d + 4 + h
        head = (lane >= h * ML_DK) & (lane < (h + 1) * ML_DK)
        bc = bcol[:, fl:fl + 1]
        br = bT[fl:fl + 1, :]
        ir = gT[il:il + 1, :]
        ic = g[:, il:il + 1]
        mprev = mst[:, h * ML_DK:h * ML_DK + 1]
        dmat = jnp.where(tri, bc - br + ir, -jnp.inf)
        inter = bc + mprev
        mj = jnp.maximum(inter, jnp.max(dmat, axis=-1, keepdims=True))
        w_inter = jnp.exp(inter - mj)
        qh = jnp.where(head, q, 0.0)
        s = lax.dot_general(qh.astype(BF16), kb, (((1,), (1,)), ((), ())), preferred_element_type=F32)
        qk = s * jnp.exp(dmat - mj)
        pv = jnp.dot(qk.astype(BF16), vb, preferred_element_type=F32)
        qn = jnp.sum(qh * nst, axis=-1, keepdims=True)
        den = jnp.sum(qk, axis=-1, keepdims=True) + w_inter * qn
        denom = jnp.maximum(jnp.abs(den), jnp.exp(-mj))
        out = jnp.where(head, (pv + qc * w_inter) / denom, out)
        blast = bc[Lc - 1:Lc, :] if d == 0 else bc[0:1, :]
        dec = blast - bc + ic
        mnew = jnp.maximum(blast + mprev, jnp.max(dec, axis=0, keepdims=True))
        wprev = jnp.exp(blast + mprev - mnew)
        ws = jnp.exp(dec - mnew)
        ws_all = jnp.where(head, ws, ws_all)
        wprev_all = jnp.where(head, wprev, wprev_all)
        mnew_all = jnp.where(head, mnew, mnew_all)
    kw = k * ws_all
    upd = jnp.dot(kw.T.astype(BF16), vb, preferred_element_type=F32)
    r2 = lax.broadcasted_iota(jnp.int32, (ML_WIDTH, ML_WIDTH), 0) // ML_DK
    c2 = lax.broadcasted_iota(jnp.int32, (ML_WIDTH, ML_WIDTH), 1) // ML_DK
    C_s[d] = Cst * wprev_all + jnp.where(r2 == c2, upd, 0.0)
    n_s[d] = nst * wprev_all + jnp.sum(kw, axis=0, keepdims=True)
    m_s[d] = mnew_all
    return out


def _mlstm_kernel(qf_ref, kf_ref, vf_ref, gf_ref, qb_ref, kb_ref, vb_ref, gb_ref, bias_ref,
                  c0_ref, n0_ref, m0_ref, hf_ref, hb_ref, cT_ref, nT_ref, mT_ref, C_s, n_s, m_s):
    c = pl.program_id(1)

    @pl.when(c == 0)
    def _():
        C_s[...] = c0_ref[0]
        n_s[...] = n0_ref[0]
        m_s[...] = m0_ref[0]

    scale = ML_DK ** -0.5
    hf_ref[0] = _mlstm_direction(0, qf_ref[0] * scale, kf_ref[0], vf_ref[0],
                                 gf_ref[0] + bias_ref[...], C_s, n_s, m_s)
    hb_ref[0] = _mlstm_direction(1, qb_ref[0] * scale, kb_ref[0], vb_ref[0],
                                 gb_ref[0] + bias_ref[...], C_s, n_s, m_s)

    @pl.when(c == pl.num_programs(1) - 1)
    def _():
        cT_ref[0] = C_s[...]
        nT_ref[0] = n_s[...]
        mT_ref[0] = m_s[...]


def mlstm(z_ml, bias, state):
    B, L, _ = z_ml.shape
    Lc = ML_CHUNK
    nc = L // Lc
    c0, n0, m0 = state
    W = ML_WIDTH
    fwd = lambda j: pl.BlockSpec((1, Lc, W), lambda b, c: (b, c, j))
    bwd = lambda j: pl.BlockSpec((1, Lc, W), lambda b, c: (b, nc - 1 - c, j))
    gcol = 4 * W // 128
    st_c = pl.BlockSpec((1, 2, W, W), lambda b, c: (b, 0, 0, 0))
    st_v = pl.BlockSpec((1, 2, 1, W), lambda b, c: (b, 0, 0, 0))
    return pl.pallas_call(
        _mlstm_kernel,
        grid=(B, nc),
        in_specs=[fwd(0), fwd(1), fwd(2), pl.BlockSpec((1, Lc, 128), lambda b, c: (b, c, gcol)),
                  bwd(0), bwd(1), bwd(2), pl.BlockSpec((1, Lc, 128), lambda b, c: (b, nc - 1 - c, gcol)),
                  pl.BlockSpec((1, 128), lambda b, c: (0, 0)),
                  st_c, st_v, st_v],
        out_specs=[pl.BlockSpec((1, Lc, W), lambda b, c: (b, c, 0)),
                   pl.BlockSpec((1, Lc, W), lambda b, c: (b, nc - 1 - c, 0)),
                   st_c, st_v, st_v],
        out_shape=[jax.ShapeDtypeStruct((B, L, W), F32), jax.ShapeDtypeStruct((B, L, W), F32),
                   jax.ShapeDtypeStruct((B, 2, W, W), F32), jax.ShapeDtypeStruct((B, 2, 1, W), F32),
                   jax.ShapeDtypeStruct((B, 2, 1, W), F32)],
        scratch_shapes=[pltpu.VMEM((2, W, W), F32), pltpu.VMEM((2, 1, W), F32), pltpu.VMEM((2, 1, W), F32)],
        compiler_params=_cp("parallel", "arbitrary"),
        name="mlstm",
    )(z_ml, z_ml, z_ml, z_ml, z_ml, z_ml, z_ml, z_ml, bias, c0, n0, m0)


def _outproj_kernel(x_ref, pool_ref, attn_ref, hf_ref, hb_ref, op_ref, mlg_ref, hs_ref, hst_ref,
                    wp_ref, wa_ref, wm_ref, ga_ref, o_ref):
    h = hf_ref[0] + hb_ref[0]
    ss = jnp.dot(h * h, hs_ref[...], precision=HIGHEST, preferred_element_type=F32) * (1.0 / ML_DK)
    inv = jnp.dot(lax.rsqrt(ss + EPS), hst_ref[...], precision=HIGHEST, preferred_element_type=F32)
    op = op_ref[0]
    ml = h * inv * mlg_ref[...] * (1.0 / (1.0 + jnp.exp(-op)))
    mix = jnp.dot(pool_ref[0].astype(BF16), wp_ref[...], preferred_element_type=F32)
    mix += jnp.dot(attn_ref[0], wa_ref[...], preferred_element_type=F32)
    mix += jnp.dot(ml.astype(BF16), wm_ref[...], preferred_element_type=F32)
    o_ref[0] = x_ref[0] + ga_ref[0] * mix


def out_proj(x, pool, attn, hf, hb, z_ml, mlg, hs, hst, wp, wa, wm, gate):
    B, L, D = x.shape
    T = min(512, L)
    W = ML_WIDTH
    tile = lambda w, j=0: pl.BlockSpec((1, T, w), lambda b, i: (b, i, j))
    full = lambda shape: pl.BlockSpec(shape, lambda b, i: (0,) * len(shape))
    return pl.pallas_call(
        _outproj_kernel,
        grid=(B, L // T),
        in_specs=[tile(D), tile(W), tile(MLA_HEADS * MLA_V), tile(W), tile(W), tile(W, 3),
                  full((1, W)), full(hs.shape), full(hst.shape),
                  full(wp.shape), full(wa.shape), full(wm.shape),
                  pl.BlockSpec((1, 1, D), lambda b, i: (b, 0, 0))],
        out_specs=tile(D),
        out_shape=jax.ShapeDtypeStruct((B, L, D), F32),
        compiler_params=_cp("parallel", "parallel"),
        name="out_proj",
    )(x, pool, attn, hf, hb, z_ml, mlg, hs, hst, wp, wa, wm, gate)


def _top16_rows(s, row):
    big = float(s.shape[0])
    vals, idxs = [], []
    for _ in range(PEER_TOPK):
        m = jnp.max(s, axis=0, keepdims=True)
        idx = jnp.min(jnp.where(s == m, row, big), axis=0, keepdims=True)
        vals.append(m)
        idxs.append(idx)
        s = jnp.where(row == idx, -jnp.inf, s)
    return jnp.concatenate(vals, axis=0), jnp.concatenate(idxs, axis=0)


def _candidate_pairs():
    return [(k1, k2) for k1 in range(PEER_TOPK) for k2 in range(PEER_TOPK) if (k1 + 1) * (k2 + 1) <= PEER_TOPK]


def _topk_kernel(qp_ref, sk_ref, g1_ref, g2_ref, ids_ref, gate_ref):
    T = 256
    row = lax.broadcasted_iota(jnp.int32, (PEER_KEYS, T), 0).astype(F32)
    pick = lambda g_ref, a: jnp.dot(g_ref[...], a, precision=HIGHEST, preferred_element_type=F32)
    ncand = g1_ref.shape[0]
    crow = lax.broadcasted_iota(jnp.int32, (ncand, T), 0).astype(F32)

    def tile(lt, _):
        cols = pl.ds(pl.multiple_of(lt * T, T), T)
        tops = []
        for c in range(2):
            s = jnp.dot(sk_ref[c], qp_ref[c, :, cols], precision=HIGHEST, preferred_element_type=F32)
            tops.append(_top16_rows(s, row))
        cand = pick(g1_ref, tops[0][0]) + pick(g2_ref, tops[1][0])
        cand = jnp.where(crow < float(len(_candidate_pairs())), cand, -jnp.inf)
        expert = pick(g1_ref, tops[0][1]) * float(PEER_KEYS) + pick(g2_ref, tops[1][1])
        best, eids = [], []
        for _ in range(PEER_TOPK):
            m = jnp.max(cand, axis=0, keepdims=True)
            ci = jnp.min(jnp.where(cand == m, crow, float(ncand)), axis=0, keepdims=True)
            sel = crow == ci
            best.append(m)
            eids.append(jnp.sum(jnp.where(sel, expert, 0.0), axis=0, keepdims=True))
            cand = jnp.where(sel, -jnp.inf, cand)
        best = jnp.concatenate(best, axis=0)
        p = jnp.exp(best - best[0:1, :])
        gate_ref[:, cols] = p / jnp.sum(p, axis=0, keepdims=True)
        ids_ref[:, cols] = jnp.concatenate(eids, axis=0).astype(jnp.int32)
        return 0

    lax.fori_loop(0, qp_ref.shape[2] // T, tile, 0)


def peer_topk(qp_t, sk, g1, g2):
    N = qp_t.shape[2]
    T = 512
    return pl.pallas_call(
        _topk_kernel,
        grid=(N // T, PEER_HEADS),
        in_specs=[pl.BlockSpec((2, PEER_KEYS, T), lambda i, h: (h, 0, i)),
                  pl.BlockSpec((2, PEER_KEYS, PEER_KEYS), lambda i, h: (h, 0, 0)),
                  pl.BlockSpec(g1.shape, lambda i, h: (0, 0)),
                  pl.BlockSpec(g2.shape, lambda i, h: (0, 0))],
        out_specs=[pl.BlockSpec((PEER_TOPK, T), lambda i, h: (h, i)),
                   pl.BlockSpec((PEER_TOPK, T), lambda i, h: (h, i))],
        out_shape=[jax.ShapeDtypeStruct((PEER_SLOTS, N), jnp.int32),
                   jax.ShapeDtypeStruct((PEER_SLOTS, N), F32)],
        compiler_params=_cp("parallel", "parallel"),
        name="peer_topk",
    )(qp_t, sk, g1, g2)


def _gather_rows(ring, half, row, tab_ref, stage):
    for j in range(PEER_SLOTS):
        r = ring[half, row, j]
        stage[pl.ds(ROW_SUB * j, ROW_SUB), :] = tab_ref[pl.ds(r, ROW_SUB), :]


def _unpack_rows(x):
    lo = pltpu.bitcast(x << 16, F32)
    hi = pltpu.bitcast(x & jnp.uint32(0xFFFF0000), F32)
    return jnp.concatenate([lo, hi], axis=-1).astype(BF16)


def _token_pipeline(ids_ref, tab_ref, stages, ring, sems, consume):
    TB, R = PEER_TB, PEER_RING

    def group_copy(k, half):
        first = pl.multiple_of((k + 1) * R, R)
        return pltpu.make_async_copy(ids_ref.at[0, pl.ds(first, R)], ring.at[half], sems.at[half])

    def head_copy():
        return pltpu.make_async_copy(ids_ref.at[0, pl.ds(0, R)], ring.at[1], sems.at[1])

    head_copy().start()
    head_copy().wait()
    group_copy(0, 0).start()
    _gather_rows(ring, 1, 0, tab_ref, stages[0])
    _gather_rows(ring, 1, 1, tab_ref, stages[1])

    def body(i, _):
        for half in range(2):
            k = 2 * i + half
            group_copy(k, half).wait()
            group_copy(k + 1, 1 - half).start()
            for s in range(R):
                consume(k * R + s, stages[s % 2])
                _gather_rows(ring, half, s, tab_ref, stages[s % 2])
        return 0

    lax.fori_loop(0, TB // (2 * R), body, 0)
    group_copy(TB // R, 0).wait()


def _peer_u_kernel(ids_ref, ht_ref, tab_ref, at_ref, stage0, stage1, ring, sems):
    lane_t = lax.broadcasted_iota(jnp.int32, at_ref.shape, 1)
    at_ref[...] = jnp.zeros(at_ref.shape, F32)

    def consume(t, stage):
        r = jnp.zeros(at_ref.shape, F32)
        for s in range(ROW_SUB):
            xs = _unpack_rows(stage[pl.ds(s, PEER_SLOTS, stride=ROW_SUB), :])
            r = r + jnp.dot(xs, ht_ref[s], preferred_element_type=F32)
        at_ref[...] = jnp.where(lane_t == t, r, at_ref[...])

    _token_pipeline(ids_ref, tab_ref, (stage0, stage1), ring, sems, consume)


def _peer_specs(TB):
    R = PEER_SLOTS * ROW_SUB
    ids_spec = pl.BlockSpec((1, TB + 2 * PEER_RING, PEER_SLOTS), lambda i: (i, 0, 0))
    table_spec = pl.BlockSpec(memory_space=pltpu.VMEM)
    stages = [pltpu.VMEM((R, 128), jnp.uint32), pltpu.VMEM((R, 128), jnp.uint32),
              pltpu.SMEM((2, PEER_RING, PEER_SLOTS), jnp.int32), pltpu.SemaphoreType.DMA((2,))]
    return ids_spec, table_spec, stages


def peer_scores(ids_flat, ht, tab):
    N = ht.shape[2]
    TB = PEER_TB
    ids_spec, table_spec, stages = _peer_specs(TB)
    return pl.pallas_call(
        _peer_u_kernel,
        grid=(N // TB,),
        in_specs=[ids_spec, pl.BlockSpec((ROW_SUB, 256, TB), lambda i: (0, 0, i)), table_spec],
        out_specs=pl.BlockSpec((PEER_SLOTS, TB), lambda i: (0, i)),
        out_shape=jax.ShapeDtypeStruct((PEER_SLOTS, N), F32),
        scratch_shapes=stages,
        compiler_params=_cp("arbitrary"),
        name="peer_scores",
    )(ids_flat, ht, tab)


def _peer_v_kernel(ids_ref, at_ref, gt_ref, tab_ref, rep_ref, o_ref, stage0, stage1, ring, sems, wexp):
    a = at_ref[...]
    w = gt_ref[...] * (0.5 * a * (1.0 + lax.erf(a * (1.0 / math.sqrt(2.0)))))
    wexp[...] = jnp.dot(w.T, rep_ref[...], precision=HIGHEST, preferred_element_type=F32)
    R = PEER_SLOTS * ROW_SUB
    own = (lax.broadcasted_iota(jnp.int32, (8, R), 0) ==
           lax.broadcasted_iota(jnp.int32, (8, R), 1) % ROW_SUB)

    def consume(t, stage):
        x = _unpack_rows(stage[...])
        wsel = jnp.where(own, wexp[pl.ds(t, 1), :], 0.0).astype(BF16)
        y = jnp.dot(wsel, x, preferred_element_type=F32)
        o_ref[t] = y[:ROW_SUB, :]

    _token_pipeline(ids_ref, tab_ref, (stage0, stage1), ring, sems, consume)


def peer_combine(ids_flat, at, gt, tab, rep):
    N = at.shape[1]
    TB = PEER_TB
    R = PEER_SLOTS * ROW_SUB
    ids_spec, table_spec, stages = _peer_specs(TB)
    slot_tile = pl.BlockSpec((PEER_SLOTS, TB), lambda i: (0, i))
    return pl.pallas_call(
        _peer_v_kernel,
        grid=(N // TB,),
        in_specs=[ids_spec, slot_tile, slot_tile, table_spec, pl.BlockSpec(rep.shape, lambda i: (0, 0))],
        out_specs=pl.BlockSpec((TB, ROW_SUB, 256), lambda i: (i, 0, 0)),
        out_shape=jax.ShapeDtypeStruct((N, ROW_SUB, 256), F32),
        scratch_shapes=stages + [pltpu.VMEM((TB, R), F32)],
        compiler_params=_cp("arbitrary"),
        name="peer_combine",
    )(ids_flat, at, gt, tab, rep)


def _residual_kernel(x_ref, y_ref, g_ref, o_ref):
    o_ref[0] = x_ref[0] + g_ref[0] * y_ref[0]


def gated_residual(x, y, gate):
    B, L, D = x.shape
    T = min(1024, L)
    tile = pl.BlockSpec((1, T, D), lambda b, i: (b, i, 0))
    return pl.pallas_call(
        _residual_kernel,
        grid=(B, L // T),
        in_specs=[tile, tile, pl.BlockSpec((1, 1, D), lambda b, i: (b, 0, 0))],
        out_specs=tile,
        out_shape=jax.ShapeDtypeStruct((B, L, D), F32),
        compiler_params=_cp("parallel", "parallel"),
        name="gated_residual",
    )(x, y, gate)


def _pad_last(a, n):
    return jnp.pad(a, [(0, 0)] * (a.ndim - 1) + [(0, n - a.shape[-1])])


def _pack_table(tab):
    bits = lax.bitcast_convert_type(tab.astype(BF16), jnp.uint16).astype(jnp.uint32)
    bits = bits.reshape(tab.shape[0], ROW_SUB, 2, 128)
    return (bits[:, :, 0, :] | (bits[:, :, 1, :] << 16)).reshape(tab.shape[0] * ROW_SUB, 128)


def _layer_weights(l, norm1_gain, norm2_gain, w_in, pool_w, pool_scale, mla_q_norm, mla_kv_norm, w_uq, w_ukv,
                   q_norm, k_norm, ml_gate_bias, ml_out_norm, w_out, peer_wq, peer_subkeys, peer_u, peer_v):
    parts, start = [], 0
    for size in IN_SIZES:
        parts.append(w_in[l][:, start:start + size])
        start += size
    w_in_r = jnp.concatenate([parts[0], parts[1], parts[2], _pad_last(parts[3], 128), parts[4], parts[5],
                              parts[6], parts[7], _pad_last(parts[8], 128)], axis=1).astype(BF16)
    wbd = jnp.zeros((POOL_WIDTH, POOL_WIDTH), F32)
    for g in range(len(POOL_WINDOWS)):
        sl = slice(g * POOL_GROUP, (g + 1) * POOL_GROUP)
        wbd = wbd.at[sl, sl].set(pool_w[l, g])
    wq = _pad_last(w_uq[l].reshape(MLA_Q_RANK, MLA_HEADS, MLA_QK), HEAD_PAD).reshape(MLA_Q_RANK, -1)
    ukv = w_ukv[l].reshape(MLA_KV_RANK, MLA_HEADS, MLA_NOPE + MLA_V)
    wk = _pad_last(ukv[..., :MLA_NOPE], HEAD_PAD).reshape(MLA_KV_RANK, -1)
    wv = _pad_last(ukv[..., MLA_NOPE:], HEAD_PAD).reshape(MLA_KV_RANK, -1)
    r = jnp.arange(128)[:, None]
    cidx = jnp.arange(MLA_HEADS * HEAD_PAD)[None, :]
    pk = ((r < MLA_ROPE) & (cidx % HEAD_PAD == MLA_NOPE + r)).astype(F32)
    lane = jnp.arange(ML_WIDTH)
    hs = (lane[:, None] // ML_DK == jnp.arange(128)[None, :]).astype(F32)
    wo = w_out[l]
    wa = wo[POOL_WIDTH:POOL_WIDTH + MLA_HEADS * MLA_V]
    return dict(
        n1=norm1_gain[l], n2=norm2_gain[l], w_in=w_in_r,
        wbd=wbd.astype(BF16), pool_scale=pool_scale[l],
        qan=mla_q_norm[l], kvan=mla_kv_norm[l],
        wq=wq.astype(BF16), wk=wk.astype(BF16), wv=wv.astype(BF16), pk=pk,
        qg=_pad_last(q_norm[l], HEAD_PAD).reshape(1, HEAD_PAD), kg=_pad_last(k_norm[l], HEAD_PAD).reshape(1, HEAD_PAD),
        gate_bias=_pad_last(ml_gate_bias[l], 128).reshape(1, 128),
        mlg=ml_out_norm[l].reshape(1, ML_WIDTH), hs=hs, hst=hs.T,
        wp=wo[:POOL_WIDTH].astype(BF16), wa=wa.astype(BF16),
        wm=wo[POOL_WIDTH + MLA_HEADS * MLA_V:].astype(BF16),
        peer_wq=peer_wq[l].astype(BF16),
        sk=peer_subkeys[l].reshape(2 * PEER_HEADS, PEER_KEYS, PEER_KEYS),
        u_tab=_pack_table(peer_u[l]), v_tab=_pack_table(peer_v[l]),
    )


def _peer_constants():
    R = PEER_SLOTS * ROW_SUB
    pairs = _candidate_pairs()
    ncand = -(-len(pairs) // 8) * 8
    k1 = jnp.array([p[0] for p in pairs] + [-1] * (ncand - len(pairs)))[:, None]
    k2 = jnp.array([p[1] for p in pairs] + [-1] * (ncand - len(pairs)))[:, None]
    rank = jnp.arange(PEER_TOPK)[None, :]
    g1 = (k1 == rank).astype(F32)
    g2 = (k2 == rank).astype(F32)
    rep = (jnp.arange(PEER_SLOTS)[:, None] == jnp.arange(R)[None, :] // ROW_SUB).astype(F32)
    return g1, g2, rep


def _rope_tables(n):
    n_rows = n // GRID_W
    row = jnp.repeat(jnp.arange(n_rows), GRID_W, total_repeat_length=n).astype(F32)
    col = (jnp.arange(n) % GRID_W).astype(F32)
    per_axis = MLA_ROPE // 2
    freqs = ROPE_BASE ** (-jnp.arange(0, per_axis, 2, dtype=F32) / per_axis)
    ang = jnp.concatenate([row[:, None] * freqs, col[:, None] * freqs], axis=-1)
    c, s = jnp.cos(ang), jnp.sin(ang)
    cos = jnp.concatenate([jnp.ones((n, MLA_NOPE), F32), c, c, jnp.ones((n, HEAD_PAD - MLA_QK), F32)], axis=-1)
    sin = jnp.concatenate([jnp.zeros((n, MLA_NOPE), F32), -s, s, jnp.zeros((n, HEAD_PAD - MLA_QK), F32)], axis=-1)
    return cos, sin


def _staged_ids(offs):
    blk = offs.reshape(-1, PEER_TB, PEER_SLOTS)
    spare = jnp.broadcast_to(blk[:, :1], (blk.shape[0], PEER_RING + 2, PEER_SLOTS))
    return jnp.concatenate([blk[:, :2], spare[:, :PEER_RING - 2], blk[:, 2:], spare], axis=1)


def _peer_ffn(x, w, shift, scale, gate, consts):
    B, L, D = x.shape
    g1, g2, rep = consts
    qp_t, h_t = peer_query(x, w["n2"], shift, scale, w["peer_wq"])
    ids_t, gates_t = peer_topk(qp_t, w["sk"], g1, g2)
    ids_flat = _staged_ids(ids_t.T * ROW_SUB)
    a_t = peer_scores(ids_flat, h_t.reshape(ROW_SUB, 256, B * L), w["u_tab"])
    y = peer_combine(ids_flat, a_t, gates_t, w["v_tab"], rep)
    return gated_residual(x, y.reshape(B, L, D), gate)


def kernel(x, c, ctx, c_ctx, norm1_gain, norm2_gain, w_ada, b_ada, w_in, pool_w, pool_scale, mla_q_norm,
           mla_kv_norm, w_uq, w_ukv, q_norm, k_norm, ml_gate_bias, ml_out_norm, w_out, peer_wq, peer_subkeys,
           peer_u, peer_v):
    B, S, D = x.shape
    Lc_ctx = ctx.shape[1]
    depth = w_in.shape[0]
    cos_x, sin_x = _rope_tables(S)
    cos_c = jnp.ones((Lc_ctx, HEAD_PAD), F32)
    sin_c = jnp.zeros((Lc_ctx, HEAD_PAD), F32)
    consts = _peer_constants()
    cond8 = jnp.zeros((8, D), F32).at[:B].set(c).at[B].set(c_ctx)
    W = ML_WIDTH
    zero_state = (jnp.zeros((B, 2, W, W), F32), jnp.zeros((B, 2, 1, W), F32),
                  jnp.full((B, 2, 1, W), NEG_INIT, F32))
    x_ctx = ctx
    for l in range(depth):
        last = l == depth - 1
        w = _layer_weights(l, norm1_gain, norm2_gain, w_in, pool_w, pool_scale, mla_q_norm, mla_kv_norm, w_uq,
                           w_ukv, q_norm, k_norm, ml_gate_bias, ml_out_norm, w_out, peer_wq, peer_subkeys,
                           peer_u, peer_v)
        mods = ada_mod(cond8, w_ada[l], b_ada[l])
        m_x = [mods[:B, i * D:(i + 1) * D].reshape(B, 1, D) for i in range(6)]
        m_c = [jnp.broadcast_to(mods[B, i * D:(i + 1) * D].reshape(1, 1, D), (B, 1, D)) for i in range(6)]

        zp_c, zm_c, zl_c = in_proj(x_ctx, w["n1"], m_c[0], m_c[1], w["w_in"])
        qc, kc, vc = mla_qkv(zm_c, w["qan"], w["kvan"], w["wq"], w["wk"], w["wv"], w["pk"], w["qg"], w["kg"],
                             cos_c, sin_c)
        hf_c, hb_c, cT, nT, mT = mlstm(zl_c, w["gate_bias"], zero_state)
        if not last:
            pool_c = pool_mixer(zp_c, w["wbd"], w["pool_scale"])
            attn_c = attention(qc, kc, vc)
            xc = out_proj(x_ctx, pool_c, attn_c, hf_c, hb_c, zl_c, w["mlg"], w["hs"], w["hst"],
                          w["wp"], w["wa"], w["wm"], m_c[2])
            x_ctx_new = _peer_ffn(xc, w, m_c[3], m_c[4], m_c[5], consts)

        zp, zm, zl = in_proj(x, w["n1"], m_x[0], m_x[1], w["w_in"])
        pool_x = pool_mixer(zp, w["wbd"], w["pool_scale"])
        q, k, v = mla_qkv(zm, w["qan"], w["kvan"], w["wq"], w["wk"], w["wv"], w["pk"], w["qg"], w["kg"],
                          cos_x, sin_x)
        attn_x = attention(q, kc, vc, k, v)
        hf, hb, _, _, _ = mlstm(zl, w["gate_bias"], (cT, nT, mT))
        x = out_proj(x, pool_x, attn_x, hf, hb, zl, w["mlg"], w["hs"], w["hst"],
                     w["wp"], w["wa"], w["wm"], m_x[2])
        x = _peer_ffn(x, w, m_x[3], m_x[4], m_x[5], consts)
        if not last:
            x_ctx = x_ctx_new
    return x
```

```python
import functools
import math

import jax
import jax.numpy as jnp
from jax import lax
from jax.experimental import pallas as pl
from jax.experimental.pallas import tpu as pltpu

F32 = jnp.float32
BF16 = jnp.bfloat16
HIGHEST = lax.Precision.HIGHEST

EPS = 1e-6
D_MODEL = 1024
GRID_W = 64
ROPE_BASE = 10000.0
POOL_WINDOWS = (2, 4, 8, 16)
POOL_GROUP = 64
POOL_WIDTH = 256
MLA_HEADS = 8
MLA_NOPE = 64
MLA_ROPE = 32
MLA_QK = 96
MLA_V = 64
MLA_Q_RANK = 384
MLA_KV_RANK = 256
MLA_SCALE = MLA_QK ** -0.5
Q_PRESCALE = MLA_SCALE * math.log2(math.e)
HEAD_PAD = 128
ATT_TQ = 512
ATT_TC = 512
VT_ROWS = 80
ML_HEADS = 4
ML_DK = 64
ML_WIDTH = 256
ML_CHUNK = 128
NEG_INIT = -1e30
IN_SIZES = (256, 384, 256, 32, 256, 256, 256, 256, 16)
PEER_HEADS = 8
PEER_KEYS = 128
PEER_TOPK = 16
PEER_SLOTS = PEER_HEADS * PEER_TOPK
PEER_TB = 128
PEER_RING = 8
ROW_SUB = 4

Z_POOL, Z_MLA, Z_ML = 256, 768, 1152
VMEM_LIMIT = 56 * 1024 * 1024


def _cp(*sem, vmem=VMEM_LIMIT):
    return pltpu.CompilerParams(dimension_semantics=sem, vmem_limit_bytes=vmem)


def _ada_kernel(c_ref, w_ref, b_ref, o_ref):
    c = c_ref[...]
    s = c * (1.0 / (1.0 + jnp.exp(-c)))
    o_ref[...] = jnp.dot(s, w_ref[...], precision=HIGHEST, preferred_element_type=F32) + b_ref[...]


def ada_mod(cond8, w, b):
    n = w.shape[1]
    tn = n // 4
    return pl.pallas_call(
        _ada_kernel,
        grid=(n // tn,),
        in_specs=[pl.BlockSpec((8, D_MODEL), lambda j: (0, 0)),
                  pl.BlockSpec((D_MODEL, tn), lambda j: (0, j)),
                  pl.BlockSpec((1, tn), lambda j: (0, j))],
        out_specs=pl.BlockSpec((8, tn), lambda j: (0, j)),
        out_shape=jax.ShapeDtypeStruct((8, n), F32),
        compiler_params=_cp("arbitrary"),
        name="ada_mod",
    )(cond8, w, b.reshape(1, n))


def _modulated(x, gain, shift, scale):
    ms = jnp.mean(x * x, axis=-1, keepdims=True)
    y = x * lax.rsqrt(ms + EPS) * gain
    return y * (1.0 + scale) + shift


def _inproj_kernel(x_ref, gain_ref, shift_ref, scale_ref, w_ref, zp_ref, zm_ref, zl_ref):
    h = _modulated(x_ref[0], gain_ref[...], shift_ref[0], scale_ref[0])
    res = jnp.dot(h.astype(BF16), w_ref[...], preferred_element_type=F32)
    zp_ref[0] = res[:, :Z_POOL]
    zm_ref[0] = res[:, Z_POOL:Z_POOL + Z_MLA]
    zl_ref[0] = res[:, Z_POOL + Z_MLA:]


def in_proj(x, gain, shift, scale, w):
    B, L, D = x.shape
    tm = min(512, L)
    n = w.shape[1]
    vec = pl.BlockSpec((1, 1, D), lambda b, i: (b, 0, 0))
    return pl.pallas_call(
        _inproj_kernel,
        grid=(B, L // tm),
        in_specs=[pl.BlockSpec((1, tm, D), lambda b, i: (b, i, 0)),
                  pl.BlockSpec((1, D), lambda b, i: (0, 0)),
                  vec, vec,
                  pl.BlockSpec((D, n), lambda b, i: (0, 0))],
        out_specs=[pl.BlockSpec((1, tm, Z_POOL), lambda b, i: (b, i, 0)),
                   pl.BlockSpec((1, tm, Z_MLA), lambda b, i: (b, i, 0)),
                   pl.BlockSpec((1, tm, Z_ML), lambda b, i: (b, i, 0))],
        out_shape=[jax.ShapeDtypeStruct((B, L, Z_POOL), F32),
                   jax.ShapeDtypeStruct((B, L, Z_MLA), F32),
                   jax.ShapeDtypeStruct((B, L, Z_ML), F32)],
        compiler_params=_cp("parallel", "parallel"),
        name="in_proj",
    )(x, gain.reshape(1, D), shift, scale, w)


def _peerq_kernel(x_ref, gain_ref, shift_ref, scale_ref, w_ref, qp_ref, h_ref):
    h = _modulated(x_ref[0], gain_ref[...], shift_ref[0], scale_ref[0])
    hb = h.astype(BF16)
    h_ref[...] = hb.astype(F32)
    res = jnp.dot(hb, w_ref[...], preferred_element_type=F32)
    for g in range(2 * PEER_HEADS):
        qp_ref[g] = res[:, g * PEER_KEYS:(g + 1) * PEER_KEYS].T


def peer_query(x, gain, shift, scale, w):
    B, L, D = x.shape
    tm = min(512, L)
    nb = L // tm
    n = w.shape[1]
    vec = pl.BlockSpec((1, 1, D), lambda b, i: (b, 0, 0))
    return pl.pallas_call(
        _peerq_kernel,
        grid=(B, nb),
        in_specs=[pl.BlockSpec((1, tm, D), lambda b, i: (b, i, 0)),
                  pl.BlockSpec((1, D), lambda b, i: (0, 0)),
                  vec, vec,
                  pl.BlockSpec((D, n), lambda b, i: (0, 0))],
        out_specs=[pl.BlockSpec((2 * PEER_HEADS, PEER_KEYS, tm), lambda b, i: (0, 0, b * nb + i)),
                   pl.BlockSpec((tm, D), lambda b, i: (b * nb + i, 0))],
        out_shape=[jax.ShapeDtypeStruct((2 * PEER_HEADS, PEER_KEYS, B * L), F32),
                   jax.ShapeDtypeStruct((B * L, D), F32)],
        compiler_params=_cp("parallel", "parallel"),
        name="peer_query",
    )(x, gain.reshape(1, D), shift, scale, w)


def _pool_kernel(p_ref, c_ref, n_ref, wbd_ref, sc_ref, o_ref, *, L, T):
    i = pl.program_id(1)
    cur = c_ref[0]
    u3 = jnp.concatenate([p_ref[0], cur, n_ref[0]], axis=0).astype(BF16)
    t = i * T + lax.broadcasted_iota(jnp.int32, (T, 3 * T), 0)
    s = (i - 1) * T + lax.broadcasted_iota(jnp.int32, (T, 3 * T), 1)
    lane = lax.broadcasted_iota(jnp.int32, (T, POOL_WIDTH), 1)
    trow = i * T + lax.broadcasted_iota(jnp.int32, (T, POOL_WIDTH), 0)
    win = jnp.zeros((T, POOL_WIDTH), F32)
    for g, w in enumerate(POOL_WINDOWS):
        lo = jnp.maximum(t - w // 2, 0)
        hi = jnp.minimum(t + w // 2, L)
        band = jnp.where((s >= lo) & (s < hi), 1.0, 0.0).astype(BF16)
        ws = jnp.dot(band, u3, preferred_element_type=F32)
        cnt = (jnp.minimum(trow + w // 2, L) - jnp.maximum(trow - w // 2, 0)).astype(F32)
        in_group = (lane >= g * POOL_GROUP) & (lane < (g + 1) * POOL_GROUP)
        win = jnp.where(in_group, ws / cnt, win)
    d = win - cur
    y = jnp.dot(d.astype(BF16), wbd_ref[...], preferred_element_type=F32)
    o_ref[0] = y * sc_ref[...]


def pool_mixer(z_pool, wbd, scale):
    B, L, C = z_pool.shape
    T = 256
    nb = L // T
    return pl.pallas_call(
        functools.partial(_pool_kernel, L=L, T=T),
        grid=(B, nb),
        in_specs=[pl.BlockSpec((1, T, C), lambda b, i: (b, jnp.maximum(i - 1, 0), 0)),
                  pl.BlockSpec((1, T, C), lambda b, i: (b, i, 0)),
                  pl.BlockSpec((1, T, C), lambda b, i: (b, jnp.minimum(i + 1, nb - 1), 0)),
                  pl.BlockSpec((C, C), lambda b, i: (0, 0)),
                  pl.BlockSpec((1, C), lambda b, i: (0, 0))],
        out_specs=pl.BlockSpec((1, T, C), lambda b, i: (b, i, 0)),
        out_shape=jax.ShapeDtypeStruct((B, L, C), F32),
        compiler_params=_cp("parallel", "parallel"),
        name="pool_mixer",
    )(z_pool, z_pool, z_pool, wbd, scale.reshape(1, C))


def _rms(x, gain, n):
    ss = jnp.sum(x * x, axis=-1, keepdims=True) * (1.0 / n)
    return x * lax.rsqrt(ss + EPS) * gain


def _mla_kernel(z_ref, qan_ref, kvan_ref, wq_ref, wk_ref, wv_ref, pk_ref, qg_ref, kg_ref,
                cos_ref, sin_ref, q_out, k_out, v_out):
    z = z_ref[0]
    zq = z[:, :MLA_Q_RANK]
    zkv = z[:, MLA_Q_RANK:MLA_Q_RANK + MLA_KV_RANK]
    zkr = z[:, MLA_Q_RANK + MLA_KV_RANK:]
    nq = _rms(zq, qan_ref[...], MLA_Q_RANK).astype(BF16)
    nkv = _rms(zkv, kvan_ref[...], MLA_KV_RANK).astype(BF16)
    qp = jnp.dot(nq, wq_ref[...], preferred_element_type=F32)
    kp = jnp.dot(nkv, wk_ref[...], preferred_element_type=F32)
    kp = kp + jnp.dot(zkr, pk_ref[...], precision=HIGHEST, preferred_element_type=F32)
    vp = jnp.dot(nkv, wv_ref[...], preferred_element_type=F32)
    cos = cos_ref[...]
    sin = sin_ref[...]
    lane = lax.broadcasted_iota(jnp.int32, cos.shape, 1)
    first_half = lane < MLA_NOPE + MLA_ROPE // 2
    extra = (VT_ROWS - MLA_V, z.shape[0])
    ones_rows = jnp.where(lax.broadcasted_iota(jnp.int32, extra, 0) == 0, 1.0, 0.0)
    for h in range(MLA_HEADS):
        sl = slice(h * HEAD_PAD, (h + 1) * HEAD_PAD)
        for src, gain_ref, out, mult in ((qp, qg_ref, q_out, Q_PRESCALE), (kp, kg_ref, k_out, 1.0)):
            xn = _rms(src[:, sl], gain_ref[...], MLA_QK)
            partner = jnp.where(first_half, pltpu.roll(xn, HEAD_PAD - MLA_ROPE // 2, 1),
                                pltpu.roll(xn, MLA_ROPE // 2, 1))
            xr = xn * cos + partner * sin
            out[0, h] = (xr * mult).astype(BF16)
        v_out[0, h] = jnp.concatenate([vp[:, sl].T[:MLA_V, :], ones_rows], axis=0).astype(BF16)


def mla_qkv(z_mla, qan, kvan, wq, wk, wv, pk, qg, kg, cos, sin):
    B, L, _ = z_mla.shape
    T = 256
    full = lambda shape: pl.BlockSpec(shape, lambda b, i: (0,) * len(shape))
    head_out = pl.BlockSpec((1, MLA_HEADS, T, HEAD_PAD), lambda b, i: (b, 0, i, 0))
    out_sds = jax.ShapeDtypeStruct((B, MLA_HEADS, L, HEAD_PAD), BF16)
    vt_out = pl.BlockSpec((1, MLA_HEADS, VT_ROWS, T), lambda b, i: (b, 0, 0, i))
    vt_sds = jax.ShapeDtypeStruct((B, MLA_HEADS, VT_ROWS, L), BF16)
    return pl.pallas_call(
        _mla_kernel,
        grid=(B, L // T),
        in_specs=[pl.BlockSpec((1, T, Z_MLA), lambda b, i: (b, i, 0)),
                  full((1, MLA_Q_RANK)), full((1, MLA_KV_RANK)),
                  full(wq.shape), full(wk.shape), full(wv.shape), full(pk.shape),
                  full((1, HEAD_PAD)), full((1, HEAD_PAD)),
                  pl.BlockSpec((T, HEAD_PAD), lambda b, i: (i, 0)),
                  pl.BlockSpec((T, HEAD_PAD), lambda b, i: (i, 0))],
        out_specs=[head_out, head_out, vt_out],
        out_shape=[out_sds, out_sds, vt_sds],
        compiler_params=_cp("parallel", "parallel"),
        name="mla_qkv",
    )(z_mla, qan.reshape(1, -1), kvan.reshape(1, -1), wq, wk, wv, pk, qg, kg, cos, sin)


def _flash_kernel(q_ref, kc_ref, vct_ref, *rest, nchunks):
    if nchunks:
        k_ref, vt_ref, o_ref, s_s, m_s, acc_s = rest
    else:
        o_ref, s_s, m_s, acc_s = rest
    nc = kc_ref.shape[2]

    def scores(h, kblk):
        n = kblk.shape[0]
        s_s[h, :n] = lax.dot_general(kblk, q_ref[0, h], (((1,), (1,)), ((), ())), preferred_element_type=F32)

    def update(h, n, vtblk):
        st = s_s[h, :n]
        m = m_s[h]
        m_new = jnp.maximum(m, jnp.max(st, axis=0, keepdims=True))
        p = jnp.exp2(st - m_new).astype(BF16)
        m_s[h] = m_new
        acc_s[h] = jnp.exp2(m - m_new) * acc_s[h] + jnp.dot(vtblk, p, preferred_element_type=F32)

    m_s[...] = jnp.full(m_s.shape, -jnp.inf, F32)
    acc_s[...] = jnp.zeros(acc_s.shape, F32)
    scores(0, kc_ref[0, 0])
    scores(1, kc_ref[0, 1])
    update(0, nc, vct_ref[0, 0])
    if nchunks:
        scores(0, k_ref[0, 0, pl.ds(0, ATT_TC), :])
    update(1, nc, vct_ref[0, 1])
    if nchunks:
        def body(c, _):
            off = pl.multiple_of(c * ATT_TC, ATT_TC)
            nxt = pl.multiple_of(jnp.minimum(c + 1, nchunks - 1) * ATT_TC, ATT_TC)
            scores(1, k_ref[0, 1, pl.ds(off, ATT_TC), :])
            update(0, ATT_TC, vt_ref[0, 0, :, pl.ds(off, ATT_TC)])
            scores(0, k_ref[0, 0, pl.ds(nxt, ATT_TC), :])
            update(1, ATT_TC, vt_ref[0, 1, :, pl.ds(off, ATT_TC)])
            return 0
        lax.fori_loop(0, nchunks, body, 0, unroll=8)
    o = jnp.concatenate([acc_s[h, :MLA_V] / acc_s[h, MLA_V:MLA_V + 1] for h in range(2)], axis=0)
    o_ref[0] = o.T.astype(BF16)


def attention(q, kc, vct, k=None, vt=None):
    B, H, Lq, _ = q.shape
    Lc = kc.shape[2]
    tq = min(ATT_TQ, Lq)
    pair4 = lambda n, d: pl.BlockSpec((1, 2, n, d), lambda b, h, i: (b, h, 0, 0))
    in_specs = [pl.BlockSpec((1, 2, tq, HEAD_PAD), lambda b, h, i: (b, h, i, 0)),
                pair4(Lc, HEAD_PAD), pair4(VT_ROWS, Lc)]
    args = [q, kc, vct]
    nchunks = 0
    if k is not None:
        Lk = k.shape[2]
        nchunks = Lk // ATT_TC
        in_specs += [pair4(Lk, HEAD_PAD), pair4(VT_ROWS, Lk)]
        args += [k, vt]
    return pl.pallas_call(
        functools.partial(_flash_kernel, nchunks=nchunks),
        grid=(B, H // 2, Lq // tq),
        in_specs=in_specs,
        out_specs=pl.BlockSpec((1, tq, 2 * MLA_V), lambda b, h, i: (b, i, h)),
        out_shape=jax.ShapeDtypeStruct((B, Lq, H * MLA_V), BF16),
        scratch_shapes=[pltpu.VMEM((2, max(ATT_TC, Lc), tq), F32), pltpu.VMEM((2, 1, tq), F32),
                        pltpu.VMEM((2, VT_ROWS, tq), F32)],
        compiler_params=_cp("parallel", "parallel", "arbitrary"),
        name="attention",
    )(*args)


def _log_sigmoid(x):
    return jnp.minimum(x, 0.0) - jnp.log(1.0 + jnp.exp(-jnp.abs(x)))


def _mlstm_direction(d, q, k, v, g, C_s, n_s, m_s):
    Lc = q.shape[0]
    row = lax.broadcasted_iota(jnp.int32, (Lc, Lc), 0)
    col = lax.broadcasted_iota(jnp.int32, (Lc, Lc), 1)
    tri = (col <= row) if d == 0 else (col >= row)
    logf = _log_sigmoid(g)
    bcol = jnp.dot(jnp.where(tri, 1.0, 0.0), logf, precision=HIGHEST, preferred_element_type=F32)
    bT = bcol.T
    gT = g.T
    lane = lax.broadcasted_iota(jnp.int32, (1, ML_WIDTH), 1)
    kb = k.astype(BF16)
    vb = v.astype(BF16)
    Cst = C_s[d]
    nst = n_s[d]
    mst = m_s[d]
    qc = jnp.dot(q.astype(BF16), Cst.astype(BF16), preferred_element_type=F32)
    out = jnp.zeros((Lc, ML_WIDTH), F32)
    ws_all = jnp.zeros((Lc, ML_WIDTH), F32)
    wprev_all = jnp.zeros((1, ML_WIDTH), F32)
    mnew_all = jnp.zeros((1, ML_WIDTH), F32)
    for h in range(ML_HEADS):
        il = 8 * d + h
        fl = 8 * d + 4 + h
        head = (lane >= h * ML_DK) & (lane < (h + 1) * ML_DK)
        bc = bcol[:, fl:fl + 1]
        br = bT[fl:fl + 1, :]
        ir = gT[il:il + 1, :]
        ic = g[:, il:il + 1]
        mprev = mst[:, h * ML_DK:h * ML_DK + 1]
        dmat = jnp.where(tri, bc - br + ir, -jnp.inf)
        inter = bc + mprev
        mj = jnp.maximum(inter, jnp.max(dmat, axis=-1, keepdims=True))
        w_inter = jnp.exp(inter - mj)
        qh = jnp.where(head, q, 0.0)
        s = lax.dot_general(qh.astype(BF16), kb, (((1,), (1,)), ((), ())), preferred_element_type=F32)
        qk = s * jnp.exp(dmat - mj)
        pv = jnp.dot(qk.astype(BF16), vb, preferred_element_type=F32)
        qn = jnp.sum(qh * nst, axis=-1, keepdims=True)
        den = jnp.sum(qk, axis=-1, keepdims=True) + w_inter * qn
        denom = jnp.maximum(jnp.abs(den), jnp.exp(-mj))
        out = jnp.where(head, (pv + qc * w_inter) / denom, out)
        blast = bc[Lc - 1:Lc, :] if d == 0 else bc[0:1, :]
        dec = blast - bc + ic
        mnew = jnp.maximum(blast + mprev, jnp.max(dec, axis=0, keepdims=True))
        wprev = jnp.exp(blast + mprev - mnew)
        ws = jnp.exp(dec - mnew)
        ws_all = jnp.where(head, ws, ws_all)
        wprev_all = jnp.where(head, wprev, wprev_all)
        mnew_all = jnp.where(head, mnew, mnew_all)
    kw = k * ws_all
    upd = jnp.dot(kw.T.astype(BF16), vb, preferred_element_type=F32)
    r2 = lax.broadcasted_iota(jnp.int32, (ML_WIDTH, ML_WIDTH), 0) // ML_DK
    c2 = lax.broadcasted_iota(jnp.int32, (ML_WIDTH, ML_WIDTH), 1) // ML_DK
    C_s[d] = Cst * wprev_all + jnp.where(r2 == c2, upd, 0.0)
    n_s[d] = nst * wprev_all + jnp.sum(kw, axis=0, keepdims=True)
    m_s[d] = mnew_all
    return out


def _mlstm_kernel(qf_ref, kf_ref, vf_ref, gf_ref, qb_ref, kb_ref, vb_ref, gb_ref, bias_ref,
                  c0_ref, n0_ref, m0_ref, hf_ref, hb_ref, cT_ref, nT_ref, mT_ref, C_s, n_s, m_s):
    c = pl.program_id(1)

    @pl.when(c == 0)
    def _():
        C_s[...] = c0_ref[0]
        n_s[...] = n0_ref[0]
        m_s[...] = m0_ref[0]

    scale = ML_DK ** -0.5
    hf_ref[0] = _mlstm_direction(0, qf_ref[0] * scale, kf_ref[0], vf_ref[0],
                                 gf_ref[0] + bias_ref[...], C_s, n_s, m_s)
    hb_ref[0] = _mlstm_direction(1, qb_ref[0] * scale, kb_ref[0], vb_ref[0],
                                 gb_ref[0] + bias_ref[...], C_s, n_s, m_s)

    @pl.when(c == pl.num_programs(1) - 1)
    def _():
        cT_ref[0] = C_s[...]
        nT_ref[0] = n_s[...]
        mT_ref[0] = m_s[...]


def mlstm(z_ml, bias, state):
    B, L, _ = z_ml.shape
    Lc = ML_CHUNK
    nc = L // Lc
    c0, n0, m0 = state
    W = ML_WIDTH
    fwd = lambda j: pl.BlockSpec((1, Lc, W), lambda b, c: (b, c, j))
    bwd = lambda j: pl.BlockSpec((1, Lc, W), lambda b, c: (b, nc - 1 - c, j))
    gcol = 4 * W // 128
    st_c = pl.BlockSpec((1, 2, W, W), lambda b, c: (b, 0, 0, 0))
    st_v = pl.BlockSpec((1, 2, 1, W), lambda b, c: (b, 0, 0, 0))
    return pl.pallas_call(
        _mlstm_kernel,
        grid=(B, nc),
        in_specs=[fwd(0), fwd(1), fwd(2), pl.BlockSpec((1, Lc, 128), lambda b, c: (b, c, gcol)),
                  bwd(0), bwd(1), bwd(2), pl.BlockSpec((1, Lc, 128), lambda b, c: (b, nc - 1 - c, gcol)),
                  pl.BlockSpec((1, 128), lambda b, c: (0, 0)),
                  st_c, st_v, st_v],
        out_specs=[pl.BlockSpec((1, Lc, W), lambda b, c: (b, c, 0)),
                   pl.BlockSpec((1, Lc, W), lambda b, c: (b, nc - 1 - c, 0)),
                   st_c, st_v, st_v],
        out_shape=[jax.ShapeDtypeStruct((B, L, W), F32), jax.ShapeDtypeStruct((B, L, W), F32),
                   jax.ShapeDtypeStruct((B, 2, W, W), F32), jax.ShapeDtypeStruct((B, 2, 1, W), F32),
                   jax.ShapeDtypeStruct((B, 2, 1, W), F32)],
        scratch_shapes=[pltpu.VMEM((2, W, W), F32), pltpu.VMEM((2, 1, W), F32), pltpu.VMEM((2, 1, W), F32)],
        compiler_params=_cp("parallel", "arbitrary"),
        name="mlstm",
    )(z_ml, z_ml, z_ml, z_ml, z_ml, z_ml, z_ml, z_ml, bias, c0, n0, m0)


def _outproj_kernel(x_ref, pool_ref, attn_ref, hf_ref, hb_ref, op_ref, mlg_ref, hs_ref, hst_ref,
                    wp_ref, wa_ref, wm_ref, ga_ref, o_ref):
    h = hf_ref[0] + hb_ref[0]
    ss = jnp.dot(h * h, hs_ref[...], precision=HIGHEST, preferred_element_type=F32) * (1.0 / ML_DK)
    inv = jnp.dot(lax.rsqrt(ss + EPS), hst_ref[...], precision=HIGHEST, preferred_element_type=F32)
    op = op_ref[0]
    ml = h * inv * mlg_ref[...] * (1.0 / (1.0 + jnp.exp(-op)))
    mix = jnp.dot(pool_ref[0].astype(BF16), wp_ref[...], preferred_element_type=F32)
    mix += jnp.dot(attn_ref[0], wa_ref[...], preferred_element_type=F32)
    mix += jnp.dot(ml.astype(BF16), wm_ref[...], preferred_element_type=F32)
    o_ref[0] = x_ref[0] + ga_ref[0] * mix


def out_proj(x, pool, attn, hf, hb, z_ml, mlg, hs, hst, wp, wa, wm, gate):
    B, L, D = x.shape
    T = min(512, L)
    W = ML_WIDTH
    tile = lambda w, j=0: pl.BlockSpec((1, T, w), lambda b, i: (b, i, j))
    full = lambda shape: pl.BlockSpec(shape, lambda b, i: (0,) * len(shape))
    return pl.pallas_call(
        _outproj_kernel,
        grid=(B, L // T),
        in_specs=[tile(D), tile(W), tile(MLA_HEADS * MLA_V), tile(W), tile(W), tile(W, 3),
                  full((1, W)), full(hs.shape), full(hst.shape),
                  full(wp.shape), full(wa.shape), full(wm.shape),
                  pl.BlockSpec((1, 1, D), lambda b, i: (b, 0, 0))],
        out_specs=tile(D),
        out_shape=jax.ShapeDtypeStruct((B, L, D), F32),
        compiler_params=_cp("parallel", "parallel"),
        name="out_proj",
    )(x, pool, attn, hf, hb, z_ml, mlg, hs, hst, wp, wa, wm, gate)


def _top16_rows(s, row):
    big = float(s.shape[0])
    vals, idxs = [], []
    for _ in range(PEER_TOPK):
        m = jnp.max(s, axis=0, keepdims=True)
        idx = jnp.min(jnp.where(s == m, row, big), axis=0, keepdims=True)
        vals.append(m)
        idxs.append(idx)
        s = jnp.where(row == idx, -jnp.inf, s)
    return jnp.concatenate(vals, axis=0), jnp.concatenate(idxs, axis=0)


def _candidate_pairs():
    return [(k1, k2) for k1 in range(PEER_TOPK) for k2 in range(PEER_TOPK) if (k1 + 1) * (k2 + 1) <= PEER_TOPK]


def _topk_kernel(qp_ref, sk_ref, g1_ref, g2_ref, ids_ref, gate_ref):
    T = 256
    row = lax.broadcasted_iota(jnp.int32, (PEER_KEYS, T), 0).astype(F32)
    pick = lambda g_ref, a: jnp.dot(g_ref[...], a, precision=HIGHEST, preferred_element_type=F32)
    ncand = g1_ref.shape[0]
    crow = lax.broadcasted_iota(jnp.int32, (ncand, T), 0).astype(F32)

    def tile(lt, _):
        cols = pl.ds(pl.multiple_of(lt * T, T), T)
        tops = []
        for c in range(2):
            s = jnp.dot(sk_ref[c], qp_ref[c, :, cols], precision=HIGHEST, preferred_element_type=F32)
            tops.append(_top16_rows(s, row))
        cand = pick(g1_ref, tops[0][0]) + pick(g2_ref, tops[1][0])
        cand = jnp.where(crow < float(len(_candidate_pairs())), cand, -jnp.inf)
        expert = pick(g1_ref, tops[0][1]) * float(PEER_KEYS) + pick(g2_ref, tops[1][1])
        best, eids = [], []
        for _ in range(PEER_TOPK):
            m = jnp.max(cand, axis=0, keepdims=True)
            ci = jnp.min(jnp.where(cand == m, crow, float(ncand)), axis=0, keepdims=True)
            sel = crow == ci
            best.append(m)
            eids.append(jnp.sum(jnp.where(sel, expert, 0.0), axis=0, keepdims=True))
            cand = jnp.where(sel, -jnp.inf, cand)
        best = jnp.concatenate(best, axis=0)
        p = jnp.exp(best - best[0:1, :])
        gate_ref[:, cols] = p / jnp.sum(p, axis=0, keepdims=True)
        ids_ref[:, cols] = jnp.concatenate(eids, axis=0).astype(jnp.int32)
        return 0

    lax.fori_loop(0, qp_ref.shape[2] // T, tile, 0)


def peer_topk(qp_t, sk, g1, g2):
    N = qp_t.shape[2]
    T = 512
    return pl.pallas_call(
        _topk_kernel,
        grid=(N // T, PEER_HEADS),
        in_specs=[pl.BlockSpec((2, PEER_KEYS, T), lambda i, h: (h, 0, i)),
                  pl.BlockSpec((2, PEER_KEYS, PEER_KEYS), lambda i, h: (h, 0, 0)),
                  pl.BlockSpec(g1.shape, lambda i, h: (0, 0)),
                  pl.BlockSpec(g2.shape, lambda i, h: (0, 0))],
        out_specs=[pl.BlockSpec((PEER_TOPK, T), lambda i, h: (h, i)),
                   pl.BlockSpec((PEER_TOPK, T), lambda i, h: (h, i))],
        out_shape=[jax.ShapeDtypeStruct((PEER_SLOTS, N), jnp.int32),
                   jax.ShapeDtypeStruct((PEER_SLOTS, N), F32)],
        compiler_params=_cp("parallel", "parallel"),
        name="peer_topk",
    )(qp_t, sk, g1, g2)


def _unpack_rows(x):
    lo = pltpu.bitcast(x << 16, F32)
    hi = pltpu.bitcast(x & jnp.uint32(0xFFFF0000), F32)
    return jnp.concatenate([lo, hi], axis=-1).astype(BF16)


def _token_pipeline(ids_ref, tab_ref, ring, sems, consume, finish_group=None):
    TB, R = PEER_TB, PEER_RING

    def group_copy(k, half):
        first = pl.multiple_of(k * R, R)
        return pltpu.make_async_copy(ids_ref.at[0, pl.ds(first, R)], ring.at[half], sems.at[half])

    group_copy(0, 0).start()

    def body(i, _):
        for half in range(2):
            k = 2 * i + half
            group_copy(k, half).wait()
            group_copy(k + 1, 1 - half).start()
            carry = None
            for s in range(R):
                rows = [tab_ref[pl.ds(ring[half, s, j], ROW_SUB), :] for j in range(PEER_SLOTS)]
                carry = consume(k * R + s, jnp.concatenate(rows, axis=0), carry)
            if finish_group is not None:
                finish_group(carry)
        return 0

    lax.fori_loop(0, TB // (2 * R), body, 0)
    group_copy(TB // R, 0).wait()


def _peer_u_kernel(ids_ref, h3_ref, tab_ref, group_ref, at_ref, ring, sems, acc_s):
    TB = at_ref.shape[1]
    lane_t = lax.broadcasted_iota(jnp.int32, (2 * 128, TB), 1)
    acc_s[...] = jnp.zeros(acc_s.shape, F32)

    def consume(t, rows, acc):
        h = h3_ref[t]
        lo = pltpu.bitcast(rows << 16, F32).reshape(-1, 8, 128)
        hi = pltpu.bitcast(rows & jnp.uint32(0xFFFF0000), F32).reshape(-1, 8, 128)
        h_lo = jnp.concatenate([h[:, :128], h[:, :128]], axis=0)
        h_hi = jnp.concatenate([h[:, 128:], h[:, 128:]], axis=0)
        prod = jnp.concatenate([(lo * h_lo[None]).reshape(-1, 128), (hi * h_hi[None]).reshape(-1, 128)],
                               axis=-1).astype(BF16)
        column = jnp.where(lane_t == t, 1.0, 0.0).astype(BF16)
        half = prod.shape[0] // 2
        parts = [jnp.dot(prod[:half], column, preferred_element_type=F32),
                 jnp.dot(prod[half:], column, preferred_element_type=F32)]
        return parts if acc is None else [a + p for a, p in zip(acc, parts)]

    def finish_group(acc):
        half = acc_s.shape[0] // 2
        acc_s[:half] += acc[0]
        acc_s[half:] += acc[1]

    _token_pipeline(ids_ref, tab_ref, ring, sems, consume, finish_group)
    at_ref[...] = jnp.dot(group_ref[...], acc_s[...], precision=HIGHEST, preferred_element_type=F32)


def _peer_specs(TB):
    R = PEER_SLOTS * ROW_SUB
    ids_spec = pl.BlockSpec((1, TB + PEER_RING, PEER_SLOTS), lambda i: (i, 0, 0))
    table_spec = pl.BlockSpec(memory_space=pltpu.VMEM)
    ring = [pltpu.SMEM((2, PEER_RING, PEER_SLOTS), jnp.int32), pltpu.SemaphoreType.DMA((2,))]
    return ids_spec, table_spec, ring


def peer_scores(ids_blocks, h3, tab, group):
    N = h3.shape[0]
    TB = PEER_TB
    R = PEER_SLOTS * ROW_SUB
    ids_spec, table_spec, ring = _peer_specs(TB)
    return pl.pallas_call(
        _peer_u_kernel,
        grid=(N // TB,),
        in_specs=[ids_spec, pl.BlockSpec((TB, ROW_SUB, 256), lambda i: (i, 0, 0)), table_spec,
                  pl.BlockSpec(group.shape, lambda i: (0, 0))],
        out_specs=pl.BlockSpec((PEER_SLOTS, TB), lambda i: (0, i)),
        out_shape=jax.ShapeDtypeStruct((PEER_SLOTS, N), F32),
        scratch_shapes=ring + [pltpu.VMEM((R, TB), F32)],
        compiler_params=_cp("arbitrary"),
        name="peer_scores",
    )(ids_blocks, h3, tab, group)


def _peer_v_kernel(ids_ref, at_ref, gt_ref, tab_ref, rep_ref, o_ref, ring, sems, wexp):
    a = at_ref[...]
    w = gt_ref[...] * (0.5 * a * (1.0 + lax.erf(a * (1.0 / math.sqrt(2.0)))))
    wexp[...] = jnp.dot(w.T, rep_ref[...], precision=HIGHEST, preferred_element_type=F32)
    R = PEER_SLOTS * ROW_SUB
    own = (lax.broadcasted_iota(jnp.int32, (8, R), 0) ==
           lax.broadcasted_iota(jnp.int32, (8, R), 1) % ROW_SUB)

    def consume(t, rows, carry):
        x = _unpack_rows(rows)
        wsel = jnp.where(own, wexp[pl.ds(t, 1), :], 0.0).astype(BF16)
        y = jnp.dot(wsel, x, preferred_element_type=F32)
        o_ref[t] = y[:ROW_SUB, :]
        return carry

    _token_pipeline(ids_ref, tab_ref, ring, sems, consume)


def peer_combine(ids_flat, at, gt, tab, rep):
    N = at.shape[1]
    TB = PEER_TB
    R = PEER_SLOTS * ROW_SUB
    ids_spec, table_spec, ring = _peer_specs(TB)
    slot_tile = pl.BlockSpec((PEER_SLOTS, TB), lambda i: (0, i))
    return pl.pallas_call(
        _peer_v_kernel,
        grid=(N // TB,),
        in_specs=[ids_spec, slot_tile, slot_tile, table_spec, pl.BlockSpec(rep.shape, lambda i: (0, 0))],
        out_specs=pl.BlockSpec((TB, ROW_SUB, 256), lambda i: (i, 0, 0)),
        out_shape=jax.ShapeDtypeStruct((N, ROW_SUB, 256), F32),
        scratch_shapes=ring + [pltpu.VMEM((TB, R), F32)],
        compiler_params=_cp("arbitrary"),
        name="peer_combine",
    )(ids_flat, at, gt, tab, rep)


def _residual_kernel(x_ref, y_ref, g_ref, o_ref):
    o_ref[0] = x_ref[0] + g_ref[0] * y_ref[0]


def gated_residual(x, y, gate):
    B, L, D = x.shape
    T = min(1024, L)
    tile = pl.BlockSpec((1, T, D), lambda b, i: (b, i, 0))
    return pl.pallas_call(
        _residual_kernel,
        grid=(B, L // T),
        in_specs=[tile, tile, pl.BlockSpec((1, 1, D), lambda b, i: (b, 0, 0))],
        out_specs=tile,
        out_shape=jax.ShapeDtypeStruct((B, L, D), F32),
        compiler_params=_cp("parallel", "parallel"),
        name="gated_residual",
    )(x, y, gate)


def _pad_last(a, n):
    return jnp.pad(a, [(0, 0)] * (a.ndim - 1) + [(0, n - a.shape[-1])])


def _pack_table(tab):
    bits = lax.bitcast_convert_type(tab.astype(BF16), jnp.uint16).astype(jnp.uint32)
    bits = bits.reshape(tab.shape[0], ROW_SUB, 2, 128)
    return (bits[:, :, 0, :] | (bits[:, :, 1, :] << 16)).reshape(tab.shape[0] * ROW_SUB, 128)


def _layer_weights(l, norm1_gain, norm2_gain, w_in, pool_w, pool_scale, mla_q_norm, mla_kv_norm, w_uq, w_ukv,
                   q_norm, k_norm, ml_gate_bias, ml_out_norm, w_out, peer_wq, peer_subkeys, peer_u, peer_v):
    parts, start = [], 0
    for size in IN_SIZES:
        parts.append(w_in[l][:, start:start + size])
        start += size
    w_in_r = jnp.concatenate([parts[0], parts[1], parts[2], _pad_last(parts[3], 128), parts[4], parts[5],
                              parts[6], parts[7], _pad_last(parts[8], 128)], axis=1).astype(BF16)
    wbd = jnp.zeros((POOL_WIDTH, POOL_WIDTH), F32)
    for g in range(len(POOL_WINDOWS)):
        sl = slice(g * POOL_GROUP, (g + 1) * POOL_GROUP)
        wbd = wbd.at[sl, sl].set(pool_w[l, g])
    wq = _pad_last(w_uq[l].reshape(MLA_Q_RANK, MLA_HEADS, MLA_QK), HEAD_PAD).reshape(MLA_Q_RANK, -1)
    ukv = w_ukv[l].reshape(MLA_KV_RANK, MLA_HEADS, MLA_NOPE + MLA_V)
    wk = _pad_last(ukv[..., :MLA_NOPE], HEAD_PAD).reshape(MLA_KV_RANK, -1)
    wv = _pad_last(ukv[..., MLA_NOPE:], HEAD_PAD).reshape(MLA_KV_RANK, -1)
    r = jnp.arange(128)[:, None]
    cidx = jnp.arange(MLA_HEADS * HEAD_PAD)[None, :]
    pk = ((r < MLA_ROPE) & (cidx % HEAD_PAD == MLA_NOPE + r)).astype(F32)
    lane = jnp.arange(ML_WIDTH)
    hs = (lane[:, None] // ML_DK == jnp.arange(128)[None, :]).astype(F32)
    wo = w_out[l]
    wa = wo[POOL_WIDTH:POOL_WIDTH + MLA_HEADS * MLA_V]
    return dict(
        n1=norm1_gain[l], n2=norm2_gain[l], w_in=w_in_r,
        wbd=wbd.astype(BF16), pool_scale=pool_scale[l],
        qan=mla_q_norm[l], kvan=mla_kv_norm[l],
        wq=wq.astype(BF16), wk=wk.astype(BF16), wv=wv.astype(BF16), pk=pk,
        qg=_pad_last(q_norm[l], HEAD_PAD).reshape(1, HEAD_PAD), kg=_pad_last(k_norm[l], HEAD_PAD).reshape(1, HEAD_PAD),
        gate_bias=_pad_last(ml_gate_bias[l], 128).reshape(1, 128),
        mlg=ml_out_norm[l].reshape(1, ML_WIDTH), hs=hs, hst=hs.T,
        wp=wo[:POOL_WIDTH].astype(BF16), wa=wa.astype(BF16),
        wm=wo[POOL_WIDTH + MLA_HEADS * MLA_V:].astype(BF16),
        peer_wq=peer_wq[l].astype(BF16),
        sk=peer_subkeys[l].reshape(2 * PEER_HEADS, PEER_KEYS, PEER_KEYS),
        u_tab=_pack_table(peer_u[l]), v_tab=_pack_table(peer_v[l]),
    )


def _peer_constants():
    R = PEER_SLOTS * ROW_SUB
    pairs = _candidate_pairs()
    ncand = -(-len(pairs) // 8) * 8
    k1 = jnp.array([p[0] for p in pairs] + [-1] * (ncand - len(pairs)))[:, None]
    k2 = jnp.array([p[1] for p in pairs] + [-1] * (ncand - len(pairs)))[:, None]
    rank = jnp.arange(PEER_TOPK)[None, :]
    g1 = (k1 == rank).astype(F32)
    g2 = (k2 == rank).astype(F32)
    rep = (jnp.arange(PEER_SLOTS)[:, None] == jnp.arange(R)[None, :] // ROW_SUB).astype(F32)
    return g1, g2, rep


def _rope_tables(n):
    n_rows = n // GRID_W
    row = jnp.repeat(jnp.arange(n_rows), GRID_W, total_repeat_length=n).astype(F32)
    col = (jnp.arange(n) % GRID_W).astype(F32)
    per_axis = MLA_ROPE // 2
    freqs = ROPE_BASE ** (-jnp.arange(0, per_axis, 2, dtype=F32) / per_axis)
    ang = jnp.concatenate([row[:, None] * freqs, col[:, None] * freqs], axis=-1)
    c, s = jnp.cos(ang), jnp.sin(ang)
    cos = jnp.concatenate([jnp.ones((n, MLA_NOPE), F32), c, c, jnp.ones((n, HEAD_PAD - MLA_QK), F32)], axis=-1)
    sin = jnp.concatenate([jnp.zeros((n, MLA_NOPE), F32), -s, s, jnp.zeros((n, HEAD_PAD - MLA_QK), F32)], axis=-1)
    return cos, sin


def _staged_ids(offs):
    blk = offs.reshape(-1, PEER_TB, PEER_SLOTS)
    spare = jnp.broadcast_to(blk[:, :1], (blk.shape[0], PEER_RING, PEER_SLOTS))
    return jnp.concatenate([blk, spare], axis=1)


def _peer_ffn(x, w, shift, scale, gate, consts):
    B, L, D = x.shape
    g1, g2, rep = consts
    qp_t, h = peer_query(x, w["n2"], shift, scale, w["peer_wq"])
    ids_t, gates_t = peer_topk(qp_t, w["sk"], g1, g2)
    ids_blocks = _staged_ids(ids_t.T * ROW_SUB)
    a_t = peer_scores(ids_blocks, h.reshape(B * L, ROW_SUB, 256), w["u_tab"], rep)
    y = peer_combine(ids_blocks, a_t, gates_t, w["v_tab"], rep)
    return gated_residual(x, y.reshape(B, L, D), gate)


def kernel(x, c, ctx, c_ctx, norm1_gain, norm2_gain, w_ada, b_ada, w_in, pool_w, pool_scale, mla_q_norm,
           mla_kv_norm, w_uq, w_ukv, q_norm, k_norm, ml_gate_bias, ml_out_norm, w_out, peer_wq, peer_subkeys,
           peer_u, peer_v):
    B, S, D = x.shape
    Lc_ctx = ctx.shape[1]
    depth = w_in.shape[0]
    cos_x, sin_x = _rope_tables(S)
    cos_c = jnp.ones((Lc_ctx, HEAD_PAD), F32)
    sin_c = jnp.zeros((Lc_ctx, HEAD_PAD), F32)
    consts = _peer_constants()
    cond8 = jnp.zeros((8, D), F32).at[:B].set(c).at[B].set(c_ctx)
    W = ML_WIDTH
    zero_state = (jnp.zeros((B, 2, W, W), F32), jnp.zeros((B, 2, 1, W), F32),
                  jnp.full((B, 2, 1, W), NEG_INIT, F32))
    x_ctx = ctx
    for l in range(depth):
        last = l == depth - 1
        w = _layer_weights(l, norm1_gain, norm2_gain, w_in, pool_w, pool_scale, mla_q_norm, mla_kv_norm, w_uq,
                           w_ukv, q_norm, k_norm, ml_gate_bias, ml_out_norm, w_out, peer_wq, peer_subkeys,
                           peer_u, peer_v)
        mods = ada_mod(cond8, w_ada[l], b_ada[l])
        m_x = [mods[:B, i * D:(i + 1) * D].reshape(B, 1, D) for i in range(6)]
        m_c = [jnp.broadcast_to(mods[B, i * D:(i + 1) * D].reshape(1, 1, D), (B, 1, D)) for i in range(6)]

        zp_c, zm_c, zl_c = in_proj(x_ctx, w["n1"], m_c[0], m_c[1], w["w_in"])
        qc, kc, vc = mla_qkv(zm_c, w["qan"], w["kvan"], w["wq"], w["wk"], w["wv"], w["pk"], w["qg"], w["kg"],
                             cos_c, sin_c)
        hf_c, hb_c, cT, nT, mT = mlstm(zl_c, w["gate_bias"], zero_state)
        if not last:
            pool_c = pool_mixer(zp_c, w["wbd"], w["pool_scale"])
            attn_c = attention(qc, kc, vc)
            xc = out_proj(x_ctx, pool_c, attn_c, hf_c, hb_c, zl_c, w["mlg"], w["hs"], w["hst"],
                          w["wp"], w["wa"], w["wm"], m_c[2])
            x_ctx_new = _peer_ffn(xc, w, m_c[3], m_c[4], m_c[5], consts)

        zp, zm, zl = in_proj(x, w["n1"], m_x[0], m_x[1], w["w_in"])
        pool_x = pool_mixer(zp, w["wbd"], w["pool_scale"])
        q, k, v = mla_qkv(zm, w["qan"], w["kvan"], w["wq"], w["wk"], w["wv"], w["pk"], w["qg"], w["kg"],
                          cos_x, sin_x)
        attn_x = attention(q, kc, vc, k, v)
        hf, hb, _, _, _ = mlstm(zl, w["gate_bias"], (cT, nT, mT))
        x = out_proj(x, pool_x, attn_x, hf, hb, zl, w["mlg"], w["hs"], w["hst"],
                     w["wp"], w["wa"], w["wm"], m_x[2])
        x = _peer_ffn(x, w, m_x[3], m_x[4], m_x[5], consts)
        if not last:
            x_ctx = x_ctx_new
    return x
```

```python
import functools
import math

import jax
import jax.numpy as jnp
from jax import lax
from jax.experimental import pallas as pl
from jax.experimental.pallas import tpu as pltpu

F32 = jnp.float32
BF16 = jnp.bfloat16
HIGHEST = lax.Precision.HIGHEST

EPS = 1e-6
D_MODEL = 1024
GRID_W = 64
ROPE_BASE = 10000.0
POOL_WINDOWS = (2, 4, 8, 16)
POOL_GROUP = 64
POOL_WIDTH = 256
MLA_HEADS = 8
MLA_NOPE = 64
MLA_ROPE = 32
MLA_QK = 96
MLA_V = 64
MLA_Q_RANK = 384
MLA_KV_RANK = 256
MLA_SCALE = MLA_QK ** -0.5
Q_PRESCALE = MLA_SCALE * math.log2(math.e)
HEAD_PAD = 128
ATT_TQ = 512
ATT_TC = 512
VT_ROWS = 80
ML_HEADS = 4
ML_DK = 64
ML_WIDTH = 256
ML_CHUNK = 128
NEG_INIT = -1e30
IN_SIZES = (256, 384, 256, 32, 256, 256, 256, 256, 16)
PEER_HEADS = 8
PEER_KEYS = 128
PEER_TOPK = 16
PEER_SLOTS = PEER_HEADS * PEER_TOPK
PEER_TB = 128
PEER_RING = 8
ROW_SUB = 4

Z_POOL, Z_MLA, Z_ML = 256, 768, 1152
VMEM_LIMIT = 56 * 1024 * 1024


def _cp(*sem, vmem=VMEM_LIMIT):
    return pltpu.CompilerParams(dimension_semantics=sem, vmem_limit_bytes=vmem)


def _ada_kernel(c_ref, w_ref, b_ref, o_ref):
    c = c_ref[...]
    s = c * (1.0 / (1.0 + jnp.exp(-c)))
    o_ref[...] = jnp.dot(s, w_ref[...], precision=HIGHEST, preferred_element_type=F32) + b_ref[...]


def ada_mod(cond8, w, b):
    n = w.shape[1]
    tn = n // 4
    return pl.pallas_call(
        _ada_kernel,
        grid=(n // tn,),
        in_specs=[pl.BlockSpec((8, D_MODEL), lambda j: (0, 0)),
                  pl.BlockSpec((D_MODEL, tn), lambda j: (0, j)),
                  pl.BlockSpec((1, tn), lambda j: (0, j))],
        out_specs=pl.BlockSpec((8, tn), lambda j: (0, j)),
        out_shape=jax.ShapeDtypeStruct((8, n), F32),
        compiler_params=_cp("arbitrary"),
        name="ada_mod",
    )(cond8, w, b.reshape(1, n))


def _modulated(x, gain, shift, scale):
    ms = jnp.mean(x * x, axis=-1, keepdims=True)
    y = x * lax.rsqrt(ms + EPS) * gain
    return y * (1.0 + scale) + shift


def _inproj_kernel(x_ref, gain_ref, shift_ref, scale_ref, w_ref, zp_ref, zm_ref, zl_ref):
    h = _modulated(x_ref[0], gain_ref[...], shift_ref[0], scale_ref[0])
    res = jnp.dot(h.astype(BF16), w_ref[...], preferred_element_type=F32)
    zp_ref[0] = res[:, :Z_POOL]
    zm_ref[0] = res[:, Z_POOL:Z_POOL + Z_MLA]
    zl_ref[0] = res[:, Z_POOL + Z_MLA:]


def in_proj(x, gain, shift, scale, w):
    B, L, D = x.shape
    tm = min(512, L)
    n = w.shape[1]
    vec = pl.BlockSpec((1, 1, D), lambda b, i: (b, 0, 0))
    return pl.pallas_call(
        _inproj_kernel,
        grid=(B, L // tm),
        in_specs=[pl.BlockSpec((1, tm, D), lambda b, i: (b, i, 0)),
                  pl.BlockSpec((1, D), lambda b, i: (0, 0)),
                  vec, vec,
                  pl.BlockSpec((D, n), lambda b, i: (0, 0))],
        out_specs=[pl.BlockSpec((1, tm, Z_POOL), lambda b, i: (b, i, 0)),
                   pl.BlockSpec((1, tm, Z_MLA), lambda b, i: (b, i, 0)),
                   pl.BlockSpec((1, tm, Z_ML), lambda b, i: (b, i, 0))],
        out_shape=[jax.ShapeDtypeStruct((B, L, Z_POOL), F32),
                   jax.ShapeDtypeStruct((B, L, Z_MLA), F32),
                   jax.ShapeDtypeStruct((B, L, Z_ML), F32)],
        compiler_params=_cp("parallel", "parallel"),
        name="in_proj",
    )(x, gain.reshape(1, D), shift, scale, w)


def _peerq_kernel(x_ref, gain_ref, shift_ref, scale_ref, w_ref, qp_ref, h_ref):
    h = _modulated(x_ref[0], gain_ref[...], shift_ref[0], scale_ref[0])
    hb = h.astype(BF16)
    h_ref[...] = hb.astype(F32)
    res = jnp.dot(hb, w_ref[...], preferred_element_type=F32)
    for g in range(2 * PEER_HEADS):
        qp_ref[g] = res[:, g * PEER_KEYS:(g + 1) * PEER_KEYS].T


def peer_query(x, gain, shift, scale, w):
    B, L, D = x.shape
    tm = min(512, L)
    nb = L // tm
    n = w.shape[1]
    vec = pl.BlockSpec((1, 1, D), lambda b, i: (b, 0, 0))
    return pl.pallas_call(
        _peerq_kernel,
        grid=(B, nb),
        in_specs=[pl.BlockSpec((1, tm, D), lambda b, i: (b, i, 0)),
                  pl.BlockSpec((1, D), lambda b, i: (0, 0)),
                  vec, vec,
                  pl.BlockSpec((D, n), lambda b, i: (0, 0))],
        out_specs=[pl.BlockSpec((2 * PEER_HEADS, PEER_KEYS, tm), lambda b, i: (0, 0, b * nb + i)),
                   pl.BlockSpec((tm, D), lambda b, i: (b * nb + i, 0))],
        out_shape=[jax.ShapeDtypeStruct((2 * PEER_HEADS, PEER_KEYS, B * L), F32),
                   jax.ShapeDtypeStruct((B * L, D), F32)],
        compiler_params=_cp("parallel", "parallel"),
        name="peer_query",
    )(x, gain.reshape(1, D), shift, scale, w)


def _pool_kernel(p_ref, c_ref, n_ref, wbd_ref, sc_ref, o_ref, *, L, T):
    i = pl.program_id(1)
    cur = c_ref[0]
    u3 = jnp.concatenate([p_ref[0], cur, n_ref[0]], axis=0).astype(BF16)
    t = i * T + lax.broadcasted_iota(jnp.int32, (T, 3 * T), 0)
    s = (i - 1) * T + lax.broadcasted_iota(jnp.int32, (T, 3 * T), 1)
    lane = lax.broadcasted_iota(jnp.int32, (T, POOL_WIDTH), 1)
    trow = i * T + lax.broadcasted_iota(jnp.int32, (T, POOL_WIDTH), 0)
    win = jnp.zeros((T, POOL_WIDTH), F32)
    for g, w in enumerate(POOL_WINDOWS):
        lo = jnp.maximum(t - w // 2, 0)
        hi = jnp.minimum(t + w // 2, L)
        band = jnp.where((s >= lo) & (s < hi), 1.0, 0.0).astype(BF16)
        ws = jnp.dot(band, u3, preferred_element_type=F32)
        cnt = (jnp.minimum(trow + w // 2, L) - jnp.maximum(trow - w // 2, 0)).astype(F32)
        in_group = (lane >= g * POOL_GROUP) & (lane < (g + 1) * POOL_GROUP)
        win = jnp.where(in_group, ws / cnt, win)
    d = win - cur
    y = jnp.dot(d.astype(BF16), wbd_ref[...], preferred_element_type=F32)
    o_ref[0] = y * sc_ref[...]


def pool_mixer(z_pool, wbd, scale):
    B, L, C = z_pool.shape
    T = 256
    nb = L // T
    return pl.pallas_call(
        functools.partial(_pool_kernel, L=L, T=T),
        grid=(B, nb),
        in_specs=[pl.BlockSpec((1, T, C), lambda b, i: (b, jnp.maximum(i - 1, 0), 0)),
                  pl.BlockSpec((1, T, C), lambda b, i: (b, i, 0)),
                  pl.BlockSpec((1, T, C), lambda b, i: (b, jnp.minimum(i + 1, nb - 1), 0)),
                  pl.BlockSpec((C, C), lambda b, i: (0, 0)),
                  pl.BlockSpec((1, C), lambda b, i: (0, 0))],
        out_specs=pl.BlockSpec((1, T, C), lambda b, i: (b, i, 0)),
        out_shape=jax.ShapeDtypeStruct((B, L, C), F32),
        compiler_params=_cp("parallel", "parallel"),
        name="pool_mixer",
    )(z_pool, z_pool, z_pool, wbd, scale.reshape(1, C))


def _rms(x, gain, n):
    ss = jnp.sum(x * x, axis=-1, keepdims=True) * (1.0 / n)
    return x * lax.rsqrt(ss + EPS) * gain


def _mla_kernel(z_ref, qan_ref, kvan_ref, wq_ref, wk_ref, wv_ref, pk_ref, qg_ref, kg_ref,
                cos_ref, sin_ref, q_out, k_out, v_out):
    z = z_ref[0]
    zq = z[:, :MLA_Q_RANK]
    zkv = z[:, MLA_Q_RANK:MLA_Q_RANK + MLA_KV_RANK]
    zkr = z[:, MLA_Q_RANK + MLA_KV_RANK:]
    nq = _rms(zq, qan_ref[...], MLA_Q_RANK).astype(BF16)
    nkv = _rms(zkv, kvan_ref[...], MLA_KV_RANK).astype(BF16)
    qp = jnp.dot(nq, wq_ref[...], preferred_element_type=F32)
    kp = jnp.dot(nkv, wk_ref[...], preferred_element_type=F32)
    kp = kp + jnp.dot(zkr, pk_ref[...], precision=HIGHEST, preferred_element_type=F32)
    vp = jnp.dot(nkv, wv_ref[...], preferred_element_type=F32)
    cos = cos_ref[...]
    sin = sin_ref[...]
    lane = lax.broadcasted_iota(jnp.int32, cos.shape, 1)
    first_half = lane < MLA_NOPE + MLA_ROPE // 2
    extra = (VT_ROWS - MLA_V, z.shape[0])
    ones_rows = jnp.where(lax.broadcasted_iota(jnp.int32, extra, 0) == 0, 1.0, 0.0)
    for h in range(MLA_HEADS):
        sl = slice(h * HEAD_PAD, (h + 1) * HEAD_PAD)
        for src, gain_ref, out, mult in ((qp, qg_ref, q_out, Q_PRESCALE), (kp, kg_ref, k_out, 1.0)):
            xn = _rms(src[:, sl], gain_ref[...], MLA_QK)
            partner = jnp.where(first_half, pltpu.roll(xn, HEAD_PAD - MLA_ROPE // 2, 1),
                                pltpu.roll(xn, MLA_ROPE // 2, 1))
            xr = xn * cos + partner * sin
            out[0, h] = (xr * mult).astype(BF16)
        v_out[0, h] = jnp.concatenate([vp[:, sl].T[:MLA_V, :], ones_rows], axis=0).astype(BF16)


def mla_qkv(z_mla, qan, kvan, wq, wk, wv, pk, qg, kg, cos, sin):
    B, L, _ = z_mla.shape
    T = 256
    full = lambda shape: pl.BlockSpec(shape, lambda b, i: (0,) * len(shape))
    head_out = pl.BlockSpec((1, MLA_HEADS, T, HEAD_PAD), lambda b, i: (b, 0, i, 0))
    out_sds = jax.ShapeDtypeStruct((B, MLA_HEADS, L, HEAD_PAD), BF16)
    vt_out = pl.BlockSpec((1, MLA_HEADS, VT_ROWS, T), lambda b, i: (b, 0, 0, i))
    vt_sds = jax.ShapeDtypeStruct((B, MLA_HEADS, VT_ROWS, L), BF16)
    return pl.pallas_call(
        _mla_kernel,
        grid=(B, L // T),
        in_specs=[pl.BlockSpec((1, T, Z_MLA), lambda b, i: (b, i, 0)),
                  full((1, MLA_Q_RANK)), full((1, MLA_KV_RANK)),
                  full(wq.shape), full(wk.shape), full(wv.shape), full(pk.shape),
                  full((1, HEAD_PAD)), full((1, HEAD_PAD)),
                  pl.BlockSpec((T, HEAD_PAD), lambda b, i: (i, 0)),
                  pl.BlockSpec((T, HEAD_PAD), lambda b, i: (i, 0))],
        out_specs=[head_out, head_out, vt_out],
        out_shape=[out_sds, out_sds, vt_sds],
        compiler_params=_cp("parallel", "parallel"),
        name="mla_qkv",
    )(z_mla, qan.reshape(1, -1), kvan.reshape(1, -1), wq, wk, wv, pk, qg, kg, cos, sin)


def _flash_kernel(q_ref, kc_ref, vct_ref, *rest, nchunks):
    if nchunks:
        k_ref, vt_ref, o_ref, s_s, m_s, acc_s = rest
    else:
        o_ref, s_s, m_s, acc_s = rest
    nc = kc_ref.shape[2]

    def scores(h, kblk):
        n = kblk.shape[0]
        s_s[h, :n] = lax.dot_general(kblk, q_ref[0, h], (((1,), (1,)), ((), ())), preferred_element_type=F32)

    def update(h, n, vtblk):
        st = s_s[h, :n]
        m = m_s[h]
        m_new = jnp.maximum(m, jnp.max(st, axis=0, keepdims=True))
        p = jnp.exp2(st - m_new).astype(BF16)
        m_s[h] = m_new
        acc_s[h] = jnp.exp2(m - m_new) * acc_s[h] + jnp.dot(vtblk, p, preferred_element_type=F32)

    m_s[...] = jnp.full(m_s.shape, -jnp.inf, F32)
    acc_s[...] = jnp.zeros(acc_s.shape, F32)
    scores(0, kc_ref[0, 0])
    scores(1, kc_ref[0, 1])
    update(0, nc, vct_ref[0, 0])
    if nchunks:
        scores(0, k_ref[0, 0, pl.ds(0, ATT_TC), :])
    update(1, nc, vct_ref[0, 1])
    if nchunks:
        def body(c, _):
            off = pl.multiple_of(c * ATT_TC, ATT_TC)
            nxt = pl.multiple_of(jnp.minimum(c + 1, nchunks - 1) * ATT_TC, ATT_TC)
            scores(1, k_ref[0, 1, pl.ds(off, ATT_TC), :])
            update(0, ATT_TC, vt_ref[0, 0, :, pl.ds(off, ATT_TC)])
            scores(0, k_ref[0, 0, pl.ds(nxt, ATT_TC), :])
            update(1, ATT_TC, vt_ref[0, 1, :, pl.ds(off, ATT_TC)])
            return 0
        lax.fori_loop(0, nchunks, body, 0, unroll=8)
    o = jnp.concatenate([acc_s[h, :MLA_V] / acc_s[h, MLA_V:MLA_V + 1] for h in range(2)], axis=0)
    o_ref[0] = o.T.astype(BF16)


def attention(q, kc, vct, k=None, vt=None):
    B, H, Lq, _ = q.shape
    Lc = kc.shape[2]
    tq = min(ATT_TQ, Lq)
    pair4 = lambda n, d: pl.BlockSpec((1, 2, n, d), lambda b, h, i: (b, h, 0, 0))
    in_specs = [pl.BlockSpec((1, 2, tq, HEAD_PAD), lambda b, h, i: (b, h, i, 0)),
                pair4(Lc, HEAD_PAD), pair4(VT_ROWS, Lc)]
    args = [q, kc, vct]
    nchunks = 0
    if k is not None:
        Lk = k.shape[2]
        nchunks = Lk // ATT_TC
        in_specs += [pair4(Lk, HEAD_PAD), pair4(VT_ROWS, Lk)]
        args += [k, vt]
    return pl.pallas_call(
        functools.partial(_flash_kernel, nchunks=nchunks),
        grid=(B, H // 2, Lq // tq),
        in_specs=in_specs,
        out_specs=pl.BlockSpec((1, tq, 2 * MLA_V), lambda b, h, i: (b, i, h)),
        out_shape=jax.ShapeDtypeStruct((B, Lq, H * MLA_V), BF16),
        scratch_shapes=[pltpu.VMEM((2, max(ATT_TC, Lc), tq), F32), pltpu.VMEM((2, 1, tq), F32),
                        pltpu.VMEM((2, VT_ROWS, tq), F32)],
        compiler_params=_cp("parallel", "parallel", "arbitrary"),
        name="attention",
    )(*args)


def _log_sigmoid(x):
    return jnp.minimum(x, 0.0) - jnp.log(1.0 + jnp.exp(-jnp.abs(x)))


def _mlstm_direction(d, q, k, v, g, C_s, n_s, m_s):
    Lc = q.shape[0]
    row = lax.broadcasted_iota(jnp.int32, (Lc, Lc), 0)
    col = lax.broadcasted_iota(jnp.int32, (Lc, Lc), 1)
    tri = (col <= row) if d == 0 else (col >= row)
    logf = _log_sigmoid(g)
    bcol = jnp.dot(jnp.where(tri, 1.0, 0.0), logf, precision=HIGHEST, preferred_element_type=F32)
    bT = bcol.T
    gT = g.T
    lane = lax.broadcasted_iota(jnp.int32, (1, ML_WIDTH), 1)
    kb = k.astype(BF16)
    vb = v.astype(BF16)
    Cst = C_s[d]
    nst = n_s[d]
    mst = m_s[d]
    qc = jnp.dot(q.astype(BF16), Cst.astype(BF16), preferred_element_type=F32)
    out = jnp.zeros((Lc, ML_WIDTH), F32)
    ws_all = jnp.zeros((Lc, ML_WIDTH), F32)
    wprev_all = jnp.zeros((1, ML_WIDTH), F32)
    mnew_all = jnp.zeros((1, ML_WIDTH), F32)
    for h in range(ML_HEADS):
        il = 8 * d + h
        fl = 8 * d + 4 + h
        head = (lane >= h * ML_DK) & (lane < (h + 1) * ML_DK)
        bc = bcol[:, fl:fl + 1]
        br = bT[fl:fl + 1, :]
        ir = gT[il:il + 1, :]
        ic = g[:, il:il + 1]
        mprev = mst[:, h * ML_DK:h * ML_DK + 1]
        dmat = jnp.where(tri, bc - br + ir, -jnp.inf)
        inter = bc + mprev
        mj = jnp.maximum(inter, jnp.max(dmat, axis=-1, keepdims=True))
        w_inter = jnp.exp(inter - mj)
        qh = jnp.where(head, q, 0.0)
        s = lax.dot_general(qh.astype(BF16), kb, (((1,), (1,)), ((), ())), preferred_element_type=F32)
        qk = s * jnp.exp(dmat - mj)
        pv = jnp.dot(qk.astype(BF16), vb, preferred_element_type=F32)
        qn = jnp.sum(qh * nst, axis=-1, keepdims=True)
        den = jnp.sum(qk, axis=-1, keepdims=True) + w_inter * qn
        denom = jnp.maximum(jnp.abs(den), jnp.exp(-mj))
        out = jnp.where(head, (pv + qc * w_inter) / denom, out)
        blast = bc[Lc - 1:Lc, :] if d == 0 else bc[0:1, :]
        dec = blast - bc + ic
        mnew = jnp.maximum(blast + mprev, jnp.max(dec, axis=0, keepdims=True))
        wprev = jnp.exp(blast + mprev - mnew)
        ws = jnp.exp(dec - mnew)
        ws_all = jnp.where(head, ws, ws_all)
        wprev_all = jnp.where(head, wprev, wprev_all)
        mnew_all = jnp.where(head, mnew, mnew_all)
    kw = k * ws_all
    upd = jnp.dot(kw.T.astype(BF16), vb, preferred_element_type=F32)
    r2 = lax.broadcasted_iota(jnp.int32, (ML_WIDTH, ML_WIDTH), 0) // ML_DK
    c2 = lax.broadcasted_iota(jnp.int32, (ML_WIDTH, ML_WIDTH), 1) // ML_DK
    C_s[d] = Cst * wprev_all + jnp.where(r2 == c2, upd, 0.0)
    n_s[d] = nst * wprev_all + jnp.sum(kw, axis=0, keepdims=True)
    m_s[d] = mnew_all
    return out


def _mlstm_kernel(qf_ref, kf_ref, vf_ref, gf_ref, qb_ref, kb_ref, vb_ref, gb_ref, bias_ref,
                  c0_ref, n0_ref, m0_ref, hf_ref, hb_ref, cT_ref, nT_ref, mT_ref, C_s, n_s, m_s):
    c = pl.program_id(1)

    @pl.when(c == 0)
    def _():
        C_s[...] = c0_ref[0]
        n_s[...] = n0_ref[0]
        m_s[...] = m0_ref[0]

    scale = ML_DK ** -0.5
    hf_ref[0] = _mlstm_direction(0, qf_ref[0] * scale, kf_ref[0], vf_ref[0],
                                 gf_ref[0] + bias_ref[...], C_s, n_s, m_s)
    hb_ref[0] = _mlstm_direction(1, qb_ref[0] * scale, kb_ref[0], vb_ref[0],
                                 gb_ref[0] + bias_ref[...], C_s, n_s, m_s)

    @pl.when(c == pl.num_programs(1) - 1)
    def _():
        cT_ref[0] = C_s[...]
        nT_ref[0] = n_s[...]
        mT_ref[0] = m_s[...]


def mlstm(z_ml, bias, state):
    B, L, _ = z_ml.shape
    Lc = ML_CHUNK
    nc = L // Lc
    c0, n0, m0 = state
    W = ML_WIDTH
    fwd = lambda j: pl.BlockSpec((1, Lc, W), lambda b, c: (b, c, j))
    bwd = lambda j: pl.BlockSpec((1, Lc, W), lambda b, c: (b, nc - 1 - c, j))
    gcol = 4 * W // 128
    st_c = pl.BlockSpec((1, 2, W, W), lambda b, c: (b, 0, 0, 0))
    st_v = pl.BlockSpec((1, 2, 1, W), lambda b, c: (b, 0, 0, 0))
    return pl.pallas_call(
        _mlstm_kernel,
        grid=(B, nc),
        in_specs=[fwd(0), fwd(1), fwd(2), pl.BlockSpec((1, Lc, 128), lambda b, c: (b, c, gcol)),
                  bwd(0), bwd(1), bwd(2), pl.BlockSpec((1, Lc, 128), lambda b, c: (b, nc - 1 - c, gcol)),
                  pl.BlockSpec((1, 128), lambda b, c: (0, 0)),
                  st_c, st_v, st_v],
        out_specs=[pl.BlockSpec((1, Lc, W), lambda b, c: (b, c, 0)),
                   pl.BlockSpec((1, Lc, W), lambda b, c: (b, nc - 1 - c, 0)),
                   st_c, st_v, st_v],
        out_shape=[jax.ShapeDtypeStruct((B, L, W), F32), jax.ShapeDtypeStruct((B, L, W), F32),
                   jax.ShapeDtypeStruct((B, 2, W, W), F32), jax.ShapeDtypeStruct((B, 2, 1, W), F32),
                   jax.ShapeDtypeStruct((B, 2, 1, W), F32)],
        scratch_shapes=[pltpu.VMEM((2, W, W), F32), pltpu.VMEM((2, 1, W), F32), pltpu.VMEM((2, 1, W), F32)],
        compiler_params=_cp("parallel", "arbitrary"),
        name="mlstm",
    )(z_ml, z_ml, z_ml, z_ml, z_ml, z_ml, z_ml, z_ml, bias, c0, n0, m0)


def _outproj_kernel(x_ref, pool_ref, attn_ref, hf_ref, hb_ref, op_ref, mlg_ref, hs_ref, hst_ref,
                    wp_ref, wa_ref, wm_ref, ga_ref, o_ref):
    h = hf_ref[0] + hb_ref[0]
    ss = jnp.dot(h * h, hs_ref[...], precision=HIGHEST, preferred_element_type=F32) * (1.0 / ML_DK)
    inv = jnp.dot(lax.rsqrt(ss + EPS), hst_ref[...], precision=HIGHEST, preferred_element_type=F32)
    op = op_ref[0]
    ml = h * inv * mlg_ref[...] * (1.0 / (1.0 + jnp.exp(-op)))
    mix = jnp.dot(pool_ref[0].astype(BF16), wp_ref[...], preferred_element_type=F32)
    mix += jnp.dot(attn_ref[0], wa_ref[...], preferred_element_type=F32)
    mix += jnp.dot(ml.astype(BF16), wm_ref[...], preferred_element_type=F32)
    o_ref[0] = x_ref[0] + ga_ref[0] * mix


def out_proj(x, pool, attn, hf, hb, z_ml, mlg, hs, hst, wp, wa, wm, gate):
    B, L, D = x.shape
    T = min(512, L)
    W = ML_WIDTH
    tile = lambda w, j=0: pl.BlockSpec((1, T, w), lambda b, i: (b, i, j))
    full = lambda shape: pl.BlockSpec(shape, lambda b, i: (0,) * len(shape))
    return pl.pallas_call(
        _outproj_kernel,
        grid=(B, L // T),
        in_specs=[tile(D), tile(W), tile(MLA_HEADS * MLA_V), tile(W), tile(W), tile(W, 3),
                  full((1, W)), full(hs.shape), full(hst.shape),
                  full(wp.shape), full(wa.shape), full(wm.shape),
                  pl.BlockSpec((1, 1, D), lambda b, i: (b, 0, 0))],
        out_specs=tile(D),
        out_shape=jax.ShapeDtypeStruct((B, L, D), F32),
        compiler_params=_cp("parallel", "parallel"),
        name="out_proj",
    )(x, pool, attn, hf, hb, z_ml, mlg, hs, hst, wp, wa, wm, gate)


def _top16_rows(s, row):
    big = float(s.shape[0])
    vals, idxs = [], []
    for _ in range(PEER_TOPK):
        m = jnp.max(s, axis=0, keepdims=True)
        idx = jnp.min(jnp.where(s == m, row, big), axis=0, keepdims=True)
        vals.append(m)
        idxs.append(idx)
        s = jnp.where(row == idx, -jnp.inf, s)
    return jnp.concatenate(vals, axis=0), jnp.concatenate(idxs, axis=0)


def _candidate_pairs():
    return [(k1, k2) for k1 in range(PEER_TOPK) for k2 in range(PEER_TOPK) if (k1 + 1) * (k2 + 1) <= PEER_TOPK]


def _topk_kernel(qp_ref, sk_ref, g1_ref, g2_ref, ids_ref, gate_ref):
    T = 256
    row = lax.broadcasted_iota(jnp.int32, (PEER_KEYS, T), 0).astype(F32)
    pick = lambda g_ref, a: jnp.dot(g_ref[...], a, precision=HIGHEST, preferred_element_type=F32)
    ncand = g1_ref.shape[0]
    crow = lax.broadcasted_iota(jnp.int32, (ncand, T), 0).astype(F32)

    def tile(lt, _):
        cols = pl.ds(pl.multiple_of(lt * T, T), T)
        tops = []
        for c in range(2):
            s = jnp.dot(sk_ref[c], qp_ref[c, :, cols], precision=HIGHEST, preferred_element_type=F32)
            tops.append(_top16_rows(s, row))
        cand = pick(g1_ref, tops[0][0]) + pick(g2_ref, tops[1][0])
        cand = jnp.where(crow < float(len(_candidate_pairs())), cand, -jnp.inf)
        expert = pick(g1_ref, tops[0][1]) * float(PEER_KEYS) + pick(g2_ref, tops[1][1])
        best, eids = [], []
        for _ in range(PEER_TOPK):
            m = jnp.max(cand, axis=0, keepdims=True)
            ci = jnp.min(jnp.where(cand == m, crow, float(ncand)), axis=0, keepdims=True)
            sel = crow == ci
            best.append(m)
            eids.append(jnp.sum(jnp.where(sel, expert, 0.0), axis=0, keepdims=True))
            cand = jnp.where(sel, -jnp.inf, cand)
        best = jnp.concatenate(best, axis=0)
        p = jnp.exp(best - best[0:1, :])
        gate_ref[:, cols] = p / jnp.sum(p, axis=0, keepdims=True)
        ids_ref[:, cols] = jnp.concatenate(eids, axis=0).astype(jnp.int32)
        return 0

    lax.fori_loop(0, qp_ref.shape[2] // T, tile, 0)


def peer_topk(qp_t, sk, g1, g2):
    N = qp_t.shape[2]
    T = 512
    return pl.pallas_call(
        _topk_kernel,
        grid=(N // T, PEER_HEADS),
        in_specs=[pl.BlockSpec((2, PEER_KEYS, T), lambda i, h: (h, 0, i)),
                  pl.BlockSpec((2, PEER_KEYS, PEER_KEYS), lambda i, h: (h, 0, 0)),
                  pl.BlockSpec(g1.shape, lambda i, h: (0, 0)),
                  pl.BlockSpec(g2.shape, lambda i, h: (0, 0))],
        out_specs=[pl.BlockSpec((PEER_TOPK, T), lambda i, h: (h, i)),
                   pl.BlockSpec((PEER_TOPK, T), lambda i, h: (h, i))],
        out_shape=[jax.ShapeDtypeStruct((PEER_SLOTS, N), jnp.int32),
                   jax.ShapeDtypeStruct((PEER_SLOTS, N), F32)],
        compiler_params=_cp("parallel", "parallel"),
        name="peer_topk",
    )(qp_t, sk, g1, g2)


def _unpack_rows(x):
    lo = pltpu.bitcast(x << 16, F32)
    hi = pltpu.bitcast(x & jnp.uint32(0xFFFF0000), F32)
    return jnp.concatenate([lo, hi], axis=-1).astype(BF16)


def _token_pipeline(ids_ref, tab_ref, ring, sems, consume, finish_group=None):
    TB, R = PEER_TB, PEER_RING

    def group_copy(k, half):
        first = pl.multiple_of(k * R, R)
        return pltpu.make_async_copy(ids_ref.at[0, pl.ds(first, R)], ring.at[half], sems.at[half])

    group_copy(0, 0).start()

    def body(i, _):
        for half in range(2):
            k = 2 * i + half
            group_copy(k, half).wait()
            group_copy(k + 1, 1 - half).start()
            carry = None
            for s in range(R):
                rows = [tab_ref[pl.ds(ring[half, s, j], ROW_SUB), :] for j in range(PEER_SLOTS)]
                carry = consume(k * R + s, jnp.concatenate(rows, axis=0), carry)
            if finish_group is not None:
                finish_group(carry)
        return 0

    lax.fori_loop(0, TB // (2 * R), body, 0)
    group_copy(TB // R, 0).wait()


def _peer_u_kernel(ids_ref, h_ref, tab_ref, group_ref, at_ref, ring, sems, acc_s):
    TB = at_ref.shape[1]
    lane_t = lax.broadcasted_iota(jnp.int32, (2 * 128, TB), 1)
    acc_s[...] = jnp.zeros(acc_s.shape, F32)

    def consume(t, rows, acc):
        h = h_ref[pl.ds(t, 1), :]
        lo = pltpu.bitcast(rows << 16, F32).reshape(-1, 8, 128)
        hi = pltpu.bitcast(rows & jnp.uint32(0xFFFF0000), F32).reshape(-1, 8, 128)
        chunk = lambda s, half: h[:, s * 256 + half * 128:s * 256 + half * 128 + 128]
        h_lo = jnp.concatenate([chunk(s, 0) for s in range(ROW_SUB)] * 2, axis=0)
        h_hi = jnp.concatenate([chunk(s, 1) for s in range(ROW_SUB)] * 2, axis=0)
        prod = jnp.concatenate([(lo * h_lo[None]).reshape(-1, 128), (hi * h_hi[None]).reshape(-1, 128)],
                               axis=-1).astype(BF16)
        column = jnp.where(lane_t == t, 1.0, 0.0).astype(BF16)
        half = prod.shape[0] // 2
        parts = [jnp.dot(prod[:half], column, preferred_element_type=F32),
                 jnp.dot(prod[half:], column, preferred_element_type=F32)]
        return parts if acc is None else [a + p for a, p in zip(acc, parts)]

    def finish_group(acc):
        half = acc_s.shape[0] // 2
        acc_s[:half] += acc[0]
        acc_s[half:] += acc[1]

    _token_pipeline(ids_ref, tab_ref, ring, sems, consume, finish_group)
    at_ref[...] = jnp.dot(group_ref[...], acc_s[...], precision=HIGHEST, preferred_element_type=F32)


def _peer_specs(TB):
    R = PEER_SLOTS * ROW_SUB
    ids_spec = pl.BlockSpec((1, TB + PEER_RING, PEER_SLOTS), lambda i: (i, 0, 0))
    table_spec = pl.BlockSpec(memory_space=pltpu.VMEM)
    ring = [pltpu.SMEM((2, PEER_RING, PEER_SLOTS), jnp.int32), pltpu.SemaphoreType.DMA((2,))]
    return ids_spec, table_spec, ring


def peer_scores(ids_blocks, h, tab, group):
    N = h.shape[0]
    TB = PEER_TB
    R = PEER_SLOTS * ROW_SUB
    ids_spec, table_spec, ring = _peer_specs(TB)
    return pl.pallas_call(
        _peer_u_kernel,
        grid=(N // TB,),
        in_specs=[ids_spec, pl.BlockSpec((TB, D_MODEL), lambda i: (i, 0)), table_spec,
                  pl.BlockSpec(group.shape, lambda i: (0, 0))],
        out_specs=pl.BlockSpec((PEER_SLOTS, TB), lambda i: (0, i)),
        out_shape=jax.ShapeDtypeStruct((PEER_SLOTS, N), F32),
        scratch_shapes=ring + [pltpu.VMEM((R, TB), F32)],
        compiler_params=_cp("arbitrary"),
        name="peer_scores",
    )(ids_blocks, h, tab, group)


def _peer_v_kernel(ids_ref, at_ref, gt_ref, tab_ref, rep_ref, o_ref, ring, sems, wexp):
    a = at_ref[...]
    w = gt_ref[...] * (0.5 * a * (1.0 + lax.erf(a * (1.0 / math.sqrt(2.0)))))
    wexp[...] = jnp.dot(w.T, rep_ref[...], precision=HIGHEST, preferred_element_type=F32)
    R = PEER_SLOTS * ROW_SUB
    own = (lax.broadcasted_iota(jnp.int32, (8, R), 0) ==
           lax.broadcasted_iota(jnp.int32, (8, R), 1) % ROW_SUB)

    def consume(t, rows, carry):
        x = _unpack_rows(rows)
        wsel = jnp.where(own, wexp[pl.ds(t, 1), :], 0.0).astype(BF16)
        y = jnp.dot(wsel, x, preferred_element_type=F32)
        for s in range(ROW_SUB):
            o_ref[pl.ds(t, 1), s * 256:(s + 1) * 256] = y[s:s + 1, :]
        return carry

    _token_pipeline(ids_ref, tab_ref, ring, sems, consume)


def peer_combine(ids_flat, at, gt, tab, rep):
    N = at.shape[1]
    TB = PEER_TB
    R = PEER_SLOTS * ROW_SUB
    ids_spec, table_spec, ring = _peer_specs(TB)
    slot_tile = pl.BlockSpec((PEER_SLOTS, TB), lambda i: (0, i))
    return pl.pallas_call(
        _peer_v_kernel,
        grid=(N // TB,),
        in_specs=[ids_spec, slot_tile, slot_tile, table_spec, pl.BlockSpec(rep.shape, lambda i: (0, 0))],
        out_specs=pl.BlockSpec((TB, D_MODEL), lambda i: (i, 0)),
        out_shape=jax.ShapeDtypeStruct((N, D_MODEL), F32),
        scratch_shapes=ring + [pltpu.VMEM((TB, R), F32)],
        compiler_params=_cp("arbitrary"),
        name="peer_combine",
    )(ids_flat, at, gt, tab, rep)


def _residual_kernel(x_ref, y_ref, g_ref, o_ref):
    o_ref[0] = x_ref[0] + g_ref[0] * y_ref[0]


def gated_residual(x, y, gate):
    B, L, D = x.shape
    T = min(1024, L)
    tile = pl.BlockSpec((1, T, D), lambda b, i: (b, i, 0))
    return pl.pallas_call(
        _residual_kernel,
        grid=(B, L // T),
        in_specs=[tile, tile, pl.BlockSpec((1, 1, D), lambda b, i: (b, 0, 0))],
        out_specs=tile,
        out_shape=jax.ShapeDtypeStruct((B, L, D), F32),
        compiler_params=_cp("parallel", "parallel"),
        name="gated_residual",
    )(x, y, gate)


def _pad_last(a, n):
    return jnp.pad(a, [(0, 0)] * (a.ndim - 1) + [(0, n - a.shape[-1])])


def _pack_kernel(x_ref, o_ref):
    n = x_ref.shape[0]
    bits = pltpu.bitcast(x_ref[...].astype(BF16).astype(F32), jnp.uint32)
    for s in range(ROW_SUB):
        lo = bits[:, s * 256:s * 256 + 128] >> 16
        hi = bits[:, s * 256 + 128:(s + 1) * 256] & jnp.uint32(0xFFFF0000)
        o_ref[pl.ds(s, n, stride=ROW_SUB), :] = lo | hi


def _pack_table(tab):
    E, D = tab.shape
    te = 512
    return pl.pallas_call(
        _pack_kernel,
        grid=(E // te,),
        in_specs=[pl.BlockSpec((te, D), lambda i: (i, 0))],
        out_specs=pl.BlockSpec((te * ROW_SUB, 128), lambda i: (i, 0)),
        out_shape=jax.ShapeDtypeStruct((E * ROW_SUB, 128), jnp.uint32),
        compiler_params=_cp("parallel"),
        name="pack_table",
    )(tab)


def _layer_weights(l, norm1_gain, norm2_gain, w_in, pool_w, pool_scale, mla_q_norm, mla_kv_norm, w_uq, w_ukv,
                   q_norm, k_norm, ml_gate_bias, ml_out_norm, w_out, peer_wq, peer_subkeys, peer_u, peer_v):
    parts, start = [], 0
    for size in IN_SIZES:
        parts.append(w_in[l][:, start:start + size])
        start += size
    w_in_r = jnp.concatenate([parts[0], parts[1], parts[2], _pad_last(parts[3], 128), parts[4], parts[5],
                              parts[6], parts[7], _pad_last(parts[8], 128)], axis=1).astype(BF16)
    wbd = jnp.zeros((POOL_WIDTH, POOL_WIDTH), F32)
    for g in range(len(POOL_WINDOWS)):
        sl = slice(g * POOL_GROUP, (g + 1) * POOL_GROUP)
        wbd = wbd.at[sl, sl].set(pool_w[l, g])
    wq = _pad_last(w_uq[l].reshape(MLA_Q_RANK, MLA_HEADS, MLA_QK), HEAD_PAD).reshape(MLA_Q_RANK, -1)
    ukv = w_ukv[l].reshape(MLA_KV_RANK, MLA_HEADS, MLA_NOPE + MLA_V)
    wk = _pad_last(ukv[..., :MLA_NOPE], HEAD_PAD).reshape(MLA_KV_RANK, -1)
    wv = _pad_last(ukv[..., MLA_NOPE:], HEAD_PAD).reshape(MLA_KV_RANK, -1)
    r = jnp.arange(128)[:, None]
    cidx = jnp.arange(MLA_HEADS * HEAD_PAD)[None, :]
    pk = ((r < MLA_ROPE) & (cidx % HEAD_PAD == MLA_NOPE + r)).astype(F32)
    lane = jnp.arange(ML_WIDTH)
    hs = (lane[:, None] // ML_DK == jnp.arange(128)[None, :]).astype(F32)
    wo = w_out[l]
    wa = wo[POOL_WIDTH:POOL_WIDTH + MLA_HEADS * MLA_V]
    return dict(
        n1=norm1_gain[l], n2=norm2_gain[l], w_in=w_in_r,
        wbd=wbd.astype(BF16), pool_scale=pool_scale[l],
        qan=mla_q_norm[l], kvan=mla_kv_norm[l],
        wq=wq.astype(BF16), wk=wk.astype(BF16), wv=wv.astype(BF16), pk=pk,
        qg=_pad_last(q_norm[l], HEAD_PAD).reshape(1, HEAD_PAD), kg=_pad_last(k_norm[l], HEAD_PAD).reshape(1, HEAD_PAD),
        gate_bias=_pad_last(ml_gate_bias[l], 128).reshape(1, 128),
        mlg=ml_out_norm[l].reshape(1, ML_WIDTH), hs=hs, hst=hs.T,
        wp=wo[:POOL_WIDTH].astype(BF16), wa=wa.astype(BF16),
        wm=wo[POOL_WIDTH + MLA_HEADS * MLA_V:].astype(BF16),
        peer_wq=peer_wq[l].astype(BF16),
        sk=peer_subkeys[l].reshape(2 * PEER_HEADS, PEER_KEYS, PEER_KEYS),
        u_tab=_pack_table(peer_u[l]), v_tab=_pack_table(peer_v[l]),
    )


def _peer_constants():
    R = PEER_SLOTS * ROW_SUB
    pairs = _candidate_pairs()
    ncand = -(-len(pairs) // 8) * 8
    k1 = jnp.array([p[0] for p in pairs] + [-1] * (ncand - len(pairs)))[:, None]
    k2 = jnp.array([p[1] for p in pairs] + [-1] * (ncand - len(pairs)))[:, None]
    rank = jnp.arange(PEER_TOPK)[None, :]
    g1 = (k1 == rank).astype(F32)
    g2 = (k2 == rank).astype(F32)
    rep = (jnp.arange(PEER_SLOTS)[:, None] == jnp.arange(R)[None, :] // ROW_SUB).astype(F32)
    return g1, g2, rep


def _rope_tables(n):
    n_rows = n // GRID_W
    row = jnp.repeat(jnp.arange(n_rows), GRID_W, total_repeat_length=n).astype(F32)
    col = (jnp.arange(n) % GRID_W).astype(F32)
    per_axis = MLA_ROPE // 2
    freqs = ROPE_BASE ** (-jnp.arange(0, per_axis, 2, dtype=F32) / per_axis)
    ang = jnp.concatenate([row[:, None] * freqs, col[:, None] * freqs], axis=-1)
    c, s = jnp.cos(ang), jnp.sin(ang)
    cos = jnp.concatenate([jnp.ones((n, MLA_NOPE), F32), c, c, jnp.ones((n, HEAD_PAD - MLA_QK), F32)], axis=-1)
    sin = jnp.concatenate([jnp.zeros((n, MLA_NOPE), F32), -s, s, jnp.zeros((n, HEAD_PAD - MLA_QK), F32)], axis=-1)
    return cos, sin


def _staged_ids(offs):
    blk = offs.reshape(-1, PEER_TB, PEER_SLOTS)
    spare = jnp.broadcast_to(blk[:, :1], (blk.shape[0], PEER_RING, PEER_SLOTS))
    return jnp.concatenate([blk, spare], axis=1)


def _peer_ffn(x, w, shift, scale, gate, consts):
    B, L, D = x.shape
    g1, g2, rep = consts
    qp_t, h = peer_query(x, w["n2"], shift, scale, w["peer_wq"])
    ids_t, gates_t = peer_topk(qp_t, w["sk"], g1, g2)
    ids_blocks = _staged_ids(ids_t.T * ROW_SUB)
    a_t = peer_scores(ids_blocks, h, w["u_tab"], rep)
    y = peer_combine(ids_blocks, a_t, gates_t, w["v_tab"], rep)
    return gated_residual(x, y.reshape(B, L, D), gate)


def kernel(x, c, ctx, c_ctx, norm1_gain, norm2_gain, w_ada, b_ada, w_in, pool_w, pool_scale, mla_q_norm,
           mla_kv_norm, w_uq, w_ukv, q_norm, k_norm, ml_gate_bias, ml_out_norm, w_out, peer_wq, peer_subkeys,
           peer_u, peer_v):
    B, S, D = x.shape
    Lc_ctx = ctx.shape[1]
    depth = w_in.shape[0]
    cos_x, sin_x = _rope_tables(S)
    cos_c = jnp.ones((Lc_ctx, HEAD_PAD), F32)
    sin_c = jnp.zeros((Lc_ctx, HEAD_PAD), F32)
    consts = _peer_constants()
    cond8 = jnp.zeros((8, D), F32).at[:B].set(c).at[B].set(c_ctx)
    W = ML_WIDTH
    zero_state = (jnp.zeros((B, 2, W, W), F32), jnp.zeros((B, 2, 1, W), F32),
                  jnp.full((B, 2, 1, W), NEG_INIT, F32))
    x_ctx = ctx
    for l in range(depth):
        last = l == depth - 1
        w = _layer_weights(l, norm1_gain, norm2_gain, w_in, pool_w, pool_scale, mla_q_norm, mla_kv_norm, w_uq,
                           w_ukv, q_norm, k_norm, ml_gate_bias, ml_out_norm, w_out, peer_wq, peer_subkeys,
                           peer_u, peer_v)
        mods = ada_mod(cond8, w_ada[l], b_ada[l])
        m_x = [mods[:B, i * D:(i + 1) * D].reshape(B, 1, D) for i in range(6)]
        m_c = [jnp.broadcast_to(mods[B, i * D:(i + 1) * D].reshape(1, 1, D), (B, 1, D)) for i in range(6)]

        zp_c, zm_c, zl_c = in_proj(x_ctx, w["n1"], m_c[0], m_c[1], w["w_in"])
        qc, kc, vc = mla_qkv(zm_c, w["qan"], w["kvan"], w["wq"], w["wk"], w["wv"], w["pk"], w["qg"], w["kg"],
                             cos_c, sin_c)
        hf_c, hb_c, cT, nT, mT = mlstm(zl_c, w["gate_bias"], zero_state)
        if not last:
            pool_c = pool_mixer(zp_c, w["wbd"], w["pool_scale"])
            attn_c = attention(qc, kc, vc)
            xc = out_proj(x_ctx, pool_c, attn_c, hf_c, hb_c, zl_c, w["mlg"], w["hs"], w["hst"],
                          w["wp"], w["wa"], w["wm"], m_c[2])
            x_ctx_new = _peer_ffn(xc, w, m_c[3], m_c[4], m_c[5], consts)

        zp, zm, zl = in_proj(x, w["n1"], m_x[0], m_x[1], w["w_in"])
        pool_x = pool_mixer(zp, w["wbd"], w["pool_scale"])
        q, k, v = mla_qkv(zm, w["qan"], w["kvan"], w["wq"], w["wk"], w["wv"], w["pk"], w["qg"], w["kg"],
                          cos_x, sin_x)
        attn_x = attention(q, kc, vc, k, v)
        hf, hb, _, _, _ = mlstm(zl, w["gate_bias"], (cT, nT, mT))
        x = out_proj(x, pool_x, attn_x, hf, hb, zl, w["mlg"], w["hs"], w["hst"],
                     w["wp"], w["wa"], w["wm"], m_x[2])
        x = _peer_ffn(x, w, m_x[3], m_x[4], m_x[5], consts)
        if not last:
            x_ctx = x_ctx_new
    return x
```

```python
import functools
import math

import jax
import jax.numpy as jnp
from jax import lax
from jax.experimental import pallas as pl
from jax.experimental.pallas import tpu as pltpu

F32 = jnp.float32
BF16 = jnp.bfloat16
HIGHEST = lax.Precision.HIGHEST

EPS = 1e-6
D_MODEL = 1024
GRID_W = 64
ROPE_BASE = 10000.0
POOL_WINDOWS = (2, 4, 8, 16)
POOL_GROUP = 64
POOL_WIDTH = 256
MLA_HEADS = 8
MLA_NOPE = 64
MLA_ROPE = 32
MLA_QK = 96
MLA_V = 64
MLA_Q_RANK = 384
MLA_KV_RANK = 256
MLA_SCALE = MLA_QK ** -0.5
Q_PRESCALE = MLA_SCALE * math.log2(math.e)
HEAD_PAD = 128
ATT_TQ = 512
ATT_TC = 512
VT_ROWS = 80
ML_HEADS = 4
ML_DK = 64
ML_WIDTH = 256
ML_CHUNK = 128
NEG_INIT = -1e30
IN_SIZES = (256, 384, 256, 32, 256, 256, 256, 256, 16)
PEER_HEADS = 8
PEER_KEYS = 128
PEER_TOPK = 16
PEER_SLOTS = PEER_HEADS * PEER_TOPK
PEER_TB = 128
PEER_RING = 8
ROW_SUB = 4

Z_POOL, Z_MLA, Z_ML = 256, 768, 1152
VMEM_LIMIT = 56 * 1024 * 1024


def _cp(*sem, vmem=VMEM_LIMIT):
    return pltpu.CompilerParams(dimension_semantics=sem, vmem_limit_bytes=vmem)


def _ada_kernel(c_ref, w_ref, b_ref, o_ref):
    c = c_ref[...]
    s = c * (1.0 / (1.0 + jnp.exp(-c)))
    o_ref[...] = jnp.dot(s, w_ref[...], precision=HIGHEST, preferred_element_type=F32) + b_ref[...]


def ada_mod(cond8, w, b):
    n = w.shape[1]
    tn = n // 4
    return pl.pallas_call(
        _ada_kernel,
        grid=(n // tn,),
        in_specs=[pl.BlockSpec((8, D_MODEL), lambda j: (0, 0)),
                  pl.BlockSpec((D_MODEL, tn), lambda j: (0, j)),
                  pl.BlockSpec((1, tn), lambda j: (0, j))],
        out_specs=pl.BlockSpec((8, tn), lambda j: (0, j)),
        out_shape=jax.ShapeDtypeStruct((8, n), F32),
        compiler_params=_cp("arbitrary"),
        name="ada_mod",
    )(cond8, w, b.reshape(1, n))


def _modulated(x, gain, shift, scale):
    ms = jnp.mean(x * x, axis=-1, keepdims=True)
    y = x * lax.rsqrt(ms + EPS) * gain
    return y * (1.0 + scale) + shift


def _inproj_kernel(x_ref, gain_ref, shift_ref, scale_ref, w_ref, zp_ref, zm_ref, zl_ref):
    h = _modulated(x_ref[0], gain_ref[...], shift_ref[0], scale_ref[0])
    res = jnp.dot(h.astype(BF16), w_ref[...], preferred_element_type=F32)
    zp_ref[0] = res[:, :Z_POOL]
    zm_ref[0] = res[:, Z_POOL:Z_POOL + Z_MLA]
    zl_ref[0] = res[:, Z_POOL + Z_MLA:]


def in_proj(x, gain, shift, scale, w):
    B, L, D = x.shape
    tm = min(512, L)
    n = w.shape[1]
    vec = pl.BlockSpec((1, 1, D), lambda b, i: (b, 0, 0))
    return pl.pallas_call(
        _inproj_kernel,
        grid=(B, L // tm),
        in_specs=[pl.BlockSpec((1, tm, D), lambda b, i: (b, i, 0)),
                  pl.BlockSpec((1, D), lambda b, i: (0, 0)),
                  vec, vec,
                  pl.BlockSpec((D, n), lambda b, i: (0, 0))],
        out_specs=[pl.BlockSpec((1, tm, Z_POOL), lambda b, i: (b, i, 0)),
                   pl.BlockSpec((1, tm, Z_MLA), lambda b, i: (b, i, 0)),
                   pl.BlockSpec((1, tm, Z_ML), lambda b, i: (b, i, 0))],
        out_shape=[jax.ShapeDtypeStruct((B, L, Z_POOL), F32),
                   jax.ShapeDtypeStruct((B, L, Z_MLA), F32),
                   jax.ShapeDtypeStruct((B, L, Z_ML), F32)],
        compiler_params=_cp("parallel", "parallel"),
        name="in_proj",
    )(x, gain.reshape(1, D), shift, scale, w)


def _peerq_kernel(x_ref, gain_ref, shift_ref, scale_ref, w_ref, qp_ref, h_ref):
    h = _modulated(x_ref[0], gain_ref[...], shift_ref[0], scale_ref[0])
    hb = h.astype(BF16)
    h_ref[...] = hb.astype(F32)
    res = jnp.dot(hb, w_ref[...], preferred_element_type=F32)
    for g in range(2 * PEER_HEADS):
        qp_ref[g] = res[:, g * PEER_KEYS:(g + 1) * PEER_KEYS].T


def peer_query(x, gain, shift, scale, w):
    B, L, D = x.shape
    tm = min(512, L)
    nb = L // tm
    n = w.shape[1]
    vec = pl.BlockSpec((1, 1, D), lambda b, i: (b, 0, 0))
    return pl.pallas_call(
        _peerq_kernel,
        grid=(B, nb),
        in_specs=[pl.BlockSpec((1, tm, D), lambda b, i: (b, i, 0)),
                  pl.BlockSpec((1, D), lambda b, i: (0, 0)),
                  vec, vec,
                  pl.BlockSpec((D, n), lambda b, i: (0, 0))],
        out_specs=[pl.BlockSpec((2 * PEER_HEADS, PEER_KEYS, tm), lambda b, i: (0, 0, b * nb + i)),
                   pl.BlockSpec((tm, D), lambda b, i: (b * nb + i, 0))],
        out_shape=[jax.ShapeDtypeStruct((2 * PEER_HEADS, PEER_KEYS, B * L), F32),
                   jax.ShapeDtypeStruct((B * L, D), F32)],
        compiler_params=_cp("parallel", "parallel"),
        name="peer_query",
    )(x, gain.reshape(1, D), shift, scale, w)


def _pool_kernel(p_ref, c_ref, n_ref, wbd_ref, sc_ref, o_ref, *, L, T):
    i = pl.program_id(1)
    cur = c_ref[0]
    u3 = jnp.concatenate([p_ref[0], cur, n_ref[0]], axis=0).astype(BF16)
    t = i * T + lax.broadcasted_iota(jnp.int32, (T, 3 * T), 0)
    s = (i - 1) * T + lax.broadcasted_iota(jnp.int32, (T, 3 * T), 1)
    lane = lax.broadcasted_iota(jnp.int32, (T, POOL_WIDTH), 1)
    trow = i * T + lax.broadcasted_iota(jnp.int32, (T, POOL_WIDTH), 0)
    win = jnp.zeros((T, POOL_WIDTH), F32)
    for g, w in enumerate(POOL_WINDOWS):
        lo = jnp.maximum(t - w // 2, 0)
        hi = jnp.minimum(t + w // 2, L)
        band = jnp.where((s >= lo) & (s < hi), 1.0, 0.0).astype(BF16)
        ws = jnp.dot(band, u3, preferred_element_type=F32)
        cnt = (jnp.minimum(trow + w // 2, L) - jnp.maximum(trow - w // 2, 0)).astype(F32)
        in_group = (lane >= g * POOL_GROUP) & (lane < (g + 1) * POOL_GROUP)
        win = jnp.where(in_group, ws / cnt, win)
    d = win - cur
    y = jnp.dot(d.astype(BF16), wbd_ref[...], preferred_element_type=F32)
    o_ref[0] = y * sc_ref[...]


def pool_mixer(z_pool, wbd, scale):
    B, L, C = z_pool.shape
    T = 256
    nb = L // T
    return pl.pallas_call(
        functools.partial(_pool_kernel, L=L, T=T),
        grid=(B, nb),
        in_specs=[pl.BlockSpec((1, T, C), lambda b, i: (b, jnp.maximum(i - 1, 0), 0)),
                  pl.BlockSpec((1, T, C), lambda b, i: (b, i, 0)),
                  pl.BlockSpec((1, T, C), lambda b, i: (b, jnp.minimum(i + 1, nb - 1), 0)),
                  pl.BlockSpec((C, C), lambda b, i: (0, 0)),
                  pl.BlockSpec((1, C), lambda b, i: (0, 0))],
        out_specs=pl.BlockSpec((1, T, C), lambda b, i: (b, i, 0)),
        out_shape=jax.ShapeDtypeStruct((B, L, C), F32),
        compiler_params=_cp("parallel", "parallel"),
        name="pool_mixer",
    )(z_pool, z_pool, z_pool, wbd, scale.reshape(1, C))


def _rms(x, gain, n):
    ss = jnp.sum(x * x, axis=-1, keepdims=True) * (1.0 / n)
    return x * lax.rsqrt(ss + EPS) * gain


def _mla_kernel(z_ref, qan_ref, kvan_ref, wq_ref, wk_ref, wv_ref, pk_ref, qg_ref, kg_ref,
                cos_ref, sin_ref, q_out, k_out, v_out):
    z = z_ref[0]
    zq = z[:, :MLA_Q_RANK]
    zkv = z[:, MLA_Q_RANK:MLA_Q_RANK + MLA_KV_RANK]
    zkr = z[:, MLA_Q_RANK + MLA_KV_RANK:]
    nq = _rms(zq, qan_ref[...], MLA_Q_RANK).astype(BF16)
    nkv = _rms(zkv, kvan_ref[...], MLA_KV_RANK).astype(BF16)
    qp = jnp.dot(nq, wq_ref[...], preferred_element_type=F32)
    kp = jnp.dot(nkv, wk_ref[...], preferred_element_type=F32)
    kp = kp + jnp.dot(zkr, pk_ref[...], precision=HIGHEST, preferred_element_type=F32)
    vp = jnp.dot(nkv, wv_ref[...], preferred_element_type=F32)
    cos = cos_ref[...]
    sin = sin_ref[...]
    lane = lax.broadcasted_iota(jnp.int32, cos.shape, 1)
    first_half = lane < MLA_NOPE + MLA_ROPE // 2
    extra = (VT_ROWS - MLA_V, z.shape[0])
    ones_rows = jnp.where(lax.broadcasted_iota(jnp.int32, extra, 0) == 0, 1.0, 0.0)
    for h in range(MLA_HEADS):
        sl = slice(h * HEAD_PAD, (h + 1) * HEAD_PAD)
        for src, gain_ref, out, mult in ((qp, qg_ref, q_out, Q_PRESCALE), (kp, kg_ref, k_out, 1.0)):
            xn = _rms(src[:, sl], gain_ref[...], MLA_QK)
            partner = jnp.where(first_half, pltpu.roll(xn, HEAD_PAD - MLA_ROPE // 2, 1),
                                pltpu.roll(xn, MLA_ROPE // 2, 1))
            xr = xn * cos + partner * sin
            out[0, h] = (xr * mult).astype(BF16)
        v_out[0, h] = jnp.concatenate([vp[:, sl].T[:MLA_V, :], ones_rows], axis=0).astype(BF16)


def mla_qkv(z_mla, qan, kvan, wq, wk, wv, pk, qg, kg, cos, sin):
    B, L, _ = z_mla.shape
    T = 256
    full = lambda shape: pl.BlockSpec(shape, lambda b, i: (0,) * len(shape))
    head_out = pl.BlockSpec((1, MLA_HEADS, T, HEAD_PAD), lambda b, i: (b, 0, i, 0))
    out_sds = jax.ShapeDtypeStruct((B, MLA_HEADS, L, HEAD_PAD), BF16)
    vt_out = pl.BlockSpec((1, MLA_HEADS, VT_ROWS, T), lambda b, i: (b, 0, 0, i))
    vt_sds = jax.ShapeDtypeStruct((B, MLA_HEADS, VT_ROWS, L), BF16)
    return pl.pallas_call(
        _mla_kernel,
        grid=(B, L // T),
        in_specs=[pl.BlockSpec((1, T, Z_MLA), lambda b, i: (b, i, 0)),
                  full((1, MLA_Q_RANK)), full((1, MLA_KV_RANK)),
                  full(wq.shape), full(wk.shape), full(wv.shape), full(pk.shape),
                  full((1, HEAD_PAD)), full((1, HEAD_PAD)),
                  pl.BlockSpec((T, HEAD_PAD), lambda b, i: (i, 0)),
                  pl.BlockSpec((T, HEAD_PAD), lambda b, i: (i, 0))],
        out_specs=[head_out, head_out, vt_out],
        out_shape=[out_sds, out_sds, vt_sds],
        compiler_params=_cp("parallel", "parallel"),
        name="mla_qkv",
    )(z_mla, qan.reshape(1, -1), kvan.reshape(1, -1), wq, wk, wv, pk, qg, kg, cos, sin)


def _flash_kernel(q_ref, kc_ref, vct_ref, *rest, nchunks):
    if nchunks:
        k_ref, vt_ref, o_ref, s_s, m_s, acc_s = rest
    else:
        o_ref, s_s, m_s, acc_s = rest
    nc = kc_ref.shape[2]

    def scores(h, kblk):
        n = kblk.shape[0]
        s_s[h, :n] = lax.dot_general(kblk, q_ref[0, h], (((1,), (1,)), ((), ())), preferred_element_type=F32)

    def update(h, n, vtblk):
        st = s_s[h, :n]
        m = m_s[h]
        m_new = jnp.maximum(m, jnp.max(st, axis=0, keepdims=True))
        p = jnp.exp2(st - m_new).astype(BF16)
        m_s[h] = m_new
        acc_s[h] = jnp.exp2(m - m_new) * acc_s[h] + jnp.dot(vtblk, p, preferred_element_type=F32)

    m_s[...] = jnp.full(m_s.shape, -jnp.inf, F32)
    acc_s[...] = jnp.zeros(acc_s.shape, F32)
    scores(0, kc_ref[0, 0])
    scores(1, kc_ref[0, 1])
    update(0, nc, vct_ref[0, 0])
    if nchunks:
        scores(0, k_ref[0, 0, pl.ds(0, ATT_TC), :])
    update(1, nc, vct_ref[0, 1])
    if nchunks:
        def body(c, _):
            off = pl.multiple_of(c * ATT_TC, ATT_TC)
            nxt = pl.multiple_of(jnp.minimum(c + 1, nchunks - 1) * ATT_TC, ATT_TC)
            scores(1, k_ref[0, 1, pl.ds(off, ATT_TC), :])
            update(0, ATT_TC, vt_ref[0, 0, :, pl.ds(off, ATT_TC)])
            scores(0, k_ref[0, 0, pl.ds(nxt, ATT_TC), :])
            update(1, ATT_TC, vt_ref[0, 1, :, pl.ds(off, ATT_TC)])
            return 0
        lax.fori_loop(0, nchunks, body, 0, unroll=8)
    o = jnp.concatenate([acc_s[h, :MLA_V] / acc_s[h, MLA_V:MLA_V + 1] for h in range(2)], axis=0)
    o_ref[0] = o.T.astype(BF16)


def attention(q, kc, vct, k=None, vt=None):
    B, H, Lq, _ = q.shape
    Lc = kc.shape[2]
    tq = min(ATT_TQ, Lq)
    pair4 = lambda n, d: pl.BlockSpec((1, 2, n, d), lambda b, h, i: (b, h, 0, 0))
    in_specs = [pl.BlockSpec((1, 2, tq, HEAD_PAD), lambda b, h, i: (b, h, i, 0)),
                pair4(Lc, HEAD_PAD), pair4(VT_ROWS, Lc)]
    args = [q, kc, vct]
    nchunks = 0
    if k is not None:
        Lk = k.shape[2]
        nchunks = Lk // ATT_TC
        in_specs += [pair4(Lk, HEAD_PAD), pair4(VT_ROWS, Lk)]
        args += [k, vt]
    return pl.pallas_call(
        functools.partial(_flash_kernel, nchunks=nchunks),
        grid=(B, H // 2, Lq // tq),
        in_specs=in_specs,
        out_specs=pl.BlockSpec((1, tq, 2 * MLA_V), lambda b, h, i: (b, i, h)),
        out_shape=jax.ShapeDtypeStruct((B, Lq, H * MLA_V), BF16),
        scratch_shapes=[pltpu.VMEM((2, max(ATT_TC, Lc), tq), F32), pltpu.VMEM((2, 1, tq), F32),
                        pltpu.VMEM((2, VT_ROWS, tq), F32)],
        compiler_params=_cp("parallel", "parallel", "arbitrary"),
        name="attention",
    )(*args)


def _log_sigmoid(x):
    return jnp.minimum(x, 0.0) - jnp.log(1.0 + jnp.exp(-jnp.abs(x)))


def _mlstm_direction(d, q, k, v, g, C_s, n_s, m_s):
    Lc = q.shape[0]
    row = lax.broadcasted_iota(jnp.int32, (Lc, Lc), 0)
    col = lax.broadcasted_iota(jnp.int32, (Lc, Lc), 1)
    tri = (col <= row) if d == 0 else (col >= row)
    logf = _log_sigmoid(g)
    bcol = jnp.dot(jnp.where(tri, 1.0, 0.0), logf, precision=HIGHEST, preferred_element_type=F32)
    bT = bcol.T
    gT = g.T
    lane = lax.broadcasted_iota(jnp.int32, (1, ML_WIDTH), 1)
    kb = k.astype(BF16)
    vb = v.astype(BF16)
    Cst = C_s[d]
    nst = n_s[d]
    mst = m_s[d]
    qc = jnp.dot(q.astype(BF16), Cst.astype(BF16), preferred_element_type=F32)
    out = jnp.zeros((Lc, ML_WIDTH), F32)
    ws_all = jnp.zeros((Lc, ML_WIDTH), F32)
    wprev_all = jnp.zeros((1, ML_WIDTH), F32)
    mnew_all = jnp.zeros((1, ML_WIDTH), F32)
    for h in range(ML_HEADS):
        il = 8 * d + h
        fl = 8 * d + 4 + h
        head = (lane >= h * ML_DK) & (lane < (h + 1) * ML_DK)
        bc = bcol[:, fl:fl + 1]
        br = bT[fl:fl + 1, :]
        ir = gT[il:il + 1, :]
        ic = g[:, il:il + 1]
        mprev = mst[:, h * ML_DK:h * ML_DK + 1]
        dmat = jnp.where(tri, bc - br + ir, -jnp.inf)
        inter = bc + mprev
        mj = jnp.maximum(inter, jnp.max(dmat, axis=-1, keepdims=True))
        w_inter = jnp.exp(inter - mj)
        qh = jnp.where(head, q, 0.0)
        s = lax.dot_general(qh.astype(BF16), kb, (((1,), (1,)), ((), ())), preferred_element_type=F32)
        qk = s * jnp.exp(dmat - mj)
        pv = jnp.dot(qk.astype(BF16), vb, preferred_element_type=F32)
        qn = jnp.sum(qh * nst, axis=-1, keepdims=True)
        den = jnp.sum(qk, axis=-1, keepdims=True) + w_inter * qn
        denom = jnp.maximum(jnp.abs(den), jnp.exp(-mj))
        out = jnp.where(head, (pv + qc * w_inter) / denom, out)
        blast = bc[Lc - 1:Lc, :] if d == 0 else bc[0:1, :]
        dec = blast - bc + ic
        mnew = jnp.maximum(blast + mprev, jnp.max(dec, axis=0, keepdims=True))
        wprev = jnp.exp(blast + mprev - mnew)
        ws = jnp.exp(dec - mnew)
        ws_all = jnp.where(head, ws, ws_all)
        wprev_all = jnp.where(head, wprev, wprev_all)
        mnew_all = jnp.where(head, mnew, mnew_all)
    kw = k * ws_all
    upd = jnp.dot(kw.T.astype(BF16), vb, preferred_element_type=F32)
    r2 = lax.broadcasted_iota(jnp.int32, (ML_WIDTH, ML_WIDTH), 0) // ML_DK
    c2 = lax.broadcasted_iota(jnp.int32, (ML_WIDTH, ML_WIDTH), 1) // ML_DK
    C_s[d] = Cst * wprev_all + jnp.where(r2 == c2, upd, 0.0)
    n_s[d] = nst * wprev_all + jnp.sum(kw, axis=0, keepdims=True)
    m_s[d] = mnew_all
    return out


def _mlstm_kernel(qf_ref, kf_ref, vf_ref, gf_ref, qb_ref, kb_ref, vb_ref, gb_ref, bias_ref,
                  c0_ref, n0_ref, m0_ref, hf_ref, hb_ref, cT_ref, nT_ref, mT_ref, C_s, n_s, m_s):
    c = pl.program_id(1)

    @pl.when(c == 0)
    def _():
        C_s[...] = c0_ref[0]
        n_s[...] = n0_ref[0]
        m_s[...] = m0_ref[0]

    scale = ML_DK ** -0.5
    hf_ref[0] = _mlstm_direction(0, qf_ref[0] * scale, kf_ref[0], vf_ref[0],
                                 gf_ref[0] + bias_ref[...], C_s, n_s, m_s)
    hb_ref[0] = _mlstm_direction(1, qb_ref[0] * scale, kb_ref[0], vb_ref[0],
                                 gb_ref[0] + bias_ref[...], C_s, n_s, m_s)

    @pl.when(c == pl.num_programs(1) - 1)
    def _():
        cT_ref[0] = C_s[...]
        nT_ref[0] = n_s[...]
        mT_ref[0] = m_s[...]


def mlstm(z_ml, bias, state):
    B, L, _ = z_ml.shape
    Lc = ML_CHUNK
    nc = L // Lc
    c0, n0, m0 = state
    W = ML_WIDTH
    fwd = lambda j: pl.BlockSpec((1, Lc, W), lambda b, c: (b, c, j))
    bwd = lambda j: pl.BlockSpec((1, Lc, W), lambda b, c: (b, nc - 1 - c, j))
    gcol = 4 * W // 128
    st_c = pl.BlockSpec((1, 2, W, W), lambda b, c: (b, 0, 0, 0))
    st_v = pl.BlockSpec((1, 2, 1, W), lambda b, c: (b, 0, 0, 0))
    return pl.pallas_call(
        _mlstm_kernel,
        grid=(B, nc),
        in_specs=[fwd(0), fwd(1), fwd(2), pl.BlockSpec((1, Lc, 128), lambda b, c: (b, c, gcol)),
                  bwd(0), bwd(1), bwd(2), pl.BlockSpec((1, Lc, 128), lambda b, c: (b, nc - 1 - c, gcol)),
                  pl.BlockSpec((1, 128), lambda b, c: (0, 0)),
                  st_c, st_v, st_v],
        out_specs=[pl.BlockSpec((1, Lc, W), lambda b, c: (b, c, 0)),
                   pl.BlockSpec((1, Lc, W), lambda b, c: (b, nc - 1 - c, 0)),
                   st_c, st_v, st_v],
        out_shape=[jax.ShapeDtypeStruct((B, L, W), F32), jax.ShapeDtypeStruct((B, L, W), F32),
                   jax.ShapeDtypeStruct((B, 2, W, W), F32), jax.ShapeDtypeStruct((B, 2, 1, W), F32),
                   jax.ShapeDtypeStruct((B, 2, 1, W), F32)],
        scratch_shapes=[pltpu.VMEM((2, W, W), F32), pltpu.VMEM((2, 1, W), F32), pltpu.VMEM((2, 1, W), F32)],
        compiler_params=_cp("parallel", "arbitrary"),
        name="mlstm",
    )(z_ml, z_ml, z_ml, z_ml, z_ml, z_ml, z_ml, z_ml, bias, c0, n0, m0)


def _outproj_kernel(x_ref, pool_ref, attn_ref, hf_ref, hb_ref, op_ref, mlg_ref, hs_ref, hst_ref,
                    wp_ref, wa_ref, wm_ref, ga_ref, o_ref):
    h = hf_ref[0] + hb_ref[0]
    ss = jnp.dot(h * h, hs_ref[...], precision=HIGHEST, preferred_element_type=F32) * (1.0 / ML_DK)
    inv = jnp.dot(lax.rsqrt(ss + EPS), hst_ref[...], precision=HIGHEST, preferred_element_type=F32)
    op = op_ref[0]
    ml = h * inv * mlg_ref[...] * (1.0 / (1.0 + jnp.exp(-op)))
    mix = jnp.dot(pool_ref[0].astype(BF16), wp_ref[...], preferred_element_type=F32)
    mix += jnp.dot(attn_ref[0], wa_ref[...], preferred_element_type=F32)
    mix += jnp.dot(ml.astype(BF16), wm_ref[...], preferred_element_type=F32)
    o_ref[0] = x_ref[0] + ga_ref[0] * mix


def out_proj(x, pool, attn, hf, hb, z_ml, mlg, hs, hst, wp, wa, wm, gate):
    B, L, D = x.shape
    T = min(512, L)
    W = ML_WIDTH
    tile = lambda w, j=0: pl.BlockSpec((1, T, w), lambda b, i: (b, i, j))
    full = lambda shape: pl.BlockSpec(shape, lambda b, i: (0,) * len(shape))
    return pl.pallas_call(
        _outproj_kernel,
        grid=(B, L // T),
        in_specs=[tile(D), tile(W), tile(MLA_HEADS * MLA_V), tile(W), tile(W), tile(W, 3),
                  full((1, W)), full(hs.shape), full(hst.shape),
                  full(wp.shape), full(wa.shape), full(wm.shape),
                  pl.BlockSpec((1, 1, D), lambda b, i: (b, 0, 0))],
        out_specs=tile(D),
        out_shape=jax.ShapeDtypeStruct((B, L, D), F32),
        compiler_params=_cp("parallel", "parallel"),
        name="out_proj",
    )(x, pool, attn, hf, hb, z_ml, mlg, hs, hst, wp, wa, wm, gate)


def _top16_rows(s, row):
    big = float(s.shape[0])
    vals, idxs = [], []
    for _ in range(PEER_TOPK):
        m = jnp.max(s, axis=0, keepdims=True)
        idx = jnp.min(jnp.where(s == m, row, big), axis=0, keepdims=True)
        vals.append(m)
        idxs.append(idx)
        s = jnp.where(row == idx, -jnp.inf, s)
    return jnp.concatenate(vals, axis=0), jnp.concatenate(idxs, axis=0)


def _candidate_pairs():
    return [(k1, k2) for k1 in range(PEER_TOPK) for k2 in range(PEER_TOPK) if (k1 + 1) * (k2 + 1) <= PEER_TOPK]


def _topk_kernel(qp_ref, sk_ref, g1_ref, g2_ref, ids_ref, gate_ref):
    T = 256
    row = lax.broadcasted_iota(jnp.int32, (PEER_KEYS, T), 0).astype(F32)
    pick = lambda g_ref, a: jnp.dot(g_ref[...], a, precision=HIGHEST, preferred_element_type=F32)
    ncand = g1_ref.shape[0]
    crow = lax.broadcasted_iota(jnp.int32, (ncand, T), 0).astype(F32)

    def tile(lt, _):
        cols = pl.ds(pl.multiple_of(lt * T, T), T)
        tops = []
        for c in range(2):
            s = jnp.dot(sk_ref[c], qp_ref[c, :, cols], precision=HIGHEST, preferred_element_type=F32)
            tops.append(_top16_rows(s, row))
        cand = pick(g1_ref, tops[0][0]) + pick(g2_ref, tops[1][0])
        cand = jnp.where(crow < float(len(_candidate_pairs())), cand, -jnp.inf)
        expert = pick(g1_ref, tops[0][1]) * float(PEER_KEYS) + pick(g2_ref, tops[1][1])
        best, eids = [], []
        for _ in range(PEER_TOPK):
            m = jnp.max(cand, axis=0, keepdims=True)
            ci = jnp.min(jnp.where(cand == m, crow, float(ncand)), axis=0, keepdims=True)
            sel = crow == ci
            best.append(m)
            eids.append(jnp.sum(jnp.where(sel, expert, 0.0), axis=0, keepdims=True))
            cand = jnp.where(sel, -jnp.inf, cand)
        best = jnp.concatenate(best, axis=0)
        p = jnp.exp(best - best[0:1, :])
        gate_ref[:, cols] = p / jnp.sum(p, axis=0, keepdims=True)
        ids_ref[:, cols] = jnp.concatenate(eids, axis=0).astype(jnp.int32)
        return 0

    lax.fori_loop(0, qp_ref.shape[2] // T, tile, 0)


def peer_topk(qp_t, sk, g1, g2):
    N = qp_t.shape[2]
    T = 512
    return pl.pallas_call(
        _topk_kernel,
        grid=(N // T, PEER_HEADS),
        in_specs=[pl.BlockSpec((2, PEER_KEYS, T), lambda i, h: (h, 0, i)),
                  pl.BlockSpec((2, PEER_KEYS, PEER_KEYS), lambda i, h: (h, 0, 0)),
                  pl.BlockSpec(g1.shape, lambda i, h: (0, 0)),
                  pl.BlockSpec(g2.shape, lambda i, h: (0, 0))],
        out_specs=[pl.BlockSpec((PEER_TOPK, T), lambda i, h: (h, i)),
                   pl.BlockSpec((PEER_TOPK, T), lambda i, h: (h, i))],
        out_shape=[jax.ShapeDtypeStruct((PEER_SLOTS, N), jnp.int32),
                   jax.ShapeDtypeStruct((PEER_SLOTS, N), F32)],
        compiler_params=_cp("parallel", "parallel"),
        name="peer_topk",
    )(qp_t, sk, g1, g2)


def _unpack_rows(x):
    lo = pltpu.bitcast(x << 16, F32)
    hi = pltpu.bitcast(x & jnp.uint32(0xFFFF0000), F32)
    return jnp.concatenate([lo, hi], axis=-1).astype(BF16)


def _token_pipeline(ids_ref, tab_ref, ring, sems, consume, finish_group=None):
    TB, R = PEER_TB, PEER_RING

    def group_copy(k, half):
        first = pl.multiple_of(k * R, R)
        return pltpu.make_async_copy(ids_ref.at[0, pl.ds(first, R)], ring.at[half], sems.at[half])

    group_copy(0, 0).start()

    def body(i, _):
        for half in range(2):
            k = 2 * i + half
            group_copy(k, half).wait()
            group_copy(k + 1, 1 - half).start()
            carry = None
            for s in range(R):
                rows = [tab_ref[pl.ds(ring[half, s, j], ROW_SUB), :] for j in range(PEER_SLOTS)]
                carry = consume(k * R + s, jnp.concatenate(rows, axis=0), carry)
            if finish_group is not None:
                finish_group(carry)
        return 0

    lax.fori_loop(0, TB // (2 * R), body, 0)
    group_copy(TB // R, 0).wait()


def _peer_u_kernel(ids_ref, h_ref, tab_ref, group_ref, at_ref, ring, sems, acc_s):
    ones = jnp.ones((8, 2 * 128), BF16)

    def consume(t, rows, acc):
        h = h_ref[pl.ds(t, 1), :]
        lo = pltpu.bitcast(rows << 16, F32).reshape(-1, 8, 128)
        hi = pltpu.bitcast(rows & jnp.uint32(0xFFFF0000), F32).reshape(-1, 8, 128)
        chunk = lambda s, half: h[:, s * 256 + half * 128:s * 256 + half * 128 + 128]
        h_lo = jnp.concatenate([chunk(s, 0) for s in range(ROW_SUB)] * 2, axis=0)
        h_hi = jnp.concatenate([chunk(s, 1) for s in range(ROW_SUB)] * 2, axis=0)
        prod = jnp.concatenate([(lo * h_lo[None]).reshape(-1, 128), (hi * h_hi[None]).reshape(-1, 128)],
                               axis=-1).astype(BF16)
        sums = lax.dot_general(ones, prod, (((1,), (1,)), ((), ())), preferred_element_type=F32)
        acc_s[pl.ds(t, 1), :] = sums[0:1, :]
        return acc

    _token_pipeline(ids_ref, tab_ref, ring, sems, consume)
    a_t = lax.dot_general(acc_s[...], group_ref[...], (((1,), (1,)), ((), ())), precision=HIGHEST,
                          preferred_element_type=F32)
    at_ref[...] = a_t.T


def _peer_specs(TB):
    R = PEER_SLOTS * ROW_SUB
    ids_spec = pl.BlockSpec((1, TB + PEER_RING, PEER_SLOTS), lambda i: (i, 0, 0))
    table_spec = pl.BlockSpec(memory_space=pltpu.VMEM)
    ring = [pltpu.SMEM((2, PEER_RING, PEER_SLOTS), jnp.int32), pltpu.SemaphoreType.DMA((2,))]
    return ids_spec, table_spec, ring


def peer_scores(ids_blocks, h, tab, group):
    N = h.shape[0]
    TB = PEER_TB
    R = PEER_SLOTS * ROW_SUB
    ids_spec, table_spec, ring = _peer_specs(TB)
    return pl.pallas_call(
        _peer_u_kernel,
        grid=(N // TB,),
        in_specs=[ids_spec, pl.BlockSpec((TB, D_MODEL), lambda i: (i, 0)), table_spec,
                  pl.BlockSpec(group.shape, lambda i: (0, 0))],
        out_specs=pl.BlockSpec((PEER_SLOTS, TB), lambda i: (0, i)),
        out_shape=jax.ShapeDtypeStruct((PEER_SLOTS, N), F32),
        scratch_shapes=ring + [pltpu.VMEM((TB, R), F32)],
        compiler_params=_cp("arbitrary"),
        name="peer_scores",
    )(ids_blocks, h, tab, group)


def _peer_v_kernel(ids_ref, at_ref, gt_ref, tab_ref, rep_ref, o_ref, ring, sems, wexp):
    a = at_ref[...]
    w = gt_ref[...] * (0.5 * a * (1.0 + lax.erf(a * (1.0 / math.sqrt(2.0)))))
    wexp[...] = jnp.dot(w.T, rep_ref[...], precision=HIGHEST, preferred_element_type=F32)
    R = PEER_SLOTS * ROW_SUB
    own = (lax.broadcasted_iota(jnp.int32, (8, R), 0) ==
           lax.broadcasted_iota(jnp.int32, (8, R), 1) % ROW_SUB)

    def consume(t, rows, carry):
        x = _unpack_rows(rows)
        wsel = jnp.where(own, wexp[pl.ds(t, 1), :], 0.0).astype(BF16)
        y = jnp.dot(wsel, x, preferred_element_type=F32)
        for s in range(ROW_SUB):
            o_ref[pl.ds(t, 1), s * 256:(s + 1) * 256] = y[s:s + 1, :]
        return carry

    _token_pipeline(ids_ref, tab_ref, ring, sems, consume)


def peer_combine(ids_flat, at, gt, tab, rep):
    N = at.shape[1]
    TB = PEER_TB
    R = PEER_SLOTS * ROW_SUB
    ids_spec, table_spec, ring = _peer_specs(TB)
    slot_tile = pl.BlockSpec((PEER_SLOTS, TB), lambda i: (0, i))
    return pl.pallas_call(
        _peer_v_kernel,
        grid=(N // TB,),
        in_specs=[ids_spec, slot_tile, slot_tile, table_spec, pl.BlockSpec(rep.shape, lambda i: (0, 0))],
        out_specs=pl.BlockSpec((TB, D_MODEL), lambda i: (i, 0)),
        out_shape=jax.ShapeDtypeStruct((N, D_MODEL), F32),
        scratch_shapes=ring + [pltpu.VMEM((TB, R), F32)],
        compiler_params=_cp("arbitrary"),
        name="peer_combine",
    )(ids_flat, at, gt, tab, rep)


def _residual_kernel(x_ref, y_ref, g_ref, o_ref):
    o_ref[0] = x_ref[0] + g_ref[0] * y_ref[0]


def gated_residual(x, y, gate):
    B, L, D = x.shape
    T = min(1024, L)
    tile = pl.BlockSpec((1, T, D), lambda b, i: (b, i, 0))
    return pl.pallas_call(
        _residual_kernel,
        grid=(B, L // T),
        in_specs=[tile, tile, pl.BlockSpec((1, 1, D), lambda b, i: (b, 0, 0))],
        out_specs=tile,
        out_shape=jax.ShapeDtypeStruct((B, L, D), F32),
        compiler_params=_cp("parallel", "parallel"),
        name="gated_residual",
    )(x, y, gate)


def _pad_last(a, n):
    return jnp.pad(a, [(0, 0)] * (a.ndim - 1) + [(0, n - a.shape[-1])])


def _pack_kernel(x_ref, o_ref):
    n = x_ref.shape[0]
    bits = pltpu.bitcast(x_ref[...].astype(BF16).astype(F32), jnp.uint32)
    for s in range(ROW_SUB):
        lo = bits[:, s * 256:s * 256 + 128] >> 16
        hi = bits[:, s * 256 + 128:(s + 1) * 256] & jnp.uint32(0xFFFF0000)
        o_ref[pl.ds(s, n, stride=ROW_SUB), :] = lo | hi


def _pack_table(tab):
    E, D = tab.shape
    te = 512
    return pl.pallas_call(
        _pack_kernel,
        grid=(E // te,),
        in_specs=[pl.BlockSpec((te, D), lambda i: (i, 0))],
        out_specs=pl.BlockSpec((te * ROW_SUB, 128), lambda i: (i, 0)),
        out_shape=jax.ShapeDtypeStruct((E * ROW_SUB, 128), jnp.uint32),
        compiler_params=_cp("parallel"),
        name="pack_table",
    )(tab)


def _layer_weights(l, norm1_gain, norm2_gain, w_in, pool_w, pool_scale, mla_q_norm, mla_kv_norm, w_uq, w_ukv,
                   q_norm, k_norm, ml_gate_bias, ml_out_norm, w_out, peer_wq, peer_subkeys, peer_u, peer_v):
    parts, start = [], 0
    for size in IN_SIZES:
        parts.append(w_in[l][:, start:start + size])
        start += size
    w_in_r = jnp.concatenate([parts[0], parts[1], parts[2], _pad_last(parts[3], 128), parts[4], parts[5],
                              parts[6], parts[7], _pad_last(parts[8], 128)], axis=1).astype(BF16)
    wbd = jnp.zeros((POOL_WIDTH, POOL_WIDTH), F32)
    for g in range(len(POOL_WINDOWS)):
        sl = slice(g * POOL_GROUP, (g + 1) * POOL_GROUP)
        wbd = wbd.at[sl, sl].set(pool_w[l, g])
    wq = _pad_last(w_uq[l].reshape(MLA_Q_RANK, MLA_HEADS, MLA_QK), HEAD_PAD).reshape(MLA_Q_RANK, -1)
    ukv = w_ukv[l].reshape(MLA_KV_RANK, MLA_HEADS, MLA_NOPE + MLA_V)
    wk = _pad_last(ukv[..., :MLA_NOPE], HEAD_PAD).reshape(MLA_KV_RANK, -1)
    wv = _pad_last(ukv[..., MLA_NOPE:], HEAD_PAD).reshape(MLA_KV_RANK, -1)
    r = jnp.arange(128)[:, None]
    cidx = jnp.arange(MLA_HEADS * HEAD_PAD)[None, :]
    pk = ((r < MLA_ROPE) & (cidx % HEAD_PAD == MLA_NOPE + r)).astype(F32)
    lane = jnp.arange(ML_WIDTH)
    hs = (lane[:, None] // ML_DK == jnp.arange(128)[None, :]).astype(F32)
    wo = w_out[l]
    wa = wo[POOL_WIDTH:POOL_WIDTH + MLA_HEADS * MLA_V]
    return dict(
        n1=norm1_gain[l], n2=norm2_gain[l], w_in=w_in_r,
        wbd=wbd.astype(BF16), pool_scale=pool_scale[l],
        qan=mla_q_norm[l], kvan=mla_kv_norm[l],
        wq=wq.astype(BF16), wk=wk.astype(BF16), wv=wv.astype(BF16), pk=pk,
        qg=_pad_last(q_norm[l], HEAD_PAD).reshape(1, HEAD_PAD), kg=_pad_last(k_norm[l], HEAD_PAD).reshape(1, HEAD_PAD),
        gate_bias=_pad_last(ml_gate_bias[l], 128).reshape(1, 128),
        mlg=ml_out_norm[l].reshape(1, ML_WIDTH), hs=hs, hst=hs.T,
        wp=wo[:POOL_WIDTH].astype(BF16), wa=wa.astype(BF16),
        wm=wo[POOL_WIDTH + MLA_HEADS * MLA_V:].astype(BF16),
        peer_wq=peer_wq[l].astype(BF16),
        sk=peer_subkeys[l].reshape(2 * PEER_HEADS, PEER_KEYS, PEER_KEYS),
        u_tab=_pack_table(peer_u[l]), v_tab=_pack_table(peer_v[l]),
    )


def _peer_constants():
    R = PEER_SLOTS * ROW_SUB
    pairs = _candidate_pairs()
    ncand = -(-len(pairs) // 8) * 8
    k1 = jnp.array([p[0] for p in pairs] + [-1] * (ncand - len(pairs)))[:, None]
    k2 = jnp.array([p[1] for p in pairs] + [-1] * (ncand - len(pairs)))[:, None]
    rank = jnp.arange(PEER_TOPK)[None, :]
    g1 = (k1 == rank).astype(F32)
    g2 = (k2 == rank).astype(F32)
    rep = (jnp.arange(PEER_SLOTS)[:, None] == jnp.arange(R)[None, :] // ROW_SUB).astype(F32)
    return g1, g2, rep


def _rope_tables(n):
    n_rows = n // GRID_W
    row = jnp.repeat(jnp.arange(n_rows), GRID_W, total_repeat_length=n).astype(F32)
    col = (jnp.arange(n) % GRID_W).astype(F32)
    per_axis = MLA_ROPE // 2
    freqs = ROPE_BASE ** (-jnp.arange(0, per_axis, 2, dtype=F32) / per_axis)
    ang = jnp.concatenate([row[:, None] * freqs, col[:, None] * freqs], axis=-1)
    c, s = jnp.cos(ang), jnp.sin(ang)
    cos = jnp.concatenate([jnp.ones((n, MLA_NOPE), F32), c, c, jnp.ones((n, HEAD_PAD - MLA_QK), F32)], axis=-1)
    sin = jnp.concatenate([jnp.zeros((n, MLA_NOPE), F32), -s, s, jnp.zeros((n, HEAD_PAD - MLA_QK), F32)], axis=-1)
    return cos, sin


def _staged_ids(offs):
    blk = offs.reshape(-1, PEER_TB, PEER_SLOTS)
    spare = jnp.broadcast_to(blk[:, :1], (blk.shape[0], PEER_RING, PEER_SLOTS))
    return jnp.concatenate([blk, spare], axis=1)


def _peer_ffn(x, w, shift, scale, gate, consts):
    B, L, D = x.shape
    g1, g2, rep = consts
    qp_t, h = peer_query(x, w["n2"], shift, scale, w["peer_wq"])
    ids_t, gates_t = peer_topk(qp_t, w["sk"], g1, g2)
    ids_blocks = _staged_ids(ids_t.T * ROW_SUB)
    a_t = peer_scores(ids_blocks, h, w["u_tab"], rep)
    y = peer_combine(ids_blocks, a_t, gates_t, w["v_tab"], rep)
    return gated_residual(x, y.reshape(B, L, D), gate)


def kernel(x, c, ctx, c_ctx, norm1_gain, norm2_gain, w_ada, b_ada, w_in, pool_w, pool_scale, mla_q_norm,
           mla_kv_norm, w_uq, w_ukv, q_norm, k_norm, ml_gate_bias, ml_out_norm, w_out, peer_wq, peer_subkeys,
           peer_u, peer_v):
    B, S, D = x.shape
    Lc_ctx = ctx.shape[1]
    depth = w_in.shape[0]
    cos_x, sin_x = _rope_tables(S)
    cos_c = jnp.ones((Lc_ctx, HEAD_PAD), F32)
    sin_c = jnp.zeros((Lc_ctx, HEAD_PAD), F32)
    consts = _peer_constants()
    cond8 = jnp.zeros((8, D), F32).at[:B].set(c).at[B].set(c_ctx)
    W = ML_WIDTH
    zero_state = (jnp.zeros((B, 2, W, W), F32), jnp.zeros((B, 2, 1, W), F32),
                  jnp.full((B, 2, 1, W), NEG_INIT, F32))
    x_ctx = ctx
    for l in range(depth):
        last = l == depth - 1
        w = _layer_weights(l, norm1_gain, norm2_gain, w_in, pool_w, pool_scale, mla_q_norm, mla_kv_norm, w_uq,
                           w_ukv, q_norm, k_norm, ml_gate_bias, ml_out_norm, w_out, peer_wq, peer_subkeys,
                           peer_u, peer_v)
        mods = ada_mod(cond8, w_ada[l], b_ada[l])
        m_x = [mods[:B, i * D:(i + 1) * D].reshape(B, 1, D) for i in range(6)]
        m_c = [jnp.broadcast_to(mods[B, i * D:(i + 1) * D].reshape(1, 1, D), (B, 1, D)) for i in range(6)]

        zp_c, zm_c, zl_c = in_proj(x_ctx, w["n1"], m_c[0], m_c[1], w["w_in"])
        qc, kc, vc = mla_qkv(zm_c, w["qan"], w["kvan"], w["wq"], w["wk"], w["wv"], w["pk"], w["qg"], w["kg"],
                             cos_c, sin_c)
        hf_c, hb_c, cT, nT, mT = mlstm(zl_c, w["gate_bias"], zero_state)
        if not last:
            pool_c = pool_mixer(zp_c, w["wbd"], w["pool_scale"])
            attn_c = attention(qc, kc, vc)
            xc = out_proj(x_ctx, pool_c, attn_c, hf_c, hb_c, zl_c, w["mlg"], w["hs"], w["hst"],
                          w["wp"], w["wa"], w["wm"], m_c[2])
            x_ctx_new = _peer_ffn(xc, w, m_c[3], m_c[4], m_c[5], consts)

        zp, zm, zl = in_proj(x, w["n1"], m_x[0], m_x[1], w["w_in"])
        pool_x = pool_mixer(zp, w["wbd"], w["pool_scale"])
        q, k, v = mla_qkv(zm, w["qan"], w["kvan"], w["wq"], w["wk"], w["wv"], w["pk"], w["qg"], w["kg"],
                          cos_x, sin_x)
        attn_x = attention(q, kc, vc, k, v)
        hf, hb, _, _, _ = mlstm(zl, w["gate_bias"], (cT, nT, mT))
        x = out_proj(x, pool_x, attn_x, hf, hb, zl, w["mlg"], w["hs"], w["hst"],
                     w["wp"], w["wa"], w["wm"], m_x[2])
        x = _peer_ffn(x, w, m_x[3], m_x[4], m_x[5], consts)
        if not last:
            x_ctx = x_ctx_new
    return x
```

```python
import functools
import math

import jax
import jax.numpy as jnp
from jax import lax
from jax.experimental import pallas as pl
from jax.experimental.pallas import tpu as pltpu

F32 = jnp.float32
BF16 = jnp.bfloat16
HIGHEST = lax.Precision.HIGHEST

EPS = 1e-6
D_MODEL = 1024
GRID_W = 64
ROPE_BASE = 10000.0
POOL_WINDOWS = (2, 4, 8, 16)
POOL_GROUP = 64
POOL_WIDTH = 256
MLA_HEADS = 8
MLA_NOPE = 64
MLA_ROPE = 32
MLA_QK = 96
MLA_V = 64
MLA_Q_RANK = 384
MLA_KV_RANK = 256
MLA_SCALE = MLA_QK ** -0.5
Q_PRESCALE = MLA_SCALE * math.log2(math.e)
HEAD_PAD = 128
ATT_TQ = 512
ATT_TC = 512
VT_ROWS = 80
ML_HEADS = 4
ML_DK = 64
ML_WIDTH = 256
ML_CHUNK = 128
NEG_INIT = -1e30
IN_SIZES = (256, 384, 256, 32, 256, 256, 256, 256, 16)
PEER_HEADS = 8
PEER_KEYS = 128
PEER_TOPK = 16
PEER_SLOTS = PEER_HEADS * PEER_TOPK
PEER_TB = 256
PEER_RING = 8
ROW_SUB = 4

Z_POOL, Z_MLA, Z_ML = 256, 768, 1152
VMEM_LIMIT = 56 * 1024 * 1024


def _cp(*sem, vmem=VMEM_LIMIT):
    return pltpu.CompilerParams(dimension_semantics=sem, vmem_limit_bytes=vmem)


def _ada_kernel(c_ref, w_ref, b_ref, o_ref):
    c = c_ref[...]
    s = c * (1.0 / (1.0 + jnp.exp(-c)))
    o_ref[...] = jnp.dot(s, w_ref[...], precision=HIGHEST, preferred_element_type=F32) + b_ref[...]


def ada_mod(cond8, w, b):
    n = w.shape[1]
    tn = n // 4
    return pl.pallas_call(
        _ada_kernel,
        grid=(n // tn,),
        in_specs=[pl.BlockSpec((8, D_MODEL), lambda j: (0, 0)),
                  pl.BlockSpec((D_MODEL, tn), lambda j: (0, j)),
                  pl.BlockSpec((1, tn), lambda j: (0, j))],
        out_specs=pl.BlockSpec((8, tn), lambda j: (0, j)),
        out_shape=jax.ShapeDtypeStruct((8, n), F32),
        compiler_params=_cp("arbitrary"),
        name="ada_mod",
    )(cond8, w, b.reshape(1, n))


def _modulated(x, gain, shift, scale):
    ms = jnp.mean(x * x, axis=-1, keepdims=True)
    y = x * lax.rsqrt(ms + EPS) * gain
    return y * (1.0 + scale) + shift


def _inproj_kernel(x_ref, gain_ref, shift_ref, scale_ref, w_ref, zp_ref, zm_ref, zl_ref):
    h = _modulated(x_ref[0], gain_ref[...], shift_ref[0], scale_ref[0])
    res = jnp.dot(h.astype(BF16), w_ref[...], preferred_element_type=F32)
    zp_ref[0] = res[:, :Z_POOL]
    zm_ref[0] = res[:, Z_POOL:Z_POOL + Z_MLA]
    zl_ref[0] = res[:, Z_POOL + Z_MLA:]


def in_proj(x, gain, shift, scale, w):
    B, L, D = x.shape
    tm = min(512, L)
    n = w.shape[1]
    vec = pl.BlockSpec((1, 1, D), lambda b, i: (b, 0, 0))
    return pl.pallas_call(
        _inproj_kernel,
        grid=(B, L // tm),
        in_specs=[pl.BlockSpec((1, tm, D), lambda b, i: (b, i, 0)),
                  pl.BlockSpec((1, D), lambda b, i: (0, 0)),
                  vec, vec,
                  pl.BlockSpec((D, n), lambda b, i: (0, 0))],
        out_specs=[pl.BlockSpec((1, tm, Z_POOL), lambda b, i: (b, i, 0)),
                   pl.BlockSpec((1, tm, Z_MLA), lambda b, i: (b, i, 0)),
                   pl.BlockSpec((1, tm, Z_ML), lambda b, i: (b, i, 0))],
        out_shape=[jax.ShapeDtypeStruct((B, L, Z_POOL), F32),
                   jax.ShapeDtypeStruct((B, L, Z_MLA), F32),
                   jax.ShapeDtypeStruct((B, L, Z_ML), F32)],
        compiler_params=_cp("parallel", "parallel"),
        name="in_proj",
    )(x, gain.reshape(1, D), shift, scale, w)


def _peerq_kernel(x_ref, gain_ref, shift_ref, scale_ref, w_ref, qp_ref, h_ref):
    h = _modulated(x_ref[0], gain_ref[...], shift_ref[0], scale_ref[0])
    hb = h.astype(BF16)
    h_ref[...] = hb.astype(F32)
    res = jnp.dot(hb, w_ref[...], preferred_element_type=F32)
    for g in range(2 * PEER_HEADS):
        qp_ref[g] = res[:, g * PEER_KEYS:(g + 1) * PEER_KEYS].T


def peer_query(x, gain, shift, scale, w):
    B, L, D = x.shape
    tm = min(512, L)
    nb = L // tm
    n = w.shape[1]
    vec = pl.BlockSpec((1, 1, D), lambda b, i: (b, 0, 0))
    return pl.pallas_call(
        _peerq_kernel,
        grid=(B, nb),
        in_specs=[pl.BlockSpec((1, tm, D), lambda b, i: (b, i, 0)),
                  pl.BlockSpec((1, D), lambda b, i: (0, 0)),
                  vec, vec,
                  pl.BlockSpec((D, n), lambda b, i: (0, 0))],
        out_specs=[pl.BlockSpec((2 * PEER_HEADS, PEER_KEYS, tm), lambda b, i: (0, 0, b * nb + i)),
                   pl.BlockSpec((tm, D), lambda b, i: (b * nb + i, 0))],
        out_shape=[jax.ShapeDtypeStruct((2 * PEER_HEADS, PEER_KEYS, B * L), F32),
                   jax.ShapeDtypeStruct((B * L, D), F32)],
        compiler_params=_cp("parallel", "parallel"),
        name="peer_query",
    )(x, gain.reshape(1, D), shift, scale, w)


def _pool_kernel(p_ref, c_ref, n_ref, wbd_ref, sc_ref, o_ref, *, L, T):
    i = pl.program_id(1)
    cur = c_ref[0]
    u3 = jnp.concatenate([p_ref[0], cur, n_ref[0]], axis=0).astype(BF16)
    t = i * T + lax.broadcasted_iota(jnp.int32, (T, 3 * T), 0)
    s = (i - 1) * T + lax.broadcasted_iota(jnp.int32, (T, 3 * T), 1)
    lane = lax.broadcasted_iota(jnp.int32, (T, POOL_WIDTH), 1)
    trow = i * T + lax.broadcasted_iota(jnp.int32, (T, POOL_WIDTH), 0)
    win = jnp.zeros((T, POOL_WIDTH), F32)
    for g, w in enumerate(POOL_WINDOWS):
        lo = jnp.maximum(t - w // 2, 0)
        hi = jnp.minimum(t + w // 2, L)
        band = jnp.where((s >= lo) & (s < hi), 1.0, 0.0).astype(BF16)
        ws = jnp.dot(band, u3, preferred_element_type=F32)
        cnt = (jnp.minimum(trow + w // 2, L) - jnp.maximum(trow - w // 2, 0)).astype(F32)
        in_group = (lane >= g * POOL_GROUP) & (lane < (g + 1) * POOL_GROUP)
        win = jnp.where(in_group, ws / cnt, win)
    d = win - cur
    y = jnp.dot(d.astype(BF16), wbd_ref[...], preferred_element_type=F32)
    o_ref[0] = y * sc_ref[...]


def pool_mixer(z_pool, wbd, scale):
    B, L, C = z_pool.shape
    T = 256
    nb = L // T
    return pl.pallas_call(
        functools.partial(_pool_kernel, L=L, T=T),
        grid=(B, nb),
        in_specs=[pl.BlockSpec((1, T, C), lambda b, i: (b, jnp.maximum(i - 1, 0), 0)),
                  pl.BlockSpec((1, T, C), lambda b, i: (b, i, 0)),
                  pl.BlockSpec((1, T, C), lambda b, i: (b, jnp.minimum(i + 1, nb - 1), 0)),
                  pl.BlockSpec((C, C), lambda b, i: (0, 0)),
                  pl.BlockSpec((1, C), lambda b, i: (0, 0))],
        out_specs=pl.BlockSpec((1, T, C), lambda b, i: (b, i, 0)),
        out_shape=jax.ShapeDtypeStruct((B, L, C), F32),
        compiler_params=_cp("parallel", "parallel"),
        name="pool_mixer",
    )(z_pool, z_pool, z_pool, wbd, scale.reshape(1, C))


def _rms(x, gain, n):
    ss = jnp.sum(x * x, axis=-1, keepdims=True) * (1.0 / n)
    return x * lax.rsqrt(ss + EPS) * gain


def _mla_kernel(z_ref, qan_ref, kvan_ref, wq_ref, wk_ref, wv_ref, pk_ref, qg_ref, kg_ref,
                cos_ref, sin_ref, q_out, k_out, v_out):
    z = z_ref[0]
    zq = z[:, :MLA_Q_RANK]
    zkv = z[:, MLA_Q_RANK:MLA_Q_RANK + MLA_KV_RANK]
    zkr = z[:, MLA_Q_RANK + MLA_KV_RANK:]
    nq = _rms(zq, qan_ref[...], MLA_Q_RANK).astype(BF16)
    nkv = _rms(zkv, kvan_ref[...], MLA_KV_RANK).astype(BF16)
    qp = jnp.dot(nq, wq_ref[...], preferred_element_type=F32)
    kp = jnp.dot(nkv, wk_ref[...], preferred_element_type=F32)
    kp = kp + jnp.dot(zkr, pk_ref[...], precision=HIGHEST, preferred_element_type=F32)
    vp = jnp.dot(nkv, wv_ref[...], preferred_element_type=F32)
    cos = cos_ref[...]
    sin = sin_ref[...]
    lane = lax.broadcasted_iota(jnp.int32, cos.shape, 1)
    first_half = lane < MLA_NOPE + MLA_ROPE // 2
    extra = (VT_ROWS - MLA_V, z.shape[0])
    ones_rows = jnp.where(lax.broadcasted_iota(jnp.int32, extra, 0) == 0, 1.0, 0.0)
    for h in range(MLA_HEADS):
        sl = slice(h * HEAD_PAD, (h + 1) * HEAD_PAD)
        for src, gain_ref, out, mult in ((qp, qg_ref, q_out, Q_PRESCALE), (kp, kg_ref, k_out, 1.0)):
            xn = _rms(src[:, sl], gain_ref[...], MLA_QK)
            partner = jnp.where(first_half, pltpu.roll(xn, HEAD_PAD - MLA_ROPE // 2, 1),
                                pltpu.roll(xn, MLA_ROPE // 2, 1))
            xr = xn * cos + partner * sin
            out[0, h] = (xr * mult).astype(BF16)
        v_out[0, h] = jnp.concatenate([vp[:, sl].T[:MLA_V, :], ones_rows], axis=0).astype(BF16)


def mla_qkv(z_mla, qan, kvan, wq, wk, wv, pk, qg, kg, cos, sin):
    B, L, _ = z_mla.shape
    T = 256
    full = lambda shape: pl.BlockSpec(shape, lambda b, i: (0,) * len(shape))
    head_out = pl.BlockSpec((1, MLA_HEADS, T, HEAD_PAD), lambda b, i: (b, 0, i, 0))
    out_sds = jax.ShapeDtypeStruct((B, MLA_HEADS, L, HEAD_PAD), BF16)
    vt_out = pl.BlockSpec((1, MLA_HEADS, VT_ROWS, T), lambda b, i: (b, 0, 0, i))
    vt_sds = jax.ShapeDtypeStruct((B, MLA_HEADS, VT_ROWS, L), BF16)
    return pl.pallas_call(
        _mla_kernel,
        grid=(B, L // T),
        in_specs=[pl.BlockSpec((1, T, Z_MLA), lambda b, i: (b, i, 0)),
                  full((1, MLA_Q_RANK)), full((1, MLA_KV_RANK)),
                  full(wq.shape), full(wk.shape), full(wv.shape), full(pk.shape),
                  full((1, HEAD_PAD)), full((1, HEAD_PAD)),
                  pl.BlockSpec((T, HEAD_PAD), lambda b, i: (i, 0)),
                  pl.BlockSpec((T, HEAD_PAD), lambda b, i: (i, 0))],
        out_specs=[head_out, head_out, vt_out],
        out_shape=[out_sds, out_sds, vt_sds],
        compiler_params=_cp("parallel", "parallel"),
        name="mla_qkv",
    )(z_mla, qan.reshape(1, -1), kvan.reshape(1, -1), wq, wk, wv, pk, qg, kg, cos, sin)


def _flash_kernel(q_ref, kc_ref, vct_ref, *rest, nchunks):
    if nchunks:
        k_ref, vt_ref, o_ref, s_s, m_s, acc_s = rest
    else:
        o_ref, s_s, m_s, acc_s = rest
    nc = kc_ref.shape[2]

    def scores(h, kblk):
        n = kblk.shape[0]
        s_s[h, :n] = lax.dot_general(kblk, q_ref[0, h], (((1,), (1,)), ((), ())), preferred_element_type=F32)

    def update(h, n, vtblk):
        st = s_s[h, :n]
        m = m_s[h]
        m_new = jnp.maximum(m, jnp.max(st, axis=0, keepdims=True))
        p = jnp.exp2(st - m_new).astype(BF16)
        m_s[h] = m_new
        acc_s[h] = jnp.exp2(m - m_new) * acc_s[h] + jnp.dot(vtblk, p, preferred_element_type=F32)

    m_s[...] = jnp.full(m_s.shape, -jnp.inf, F32)
    acc_s[...] = jnp.zeros(acc_s.shape, F32)
    scores(0, kc_ref[0, 0])
    scores(1, kc_ref[0, 1])
    update(0, nc, vct_ref[0, 0])
    if nchunks:
        scores(0, k_ref[0, 0, pl.ds(0, ATT_TC), :])
    update(1, nc, vct_ref[0, 1])
    if nchunks:
        def body(c, _):
            off = pl.multiple_of(c * ATT_TC, ATT_TC)
            nxt = pl.multiple_of(jnp.minimum(c + 1, nchunks - 1) * ATT_TC, ATT_TC)
            scores(1, k_ref[0, 1, pl.ds(off, ATT_TC), :])
            update(0, ATT_TC, vt_ref[0, 0, :, pl.ds(off, ATT_TC)])
            scores(0, k_ref[0, 0, pl.ds(nxt, ATT_TC), :])
            update(1, ATT_TC, vt_ref[0, 1, :, pl.ds(off, ATT_TC)])
            return 0
        lax.fori_loop(0, nchunks, body, 0, unroll=8)
    o = jnp.concatenate([acc_s[h, :MLA_V] / acc_s[h, MLA_V:MLA_V + 1] for h in range(2)], axis=0)
    o_ref[0] = o.T.astype(BF16)


def attention(q, kc, vct, k=None, vt=None):
    B, H, Lq, _ = q.shape
    Lc = kc.shape[2]
    tq = min(ATT_TQ, Lq)
    pair4 = lambda n, d: pl.BlockSpec((1, 2, n, d), lambda b, h, i: (b, h, 0, 0))
    in_specs = [pl.BlockSpec((1, 2, tq, HEAD_PAD), lambda b, h, i: (b, h, i, 0)),
                pair4(Lc, HEAD_PAD), pair4(VT_ROWS, Lc)]
    args = [q, kc, vct]
    nchunks = 0
    if k is not None:
        Lk = k.shape[2]
        nchunks = Lk // ATT_TC
        in_specs += [pair4(Lk, HEAD_PAD), pair4(VT_ROWS, Lk)]
        args += [k, vt]
    return pl.pallas_call(
        functools.partial(_flash_kernel, nchunks=nchunks),
        grid=(B, H // 2, Lq // tq),
        in_specs=in_specs,
        out_specs=pl.BlockSpec((1, tq, 2 * MLA_V), lambda b, h, i: (b, i, h)),
        out_shape=jax.ShapeDtypeStruct((B, Lq, H * MLA_V), BF16),
        scratch_shapes=[pltpu.VMEM((2, max(ATT_TC, Lc), tq), F32), pltpu.VMEM((2, 1, tq), F32),
                        pltpu.VMEM((2, VT_ROWS, tq), F32)],
        compiler_params=_cp("parallel", "parallel", "arbitrary"),
        name="attention",
    )(*args)


def _log_sigmoid(x):
    return jnp.minimum(x, 0.0) - jnp.log(1.0 + jnp.exp(-jnp.abs(x)))


def _mlstm_direction(d, q, k, v, g, C_s, n_s, m_s):
    Lc = q.shape[0]
    row = lax.broadcasted_iota(jnp.int32, (Lc, Lc), 0)
    col = lax.broadcasted_iota(jnp.int32, (Lc, Lc), 1)
    tri = (col <= row) if d == 0 else (col >= row)
    logf = _log_sigmoid(g)
    bcol = jnp.dot(jnp.where(tri, 1.0, 0.0), logf, precision=HIGHEST, preferred_element_type=F32)
    bT = bcol.T
    gT = g.T
    lane = lax.broadcasted_iota(jnp.int32, (1, ML_WIDTH), 1)
    kb = k.astype(BF16)
    vb = v.astype(BF16)
    Cst = C_s[d]
    nst = n_s[d]
    mst = m_s[d]
    qc = jnp.dot(q.astype(BF16), Cst.astype(BF16), preferred_element_type=F32)
    out = jnp.zeros((Lc, ML_WIDTH), F32)
    ws_all = jnp.zeros((Lc, ML_WIDTH), F32)
    wprev_all = jnp.zeros((1, ML_WIDTH), F32)
    mnew_all = jnp.zeros((1, ML_WIDTH), F32)
    for h in range(ML_HEADS):
        il = 8 * d + h
        fl = 8 * d + 4 + h
        head = (lane >= h * ML_DK) & (lane < (h + 1) * ML_DK)
        bc = bcol[:, fl:fl + 1]
        br = bT[fl:fl + 1, :]
        ir = gT[il:il + 1, :]
        ic = g[:, il:il + 1]
        mprev = mst[:, h * ML_DK:h * ML_DK + 1]
        dmat = jnp.where(tri, bc - br + ir, -jnp.inf)
        inter = bc + mprev
        mj = jnp.maximum(inter, jnp.max(dmat, axis=-1, keepdims=True))
        w_inter = jnp.exp(inter - mj)
        qh = jnp.where(head, q, 0.0)
        s = lax.dot_general(qh.astype(BF16), kb, (((1,), (1,)), ((), ())), preferred_element_type=F32)
        qk = s * jnp.exp(dmat - mj)
        pv = jnp.dot(qk.astype(BF16), vb, preferred_element_type=F32)
        qn = jnp.sum(qh * nst, axis=-1, keepdims=True)
        den = jnp.sum(qk, axis=-1, keepdims=True) + w_inter * qn
        denom = jnp.maximum(jnp.abs(den), jnp.exp(-mj))
        out = jnp.where(head, (pv + qc * w_inter) / denom, out)
        blast = bc[Lc - 1:Lc, :] if d == 0 else bc[0:1, :]
        dec = blast - bc + ic
        mnew = jnp.maximum(blast + mprev, jnp.max(dec, axis=0, keepdims=True))
        wprev = jnp.exp(blast + mprev - mnew)
        ws = jnp.exp(dec - mnew)
        ws_all = jnp.where(head, ws, ws_all)
        wprev_all = jnp.where(head, wprev, wprev_all)
        mnew_all = jnp.where(head, mnew, mnew_all)
    kw = k * ws_all
    upd = jnp.dot(kw.T.astype(BF16), vb, preferred_element_type=F32)
    r2 = lax.broadcasted_iota(jnp.int32, (ML_WIDTH, ML_WIDTH), 0) // ML_DK
    c2 = lax.broadcasted_iota(jnp.int32, (ML_WIDTH, ML_WIDTH), 1) // ML_DK
    C_s[d] = Cst * wprev_all + jnp.where(r2 == c2, upd, 0.0)
    n_s[d] = nst * wprev_all + jnp.sum(kw, axis=0, keepdims=True)
    m_s[d] = mnew_all
    return out


def _mlstm_kernel(qf_ref, kf_ref, vf_ref, gf_ref, qb_ref, kb_ref, vb_ref, gb_ref, bias_ref,
                  c0_ref, n0_ref, m0_ref, hf_ref, hb_ref, cT_ref, nT_ref, mT_ref, C_s, n_s, m_s):
    c = pl.program_id(1)

    @pl.when(c == 0)
    def _():
        C_s[...] = c0_ref[0]
        n_s[...] = n0_ref[0]
        m_s[...] = m0_ref[0]

    scale = ML_DK ** -0.5
    hf_ref[0] = _mlstm_direction(0, qf_ref[0] * scale, kf_ref[0], vf_ref[0],
                                 gf_ref[0] + bias_ref[...], C_s, n_s, m_s)
    hb_ref[0] = _mlstm_direction(1, qb_ref[0] * scale, kb_ref[0], vb_ref[0],
                                 gb_ref[0] + bias_ref[...], C_s, n_s, m_s)

    @pl.when(c == pl.num_programs(1) - 1)
    def _():
        cT_ref[0] = C_s[...]
        nT_ref[0] = n_s[...]
        mT_ref[0] = m_s[...]


def mlstm(z_ml, bias, state):
    B, L, _ = z_ml.shape
    Lc = ML_CHUNK
    nc = L // Lc
    c0, n0, m0 = state
    W = ML_WIDTH
    fwd = lambda j: pl.BlockSpec((1, Lc, W), lambda b, c: (b, c, j))
    bwd = lambda j: pl.BlockSpec((1, Lc, W), lambda b, c: (b, nc - 1 - c, j))
    gcol = 4 * W // 128
    st_c = pl.BlockSpec((1, 2, W, W), lambda b, c: (b, 0, 0, 0))
    st_v = pl.BlockSpec((1, 2, 1, W), lambda b, c: (b, 0, 0, 0))
    return pl.pallas_call(
        _mlstm_kernel,
        grid=(B, nc),
        in_specs=[fwd(0), fwd(1), fwd(2), pl.BlockSpec((1, Lc, 128), lambda b, c: (b, c, gcol)),
                  bwd(0), bwd(1), bwd(2), pl.BlockSpec((1, Lc, 128), lambda b, c: (b, nc - 1 - c, gcol)),
                  pl.BlockSpec((1, 128), lambda b, c: (0, 0)),
                  st_c, st_v, st_v],
        out_specs=[pl.BlockSpec((1, Lc, W), lambda b, c: (b, c, 0)),
                   pl.BlockSpec((1, Lc, W), lambda b, c: (b, nc - 1 - c, 0)),
                   st_c, st_v, st_v],
        out_shape=[jax.ShapeDtypeStruct((B, L, W), F32), jax.ShapeDtypeStruct((B, L, W), F32),
                   jax.ShapeDtypeStruct((B, 2, W, W), F32), jax.ShapeDtypeStruct((B, 2, 1, W), F32),
                   jax.ShapeDtypeStruct((B, 2, 1, W), F32)],
        scratch_shapes=[pltpu.VMEM((2, W, W), F32), pltpu.VMEM((2, 1, W), F32), pltpu.VMEM((2, 1, W), F32)],
        compiler_params=_cp("parallel", "arbitrary"),
        name="mlstm",
    )(z_ml, z_ml, z_ml, z_ml, z_ml, z_ml, z_ml, z_ml, bias, c0, n0, m0)


def _outproj_kernel(x_ref, pool_ref, attn_ref, hf_ref, hb_ref, op_ref, mlg_ref, hs_ref, hst_ref,
                    wp_ref, wa_ref, wm_ref, ga_ref, o_ref):
    h = hf_ref[0] + hb_ref[0]
    ss = jnp.dot(h * h, hs_ref[...], precision=HIGHEST, preferred_element_type=F32) * (1.0 / ML_DK)
    inv = jnp.dot(lax.rsqrt(ss + EPS), hst_ref[...], precision=HIGHEST, preferred_element_type=F32)
    op = op_ref[0]
    ml = h * inv * mlg_ref[...] * (1.0 / (1.0 + jnp.exp(-op)))
    mix = jnp.dot(pool_ref[0].astype(BF16), wp_ref[...], preferred_element_type=F32)
    mix += jnp.dot(attn_ref[0], wa_ref[...], preferred_element_type=F32)
    mix += jnp.dot(ml.astype(BF16), wm_ref[...], preferred_element_type=F32)
    o_ref[0] = x_ref[0] + ga_ref[0] * mix


def out_proj(x, pool, attn, hf, hb, z_ml, mlg, hs, hst, wp, wa, wm, gate):
    B, L, D = x.shape
    T = min(512, L)
    W = ML_WIDTH
    tile = lambda w, j=0: pl.BlockSpec((1, T, w), lambda b, i: (b, i, j))
    full = lambda shape: pl.BlockSpec(shape, lambda b, i: (0,) * len(shape))
    return pl.pallas_call(
        _outproj_kernel,
        grid=(B, L // T),
        in_specs=[tile(D), tile(W), tile(MLA_HEADS * MLA_V), tile(W), tile(W), tile(W, 3),
                  full((1, W)), full(hs.shape), full(hst.shape),
                  full(wp.shape), full(wa.shape), full(wm.shape),
                  pl.BlockSpec((1, 1, D), lambda b, i: (b, 0, 0))],
        out_specs=tile(D),
        out_shape=jax.ShapeDtypeStruct((B, L, D), F32),
        compiler_params=_cp("parallel", "parallel"),
        name="out_proj",
    )(x, pool, attn, hf, hb, z_ml, mlg, hs, hst, wp, wa, wm, gate)


def _top16_rows(s, row):
    big = float(s.shape[0])
    vals, idxs = [], []
    for _ in range(PEER_TOPK):
        m = jnp.max(s, axis=0, keepdims=True)
        idx = jnp.min(jnp.where(s == m, row, big), axis=0, keepdims=True)
        vals.append(m)
        idxs.append(idx)
        s = jnp.where(row == idx, -jnp.inf, s)
    return jnp.concatenate(vals, axis=0), jnp.concatenate(idxs, axis=0)


def _candidate_pairs():
    return [(k1, k2) for k1 in range(PEER_TOPK) for k2 in range(PEER_TOPK) if (k1 + 1) * (k2 + 1) <= PEER_TOPK]


def _topk_kernel(qp_ref, sk_ref, g1_ref, g2_ref, ids_ref, gate_ref):
    T = 256
    row = lax.broadcasted_iota(jnp.int32, (PEER_KEYS, T), 0).astype(F32)
    pick = lambda g_ref, a: jnp.dot(g_ref[...], a, precision=HIGHEST, preferred_element_type=F32)
    ncand = g1_ref.shape[0]
    crow = lax.broadcasted_iota(jnp.int32, (ncand, T), 0).astype(F32)

    def tile(lt, _):
        cols = pl.ds(pl.multiple_of(lt * T, T), T)
        tops = []
        for c in range(2):
            s = jnp.dot(sk_ref[c], qp_ref[c, :, cols], precision=HIGHEST, preferred_element_type=F32)
            tops.append(_top16_rows(s, row))
        cand = pick(g1_ref, tops[0][0]) + pick(g2_ref, tops[1][0])
        cand = jnp.where(crow < float(len(_candidate_pairs())), cand, -jnp.inf)
        expert = pick(g1_ref, tops[0][1]) * float(PEER_KEYS) + pick(g2_ref, tops[1][1])
        best, eids = [], []
        for _ in range(PEER_TOPK):
            m = jnp.max(cand, axis=0, keepdims=True)
            ci = jnp.min(jnp.where(cand == m, crow, float(ncand)), axis=0, keepdims=True)
            sel = crow == ci
            best.append(m)
            eids.append(jnp.sum(jnp.where(sel, expert, 0.0), axis=0, keepdims=True))
            cand = jnp.where(sel, -jnp.inf, cand)
        best = jnp.concatenate(best, axis=0)
        p = jnp.exp(best - best[0:1, :])
        gate_ref[:, cols] = p / jnp.sum(p, axis=0, keepdims=True)
        ids_ref[:, cols] = jnp.concatenate(eids, axis=0).astype(jnp.int32)
        return 0

    lax.fori_loop(0, qp_ref.shape[2] // T, tile, 0)


def peer_topk(qp_t, sk, g1, g2):
    N = qp_t.shape[2]
    T = 512
    return pl.pallas_call(
        _topk_kernel,
        grid=(N // T, PEER_HEADS),
        in_specs=[pl.BlockSpec((2, PEER_KEYS, T), lambda i, h: (h, 0, i)),
                  pl.BlockSpec((2, PEER_KEYS, PEER_KEYS), lambda i, h: (h, 0, 0)),
                  pl.BlockSpec(g1.shape, lambda i, h: (0, 0)),
                  pl.BlockSpec(g2.shape, lambda i, h: (0, 0))],
        out_specs=[pl.BlockSpec((PEER_TOPK, T), lambda i, h: (h, i)),
                   pl.BlockSpec((PEER_TOPK, T), lambda i, h: (h, i))],
        out_shape=[jax.ShapeDtypeStruct((PEER_SLOTS, N), jnp.int32),
                   jax.ShapeDtypeStruct((PEER_SLOTS, N), F32)],
        compiler_params=_cp("parallel", "parallel"),
        name="peer_topk",
    )(qp_t, sk, g1, g2)


def _unpack_rows(x):
    lo = pltpu.bitcast(x << 16, F32)
    hi = pltpu.bitcast(x & jnp.uint32(0xFFFF0000), F32)
    return jnp.concatenate([lo, hi], axis=-1).astype(BF16)


def _token_pipeline(ids_ref, tab_ref, ring, sems, consume, finish_group=None):
    TB, R = PEER_TB, PEER_RING

    def group_copy(k, half):
        first = pl.multiple_of(k * R, R)
        return pltpu.make_async_copy(ids_ref.at[0, pl.ds(first, R)], ring.at[half], sems.at[half])

    group_copy(0, 0).start()

    def body(i, _):
        for half in range(2):
            k = 2 * i + half
            group_copy(k, half).wait()
            group_copy(k + 1, 1 - half).start()
            carry = None
            for s in range(R):
                rows = [tab_ref[pl.ds(ring[half, s, j], ROW_SUB), :] for j in range(PEER_SLOTS)]
                carry = consume(k * R + s, jnp.concatenate(rows, axis=0), carry)
            if finish_group is not None:
                finish_group(carry)
        return 0

    lax.fori_loop(0, TB // (2 * R), body, 0)
    group_copy(TB // R, 0).wait()


def _peer_u_kernel(ids_ref, h_ref, tab_ref, group_ref, at_ref, ring, sems, acc_s):
    ones = jnp.ones((8, 2 * 128), BF16)

    def consume(t, rows, acc):
        h = h_ref[pl.ds(t, 1), :]
        lo = pltpu.bitcast(rows << 16, F32).reshape(-1, 8, 128)
        hi = pltpu.bitcast(rows & jnp.uint32(0xFFFF0000), F32).reshape(-1, 8, 128)
        chunk = lambda s, half: h[:, s * 256 + half * 128:s * 256 + half * 128 + 128]
        h_lo = jnp.concatenate([chunk(s, 0) for s in range(ROW_SUB)] * 2, axis=0)
        h_hi = jnp.concatenate([chunk(s, 1) for s in range(ROW_SUB)] * 2, axis=0)
        prod = jnp.concatenate([(lo * h_lo[None]).reshape(-1, 128), (hi * h_hi[None]).reshape(-1, 128)],
                               axis=-1).astype(BF16)
        sums = lax.dot_general(ones, prod, (((1,), (1,)), ((), ())), preferred_element_type=F32)
        acc_s[pl.ds(t, 1), :] = sums[0:1, :]
        return acc

    _token_pipeline(ids_ref, tab_ref, ring, sems, consume)
    a_t = lax.dot_general(acc_s[...], group_ref[...], (((1,), (1,)), ((), ())), precision=HIGHEST,
                          preferred_element_type=F32)
    at_ref[...] = a_t.T


def _peer_specs(TB):
    R = PEER_SLOTS * ROW_SUB
    ids_spec = pl.BlockSpec((1, TB + PEER_RING, PEER_SLOTS), lambda i: (i, 0, 0))
    table_spec = pl.BlockSpec(memory_space=pltpu.VMEM)
    ring = [pltpu.SMEM((2, PEER_RING, PEER_SLOTS), jnp.int32), pltpu.SemaphoreType.DMA((2,))]
    return ids_spec, table_spec, ring


def peer_scores(ids_blocks, h, tab, group):
    N = h.shape[0]
    TB = PEER_TB
    R = PEER_SLOTS * ROW_SUB
    ids_spec, table_spec, ring = _peer_specs(TB)
    return pl.pallas_call(
        _peer_u_kernel,
        grid=(N // TB,),
        in_specs=[ids_spec, pl.BlockSpec((TB, D_MODEL), lambda i: (i, 0)), table_spec,
                  pl.BlockSpec(group.shape, lambda i: (0, 0))],
        out_specs=pl.BlockSpec((PEER_SLOTS, TB), lambda i: (0, i)),
        out_shape=jax.ShapeDtypeStruct((PEER_SLOTS, N), F32),
        scratch_shapes=ring + [pltpu.VMEM((TB, R), F32)],
        compiler_params=_cp("arbitrary"),
        name="peer_scores",
    )(ids_blocks, h, tab, group)


def _peer_v_kernel(ids_ref, at_ref, gt_ref, tab_ref, rep_ref, x_ref, res_gate_ref, o_ref, ring, sems, wexp):
    a = at_ref[...]
    w = gt_ref[...] * (0.5 * a * (1.0 + lax.erf(a * (1.0 / math.sqrt(2.0)))))
    wexp[...] = jnp.dot(w.T, rep_ref[...], precision=HIGHEST, preferred_element_type=F32)
    R = PEER_SLOTS * ROW_SUB
    own = (lax.broadcasted_iota(jnp.int32, (8, R), 0) ==
           lax.broadcasted_iota(jnp.int32, (8, R), 1) % ROW_SUB)

    def consume(t, rows, carry):
        x = _unpack_rows(rows)
        wsel = jnp.where(own, wexp[pl.ds(t, 1), :], 0.0).astype(BF16)
        y = jnp.dot(wsel, x, preferred_element_type=F32)
        for s in range(ROW_SUB):
            cols = slice(s * 256, (s + 1) * 256)
            o_ref[pl.ds(t, 1), cols] = x_ref[pl.ds(t, 1), cols] + res_gate_ref[0, :, cols] * y[s:s + 1, :]
        return carry

    _token_pipeline(ids_ref, tab_ref, ring, sems, consume)


def peer_combine(ids_flat, at, gt, tab, rep, x, res_gate):
    B, L, D = x.shape
    N = at.shape[1]
    TB = PEER_TB
    R = PEER_SLOTS * ROW_SUB
    ids_spec, table_spec, ring = _peer_specs(TB)
    slot_tile = pl.BlockSpec((PEER_SLOTS, TB), lambda i: (0, i))
    row_tile = pl.BlockSpec((TB, D), lambda i: (i, 0))
    out = pl.pallas_call(
        _peer_v_kernel,
        grid=(N // TB,),
        in_specs=[ids_spec, slot_tile, slot_tile, table_spec, pl.BlockSpec(rep.shape, lambda i: (0, 0)),
                  row_tile, pl.BlockSpec((1, 1, D), lambda i: ((i * TB) // L, 0, 0))],
        out_specs=row_tile,
        out_shape=jax.ShapeDtypeStruct((N, D), F32),
        scratch_shapes=ring + [pltpu.VMEM((TB, R), F32)],
        compiler_params=_cp("arbitrary"),
        name="peer_combine",
    )(ids_flat, at, gt, tab, rep, x.reshape(N, D), res_gate)
    return out.reshape(B, L, D)


def _pad_last(a, n):
    return jnp.pad(a, [(0, 0)] * (a.ndim - 1) + [(0, n - a.shape[-1])])


def _pack_kernel(x_ref, o_ref):
    n = x_ref.shape[0]
    bits = pltpu.bitcast(x_ref[...].astype(BF16).astype(F32), jnp.uint32)
    for s in range(ROW_SUB):
        lo = bits[:, s * 256:s * 256 + 128] >> 16
        hi = bits[:, s * 256 + 128:(s + 1) * 256] & jnp.uint32(0xFFFF0000)
        o_ref[pl.ds(s, n, stride=ROW_SUB), :] = lo | hi


def _pack_table(tab):
    E, D = tab.shape
    te = 512
    return pl.pallas_call(
        _pack_kernel,
        grid=(E // te,),
        in_specs=[pl.BlockSpec((te, D), lambda i: (i, 0))],
        out_specs=pl.BlockSpec((te * ROW_SUB, 128), lambda i: (i, 0)),
        out_shape=jax.ShapeDtypeStruct((E * ROW_SUB, 128), jnp.uint32),
        compiler_params=_cp("parallel"),
        name="pack_table",
    )(tab)


def _layer_weights(l, norm1_gain, norm2_gain, w_in, pool_w, pool_scale, mla_q_norm, mla_kv_norm, w_uq, w_ukv,
                   q_norm, k_norm, ml_gate_bias, ml_out_norm, w_out, peer_wq, peer_subkeys, peer_u, peer_v):
    parts, start = [], 0
    for size in IN_SIZES:
        parts.append(w_in[l][:, start:start + size])
        start += size
    w_in_r = jnp.concatenate([parts[0], parts[1], parts[2], _pad_last(parts[3], 128), parts[4], parts[5],
                              parts[6], parts[7], _pad_last(parts[8], 128)], axis=1).astype(BF16)
    wbd = jnp.zeros((POOL_WIDTH, POOL_WIDTH), F32)
    for g in range(len(POOL_WINDOWS)):
        sl = slice(g * POOL_GROUP, (g + 1) * POOL_GROUP)
        wbd = wbd.at[sl, sl].set(pool_w[l, g])
    wq = _pad_last(w_uq[l].reshape(MLA_Q_RANK, MLA_HEADS, MLA_QK), HEAD_PAD).reshape(MLA_Q_RANK, -1)
    ukv = w_ukv[l].reshape(MLA_KV_RANK, MLA_HEADS, MLA_NOPE + MLA_V)
    wk = _pad_last(ukv[..., :MLA_NOPE], HEAD_PAD).reshape(MLA_KV_RANK, -1)
    wv = _pad_last(ukv[..., MLA_NOPE:], HEAD_PAD).reshape(MLA_KV_RANK, -1)
    r = jnp.arange(128)[:, None]
    cidx = jnp.arange(MLA_HEADS * HEAD_PAD)[None, :]
    pk = ((r < MLA_ROPE) & (cidx % HEAD_PAD == MLA_NOPE + r)).astype(F32)
    lane = jnp.arange(ML_WIDTH)
    hs = (lane[:, None] // ML_DK == jnp.arange(128)[None, :]).astype(F32)
    wo = w_out[l]
    wa = wo[POOL_WIDTH:POOL_WIDTH + MLA_HEADS * MLA_V]
    return dict(
        n1=norm1_gain[l], n2=norm2_gain[l], w_in=w_in_r,
        wbd=wbd.astype(BF16), pool_scale=pool_scale[l],
        qan=mla_q_norm[l], kvan=mla_kv_norm[l],
        wq=wq.astype(BF16), wk=wk.astype(BF16), wv=wv.astype(BF16), pk=pk,
        qg=_pad_last(q_norm[l], HEAD_PAD).reshape(1, HEAD_PAD), kg=_pad_last(k_norm[l], HEAD_PAD).reshape(1, HEAD_PAD),
        gate_bias=_pad_last(ml_gate_bias[l], 128).reshape(1, 128),
        mlg=ml_out_norm[l].reshape(1, ML_WIDTH), hs=hs, hst=hs.T,
        wp=wo[:POOL_WIDTH].astype(BF16), wa=wa.astype(BF16),
        wm=wo[POOL_WIDTH + MLA_HEADS * MLA_V:].astype(BF16),
        peer_wq=peer_wq[l].astype(BF16),
        sk=peer_subkeys[l].reshape(2 * PEER_HEADS, PEER_KEYS, PEER_KEYS),
        u_tab=_pack_table(peer_u[l]), v_tab=_pack_table(peer_v[l]),
    )


def _peer_constants():
    R = PEER_SLOTS * ROW_SUB
    pairs = _candidate_pairs()
    ncand = -(-len(pairs) // 8) * 8
    k1 = jnp.array([p[0] for p in pairs] + [-1] * (ncand - len(pairs)))[:, None]
    k2 = jnp.array([p[1] for p in pairs] + [-1] * (ncand - len(pairs)))[:, None]
    rank = jnp.arange(PEER_TOPK)[None, :]
    g1 = (k1 == rank).astype(F32)
    g2 = (k2 == rank).astype(F32)
    rep = (jnp.arange(PEER_SLOTS)[:, None] == jnp.arange(R)[None, :] // ROW_SUB).astype(F32)
    return g1, g2, rep


def _rope_tables(n):
    n_rows = n // GRID_W
    row = jnp.repeat(jnp.arange(n_rows), GRID_W, total_repeat_length=n).astype(F32)
    col = (jnp.arange(n) % GRID_W).astype(F32)
    per_axis = MLA_ROPE // 2
    freqs = ROPE_BASE ** (-jnp.arange(0, per_axis, 2, dtype=F32) / per_axis)
    ang = jnp.concatenate([row[:, None] * freqs, col[:, None] * freqs], axis=-1)
    c, s = jnp.cos(ang), jnp.sin(ang)
    cos = jnp.concatenate([jnp.ones((n, MLA_NOPE), F32), c, c, jnp.ones((n, HEAD_PAD - MLA_QK), F32)], axis=-1)
    sin = jnp.concatenate([jnp.zeros((n, MLA_NOPE), F32), -s, s, jnp.zeros((n, HEAD_PAD - MLA_QK), F32)], axis=-1)
    return cos, sin


def _staged_ids(offs):
    blk = offs.reshape(-1, PEER_TB, PEER_SLOTS)
    spare = jnp.broadcast_to(blk[:, :1], (blk.shape[0], PEER_RING, PEER_SLOTS))
    return jnp.concatenate([blk, spare], axis=1)


def _peer_ffn(x, w, shift, scale, gate, consts):
    B, L, D = x.shape
    g1, g2, rep = consts
    qp_t, h = peer_query(x, w["n2"], shift, scale, w["peer_wq"])
    ids_t, gates_t = peer_topk(qp_t, w["sk"], g1, g2)
    ids_blocks = _staged_ids(ids_t.T * ROW_SUB)
    a_t = peer_scores(ids_blocks, h, w["u_tab"], rep)
    return peer_combine(ids_blocks, a_t, gates_t, w["v_tab"], rep, x, gate)


def kernel(x, c, ctx, c_ctx, norm1_gain, norm2_gain, w_ada, b_ada, w_in, pool_w, pool_scale, mla_q_norm,
           mla_kv_norm, w_uq, w_ukv, q_norm, k_norm, ml_gate_bias, ml_out_norm, w_out, peer_wq, peer_subkeys,
           peer_u, peer_v):
    B, S, D = x.shape
    Lc_ctx = ctx.shape[1]
    depth = w_in.shape[0]
    cos_x, sin_x = _rope_tables(S)
    cos_c = jnp.ones((Lc_ctx, HEAD_PAD), F32)
    sin_c = jnp.zeros((Lc_ctx, HEAD_PAD), F32)
    consts = _peer_constants()
    cond8 = jnp.zeros((8, D), F32).at[:B].set(c).at[B].set(c_ctx)
    W = ML_WIDTH
    zero_state = (jnp.zeros((B, 2, W, W), F32), jnp.zeros((B, 2, 1, W), F32),
                  jnp.full((B, 2, 1, W), NEG_INIT, F32))
    x_ctx = ctx
    for l in range(depth):
        last = l == depth - 1
        w = _layer_weights(l, norm1_gain, norm2_gain, w_in, pool_w, pool_scale, mla_q_norm, mla_kv_norm, w_uq,
                           w_ukv, q_norm, k_norm, ml_gate_bias, ml_out_norm, w_out, peer_wq, peer_subkeys,
                           peer_u, peer_v)
        mods = ada_mod(cond8, w_ada[l], b_ada[l])
        m_x = [mods[:B, i * D:(i + 1) * D].reshape(B, 1, D) for i in range(6)]
        m_c = [jnp.broadcast_to(mods[B, i * D:(i + 1) * D].reshape(1, 1, D), (B, 1, D)) for i in range(6)]

        zp_c, zm_c, zl_c = in_proj(x_ctx, w["n1"], m_c[0], m_c[1], w["w_in"])
        qc, kc, vc = mla_qkv(zm_c, w["qan"], w["kvan"], w["wq"], w["wk"], w["wv"], w["pk"], w["qg"], w["kg"],
                             cos_c, sin_c)
        hf_c, hb_c, cT, nT, mT = mlstm(zl_c, w["gate_bias"], zero_state)
        if not last:
            pool_c = pool_mixer(zp_c, w["wbd"], w["pool_scale"])
            attn_c = attention(qc, kc, vc)
            xc = out_proj(x_ctx, pool_c, attn_c, hf_c, hb_c, zl_c, w["mlg"], w["hs"], w["hst"],
                          w["wp"], w["wa"], w["wm"], m_c[2])
            x_ctx_new = _peer_ffn(xc, w, m_c[3], m_c[4], m_c[5], consts)

        zp, zm, zl = in_proj(x, w["n1"], m_x[0], m_x[1], w["w_in"])
        pool_x = pool_mixer(zp, w["wbd"], w["pool_scale"])
        q, k, v = mla_qkv(zm, w["qan"], w["kvan"], w["wq"], w["wk"], w["wv"], w["pk"], w["qg"], w["kg"],
                          cos_x, sin_x)
        attn_x = attention(q, kc, vc, k, v)
        hf, hb, _, _, _ = mlstm(zl, w["gate_bias"], (cT, nT, mT))
        x = out_proj(x, pool_x, attn_x, hf, hb, zl, w["mlg"], w["hs"], w["hst"],
                     w["wp"], w["wa"], w["wm"], m_x[2])
        x = _peer_ffn(x, w, m_x[3], m_x[4], m_x[5], consts)
        if not last:
            x_ctx = x_ctx_new
    return x
```

```python
import functools
import math

import jax
import jax.numpy as jnp
from jax import lax
from jax.experimental import pallas as pl
from jax.experimental.pallas import tpu as pltpu

F32 = jnp.float32
BF16 = jnp.bfloat16
HIGHEST = lax.Precision.HIGHEST

EPS = 1e-6
D_MODEL = 1024
GRID_W = 64
ROPE_BASE = 10000.0
POOL_WINDOWS = (2, 4, 8, 16)
POOL_GROUP = 64
POOL_WIDTH = 256
MLA_HEADS = 8
MLA_NOPE = 64
MLA_ROPE = 32
MLA_QK = 96
MLA_V = 64
MLA_Q_RANK = 384
MLA_KV_RANK = 256
MLA_SCALE = MLA_QK ** -0.5
Q_PRESCALE = MLA_SCALE * math.log2(math.e)
HEAD_PAD = 128
ATT_TQ = 512
ATT_TC = 512
VT_ROWS = 80
ML_HEADS = 4
ML_DK = 64
ML_WIDTH = 256
ML_CHUNK = 128
NEG_INIT = -1e30
IN_SIZES = (256, 384, 256, 32, 256, 256, 256, 256, 16)
PEER_HEADS = 8
PEER_KEYS = 128
PEER_TOPK = 16
PEER_SLOTS = PEER_HEADS * PEER_TOPK
PEER_TB = 256
PEER_RING = 8
ROW_SUB = 4

Z_POOL, Z_MLA, Z_ML = 256, 768, 1152
VMEM_LIMIT = 56 * 1024 * 1024


def _cp(*sem, vmem=VMEM_LIMIT):
    return pltpu.CompilerParams(dimension_semantics=sem, vmem_limit_bytes=vmem)


def _ada_kernel(c_ref, w_ref, b_ref, o_ref):
    c = c_ref[...]
    s = c * (1.0 / (1.0 + jnp.exp(-c)))
    o_ref[...] = jnp.dot(s, w_ref[...], precision=HIGHEST, preferred_element_type=F32) + b_ref[...]


def ada_mod(cond8, w, b):
    n = w.shape[1]
    tn = n // 4
    return pl.pallas_call(
        _ada_kernel,
        grid=(n // tn,),
        in_specs=[pl.BlockSpec((8, D_MODEL), lambda j: (0, 0)),
                  pl.BlockSpec((D_MODEL, tn), lambda j: (0, j)),
                  pl.BlockSpec((1, tn), lambda j: (0, j))],
        out_specs=pl.BlockSpec((8, tn), lambda j: (0, j)),
        out_shape=jax.ShapeDtypeStruct((8, n), F32),
        compiler_params=_cp("arbitrary"),
        name="ada_mod",
    )(cond8, w, b.reshape(1, n))


def _modulated(x, gain, shift, scale):
    ms = jnp.mean(x * x, axis=-1, keepdims=True)
    y = x * lax.rsqrt(ms + EPS) * gain
    return y * (1.0 + scale) + shift


def _inproj_kernel(x_ref, gain_ref, shift_ref, scale_ref, w_ref, zp_ref, zm_ref, zl_ref):
    h = _modulated(x_ref[0], gain_ref[...], shift_ref[0], scale_ref[0])
    res = jnp.dot(h.astype(BF16), w_ref[...], preferred_element_type=F32)
    zp_ref[0] = res[:, :Z_POOL]
    zm_ref[0] = res[:, Z_POOL:Z_POOL + Z_MLA]
    zl_ref[0] = res[:, Z_POOL + Z_MLA:]


def in_proj(x, gain, shift, scale, w):
    B, L, D = x.shape
    tm = min(512, L)
    n = w.shape[1]
    vec = pl.BlockSpec((1, 1, D), lambda b, i: (b, 0, 0))
    return pl.pallas_call(
        _inproj_kernel,
        grid=(B, L // tm),
        in_specs=[pl.BlockSpec((1, tm, D), lambda b, i: (b, i, 0)),
                  pl.BlockSpec((1, D), lambda b, i: (0, 0)),
                  vec, vec,
                  pl.BlockSpec((D, n), lambda b, i: (0, 0))],
        out_specs=[pl.BlockSpec((1, tm, Z_POOL), lambda b, i: (b, i, 0)),
                   pl.BlockSpec((1, tm, Z_MLA), lambda b, i: (b, i, 0)),
                   pl.BlockSpec((1, tm, Z_ML), lambda b, i: (b, i, 0))],
        out_shape=[jax.ShapeDtypeStruct((B, L, Z_POOL), F32),
                   jax.ShapeDtypeStruct((B, L, Z_MLA), F32),
                   jax.ShapeDtypeStruct((B, L, Z_ML), F32)],
        compiler_params=_cp("parallel", "parallel"),
        name="in_proj",
    )(x, gain.reshape(1, D), shift, scale, w)


def _peerq_kernel(x_ref, gain_ref, shift_ref, scale_ref, w_ref, qp_ref, h_ref):
    h = _modulated(x_ref[0], gain_ref[...], shift_ref[0], scale_ref[0])
    hb = h.astype(BF16)
    h_ref[...] = hb.astype(F32)
    res = jnp.dot(hb, w_ref[...], preferred_element_type=F32)
    for g in range(2 * PEER_HEADS):
        qp_ref[g] = res[:, g * PEER_KEYS:(g + 1) * PEER_KEYS].T


def peer_query(x, gain, shift, scale, w):
    B, L, D = x.shape
    tm = min(512, L)
    nb = L // tm
    n = w.shape[1]
    vec = pl.BlockSpec((1, 1, D), lambda b, i: (b, 0, 0))
    return pl.pallas_call(
        _peerq_kernel,
        grid=(B, nb),
        in_specs=[pl.BlockSpec((1, tm, D), lambda b, i: (b, i, 0)),
                  pl.BlockSpec((1, D), lambda b, i: (0, 0)),
                  vec, vec,
                  pl.BlockSpec((D, n), lambda b, i: (0, 0))],
        out_specs=[pl.BlockSpec((2 * PEER_HEADS, PEER_KEYS, tm), lambda b, i: (0, 0, b * nb + i)),
                   pl.BlockSpec((tm, D), lambda b, i: (b * nb + i, 0))],
        out_shape=[jax.ShapeDtypeStruct((2 * PEER_HEADS, PEER_KEYS, B * L), F32),
                   jax.ShapeDtypeStruct((B * L, D), F32)],
        compiler_params=_cp("parallel", "parallel"),
        name="peer_query",
    )(x, gain.reshape(1, D), shift, scale, w)


def _pool_kernel(p_ref, c_ref, n_ref, wbd_ref, sc_ref, o_ref, *, L, T):
    i = pl.program_id(1)
    cur = c_ref[0]
    u3 = jnp.concatenate([p_ref[0], cur, n_ref[0]], axis=0).astype(BF16)
    t = i * T + lax.broadcasted_iota(jnp.int32, (T, 3 * T), 0)
    s = (i - 1) * T + lax.broadcasted_iota(jnp.int32, (T, 3 * T), 1)
    lane = lax.broadcasted_iota(jnp.int32, (T, POOL_WIDTH), 1)
    trow = i * T + lax.broadcasted_iota(jnp.int32, (T, POOL_WIDTH), 0)
    win = jnp.zeros((T, POOL_WIDTH), F32)
    for g, w in enumerate(POOL_WINDOWS):
        lo = jnp.maximum(t - w // 2, 0)
        hi = jnp.minimum(t + w // 2, L)
        band = jnp.where((s >= lo) & (s < hi), 1.0, 0.0).astype(BF16)
        ws = jnp.dot(band, u3, preferred_element_type=F32)
        cnt = (jnp.minimum(trow + w // 2, L) - jnp.maximum(trow - w // 2, 0)).astype(F32)
        in_group = (lane >= g * POOL_GROUP) & (lane < (g + 1) * POOL_GROUP)
        win = jnp.where(in_group, ws / cnt, win)
    d = win - cur
    y = jnp.dot(d.astype(BF16), wbd_ref[...], preferred_element_type=F32)
    o_ref[0] = y * sc_ref[...]


def pool_mixer(z_pool, wbd, scale):
    B, L, C = z_pool.shape
    T = 256
    nb = L // T
    return pl.pallas_call(
        functools.partial(_pool_kernel, L=L, T=T),
        grid=(B, nb),
        in_specs=[pl.BlockSpec((1, T, C), lambda b, i: (b, jnp.maximum(i - 1, 0), 0)),
                  pl.BlockSpec((1, T, C), lambda b, i: (b, i, 0)),
                  pl.BlockSpec((1, T, C), lambda b, i: (b, jnp.minimum(i + 1, nb - 1), 0)),
                  pl.BlockSpec((C, C), lambda b, i: (0, 0)),
                  pl.BlockSpec((1, C), lambda b, i: (0, 0))],
        out_specs=pl.BlockSpec((1, T, C), lambda b, i: (b, i, 0)),
        out_shape=jax.ShapeDtypeStruct((B, L, C), F32),
        compiler_params=_cp("parallel", "parallel"),
        name="pool_mixer",
    )(z_pool, z_pool, z_pool, wbd, scale.reshape(1, C))


def _rms(x, gain, n):
    ss = jnp.sum(x * x, axis=-1, keepdims=True) * (1.0 / n)
    return x * lax.rsqrt(ss + EPS) * gain


def _mla_kernel(z_ref, qan_ref, kvan_ref, wq_ref, wk_ref, wv_ref, pk_ref, qg_ref, kg_ref,
                cos_ref, sin_ref, q_out, k_out, v_out):
    z = z_ref[0]
    zq = z[:, :MLA_Q_RANK]
    zkv = z[:, MLA_Q_RANK:MLA_Q_RANK + MLA_KV_RANK]
    zkr = z[:, MLA_Q_RANK + MLA_KV_RANK:]
    nq = _rms(zq, qan_ref[...], MLA_Q_RANK).astype(BF16)
    nkv = _rms(zkv, kvan_ref[...], MLA_KV_RANK).astype(BF16)
    qp = jnp.dot(nq, wq_ref[...], preferred_element_type=F32)
    kp = jnp.dot(nkv, wk_ref[...], preferred_element_type=F32)
    kp = kp + jnp.dot(zkr, pk_ref[...], precision=HIGHEST, preferred_element_type=F32)
    vp = jnp.dot(nkv, wv_ref[...], preferred_element_type=F32)
    cos = cos_ref[...]
    sin = sin_ref[...]
    lane = lax.broadcasted_iota(jnp.int32, cos.shape, 1)
    first_half = lane < MLA_NOPE + MLA_ROPE // 2
    extra = (VT_ROWS - MLA_V, z.shape[0])
    ones_rows = jnp.where(lax.broadcasted_iota(jnp.int32, extra, 0) == 0, 1.0, 0.0)
    for h in range(MLA_HEADS):
        sl = slice(h * HEAD_PAD, (h + 1) * HEAD_PAD)
        for src, gain_ref, out, mult in ((qp, qg_ref, q_out, Q_PRESCALE), (kp, kg_ref, k_out, 1.0)):
            xn = _rms(src[:, sl], gain_ref[...], MLA_QK)
            partner = jnp.where(first_half, pltpu.roll(xn, HEAD_PAD - MLA_ROPE // 2, 1),
                                pltpu.roll(xn, MLA_ROPE // 2, 1))
            xr = xn * cos + partner * sin
            out[0, h] = (xr * mult).astype(BF16)
        v_out[0, h] = jnp.concatenate([vp[:, sl].T[:MLA_V, :], ones_rows], axis=0).astype(BF16)


def mla_qkv(z_mla, qan, kvan, wq, wk, wv, pk, qg, kg, cos, sin):
    B, L, _ = z_mla.shape
    T = 256
    full = lambda shape: pl.BlockSpec(shape, lambda b, i: (0,) * len(shape))
    head_out = pl.BlockSpec((1, MLA_HEADS, T, HEAD_PAD), lambda b, i: (b, 0, i, 0))
    out_sds = jax.ShapeDtypeStruct((B, MLA_HEADS, L, HEAD_PAD), BF16)
    vt_out = pl.BlockSpec((1, MLA_HEADS, VT_ROWS, T), lambda b, i: (b, 0, 0, i))
    vt_sds = jax.ShapeDtypeStruct((B, MLA_HEADS, VT_ROWS, L), BF16)
    return pl.pallas_call(
        _mla_kernel,
        grid=(B, L // T),
        in_specs=[pl.BlockSpec((1, T, Z_MLA), lambda b, i: (b, i, 0)),
                  full((1, MLA_Q_RANK)), full((1, MLA_KV_RANK)),
                  full(wq.shape), full(wk.shape), full(wv.shape), full(pk.shape),
                  full((1, HEAD_PAD)), full((1, HEAD_PAD)),
                  pl.BlockSpec((T, HEAD_PAD), lambda b, i: (i, 0)),
                  pl.BlockSpec((T, HEAD_PAD), lambda b, i: (i, 0))],
        out_specs=[head_out, head_out, vt_out],
        out_shape=[out_sds, out_sds, vt_sds],
        compiler_params=_cp("parallel", "parallel"),
        name="mla_qkv",
    )(z_mla, qan.reshape(1, -1), kvan.reshape(1, -1), wq, wk, wv, pk, qg, kg, cos, sin)


def _flash_kernel(q_ref, kc_ref, vct_ref, *rest, nchunks):
    if nchunks:
        k_ref, vt_ref, o_ref, s_s, mx_s, m_s, acc_s = rest
    else:
        o_ref, s_s, mx_s, m_s, acc_s = rest
    nc = kc_ref.shape[2]

    def scores(h, kblk):
        n = kblk.shape[0]
        st = lax.dot_general(kblk, q_ref[0, h], (((1,), (1,)), ((), ())), preferred_element_type=F32)
        s_s[h, :n] = st
        mx_s[h] = jnp.max(st, axis=0, keepdims=True)

    def update(h, n, vtblk):
        st = s_s[h, :n]
        m = m_s[h]
        m_new = jnp.maximum(m, mx_s[h])
        p = jnp.exp2(st - m_new).astype(BF16)
        m_s[h] = m_new
        acc_s[h] = jnp.exp2(m - m_new) * acc_s[h] + jnp.dot(vtblk, p, preferred_element_type=F32)

    m_s[...] = jnp.full(m_s.shape, -jnp.inf, F32)
    acc_s[...] = jnp.zeros(acc_s.shape, F32)
    scores(0, kc_ref[0, 0])
    scores(1, kc_ref[0, 1])
    update(0, nc, vct_ref[0, 0])
    if nchunks:
        scores(0, k_ref[0, 0, pl.ds(0, ATT_TC), :])
    update(1, nc, vct_ref[0, 1])
    if nchunks:
        def body(c, _):
            off = pl.multiple_of(c * ATT_TC, ATT_TC)
            nxt = pl.multiple_of(jnp.minimum(c + 1, nchunks - 1) * ATT_TC, ATT_TC)
            scores(1, k_ref[0, 1, pl.ds(off, ATT_TC), :])
            update(0, ATT_TC, vt_ref[0, 0, :, pl.ds(off, ATT_TC)])
            scores(0, k_ref[0, 0, pl.ds(nxt, ATT_TC), :])
            update(1, ATT_TC, vt_ref[0, 1, :, pl.ds(off, ATT_TC)])
            return 0
        lax.fori_loop(0, nchunks, body, 0, unroll=8)
    o = jnp.concatenate([acc_s[h, :MLA_V] / acc_s[h, MLA_V:MLA_V + 1] for h in range(2)], axis=0)
    o_ref[0] = o.T.astype(BF16)


def attention(q, kc, vct, k=None, vt=None):
    B, H, Lq, _ = q.shape
    Lc = kc.shape[2]
    tq = min(ATT_TQ, Lq)
    pair4 = lambda n, d: pl.BlockSpec((1, 2, n, d), lambda b, h, i: (b, h, 0, 0))
    in_specs = [pl.BlockSpec((1, 2, tq, HEAD_PAD), lambda b, h, i: (b, h, i, 0)),
                pair4(Lc, HEAD_PAD), pair4(VT_ROWS, Lc)]
    args = [q, kc, vct]
    nchunks = 0
    if k is not None:
        Lk = k.shape[2]
        nchunks = Lk // ATT_TC
        in_specs += [pair4(Lk, HEAD_PAD), pair4(VT_ROWS, Lk)]
        args += [k, vt]
    return pl.pallas_call(
        functools.partial(_flash_kernel, nchunks=nchunks),
        grid=(B, H // 2, Lq // tq),
        in_specs=in_specs,
        out_specs=pl.BlockSpec((1, tq, 2 * MLA_V), lambda b, h, i: (b, i, h)),
        out_shape=jax.ShapeDtypeStruct((B, Lq, H * MLA_V), BF16),
        scratch_shapes=[pltpu.VMEM((2, max(ATT_TC, Lc), tq), F32), pltpu.VMEM((2, 1, tq), F32),
                        pltpu.VMEM((2, 1, tq), F32), pltpu.VMEM((2, VT_ROWS, tq), F32)],
        compiler_params=_cp("parallel", "parallel", "arbitrary"),
        name="attention",
    )(*args)


def _log_sigmoid(x):
    return jnp.minimum(x, 0.0) - jnp.log(1.0 + jnp.exp(-jnp.abs(x)))


def _mlstm_direction(d, q, k, v, g, C_s, n_s, m_s):
    Lc = q.shape[0]
    row = lax.broadcasted_iota(jnp.int32, (Lc, Lc), 0)
    col = lax.broadcasted_iota(jnp.int32, (Lc, Lc), 1)
    tri = (col <= row) if d == 0 else (col >= row)
    logf = _log_sigmoid(g)
    bcol = jnp.dot(jnp.where(tri, 1.0, 0.0), logf, precision=HIGHEST, preferred_element_type=F32)
    bT = bcol.T
    gT = g.T
    lane = lax.broadcasted_iota(jnp.int32, (1, ML_WIDTH), 1)
    kb = k.astype(BF16)
    vb = v.astype(BF16)
    Cst = C_s[d]
    nst = n_s[d]
    mst = m_s[d]
    qc = jnp.dot(q.astype(BF16), Cst.astype(BF16), preferred_element_type=F32)
    out = jnp.zeros((Lc, ML_WIDTH), F32)
    ws_all = jnp.zeros((Lc, ML_WIDTH), F32)
    wprev_all = jnp.zeros((1, ML_WIDTH), F32)
    mnew_all = jnp.zeros((1, ML_WIDTH), F32)
    for h in range(ML_HEADS):
        il = 8 * d + h
        fl = 8 * d + 4 + h
        head = (lane >= h * ML_DK) & (lane < (h + 1) * ML_DK)
        bc = bcol[:, fl:fl + 1]
        br = bT[fl:fl + 1, :]
        ir = gT[il:il + 1, :]
        ic = g[:, il:il + 1]
        mprev = mst[:, h * ML_DK:h * ML_DK + 1]
        dmat = jnp.where(tri, bc - br + ir, -jnp.inf)
        inter = bc + mprev
        mj = jnp.maximum(inter, jnp.max(dmat, axis=-1, keepdims=True))
        w_inter = jnp.exp(inter - mj)
        qh = jnp.where(head, q, 0.0)
        s = lax.dot_general(qh.astype(BF16), kb, (((1,), (1,)), ((), ())), preferred_element_type=F32)
        qk = s * jnp.exp(dmat - mj)
        pv = jnp.dot(qk.astype(BF16), vb, preferred_element_type=F32)
        qn = jnp.sum(qh * nst, axis=-1, keepdims=True)
        den = jnp.sum(qk, axis=-1, keepdims=True) + w_inter * qn
        denom = jnp.maximum(jnp.abs(den), jnp.exp(-mj))
        out = jnp.where(head, (pv + qc * w_inter) / denom, out)
        blast = bc[Lc - 1:Lc, :] if d == 0 else bc[0:1, :]
        dec = blast - bc + ic
        mnew = jnp.maximum(blast + mprev, jnp.max(dec, axis=0, keepdims=True))
        wprev = jnp.exp(blast + mprev - mnew)
        ws = jnp.exp(dec - mnew)
        ws_all = jnp.where(head, ws, ws_all)
        wprev_all = jnp.where(head, wprev, wprev_all)
        mnew_all = jnp.where(head, mnew, mnew_all)
    kw = k * ws_all
    upd = jnp.dot(kw.T.astype(BF16), vb, preferred_element_type=F32)
    r2 = lax.broadcasted_iota(jnp.int32, (ML_WIDTH, ML_WIDTH), 0) // ML_DK
    c2 = lax.broadcasted_iota(jnp.int32, (ML_WIDTH, ML_WIDTH), 1) // ML_DK
    C_s[d] = Cst * wprev_all + jnp.where(r2 == c2, upd, 0.0)
    n_s[d] = nst * wprev_all + jnp.sum(kw, axis=0, keepdims=True)
    m_s[d] = mnew_all
    return out


def _mlstm_kernel(qf_ref, kf_ref, vf_ref, gf_ref, qb_ref, kb_ref, vb_ref, gb_ref, bias_ref,
                  c0_ref, n0_ref, m0_ref, hf_ref, hb_ref, cT_ref, nT_ref, mT_ref, C_s, n_s, m_s):
    c = pl.program_id(1)

    @pl.when(c == 0)
    def _():
        C_s[...] = c0_ref[0]
        n_s[...] = n0_ref[0]
        m_s[...] = m0_ref[0]

    scale = ML_DK ** -0.5
    hf_ref[0] = _mlstm_direction(0, qf_ref[0] * scale, kf_ref[0], vf_ref[0],
                                 gf_ref[0] + bias_ref[...], C_s, n_s, m_s)
    hb_ref[0] = _mlstm_direction(1, qb_ref[0] * scale, kb_ref[0], vb_ref[0],
                                 gb_ref[0] + bias_ref[...], C_s, n_s, m_s)

    @pl.when(c == pl.num_programs(1) - 1)
    def _():
        cT_ref[0] = C_s[...]
        nT_ref[0] = n_s[...]
        mT_ref[0] = m_s[...]


def mlstm(z_ml, bias, state):
    B, L, _ = z_ml.shape
    Lc = ML_CHUNK
    nc = L // Lc
    c0, n0, m0 = state
    W = ML_WIDTH
    fwd = lambda j: pl.BlockSpec((1, Lc, W), lambda b, c: (b, c, j))
    bwd = lambda j: pl.BlockSpec((1, Lc, W), lambda b, c: (b, nc - 1 - c, j))
    gcol = 4 * W // 128
    st_c = pl.BlockSpec((1, 2, W, W), lambda b, c: (b, 0, 0, 0))
    st_v = pl.BlockSpec((1, 2, 1, W), lambda b, c: (b, 0, 0, 0))
    return pl.pallas_call(
        _mlstm_kernel,
        grid=(B, nc),
        in_specs=[fwd(0), fwd(1), fwd(2), pl.BlockSpec((1, Lc, 128), lambda b, c: (b, c, gcol)),
                  bwd(0), bwd(1), bwd(2), pl.BlockSpec((1, Lc, 128), lambda b, c: (b, nc - 1 - c, gcol)),
                  pl.BlockSpec((1, 128), lambda b, c: (0, 0)),
                  st_c, st_v, st_v],
        out_specs=[pl.BlockSpec((1, Lc, W), lambda b, c: (b, c, 0)),
                   pl.BlockSpec((1, Lc, W), lambda b, c: (b, nc - 1 - c, 0)),
                   st_c, st_v, st_v],
        out_shape=[jax.ShapeDtypeStruct((B, L, W), F32), jax.ShapeDtypeStruct((B, L, W), F32),
                   jax.ShapeDtypeStruct((B, 2, W, W), F32), jax.ShapeDtypeStruct((B, 2, 1, W), F32),
                   jax.ShapeDtypeStruct((B, 2, 1, W), F32)],
        scratch_shapes=[pltpu.VMEM((2, W, W), F32), pltpu.VMEM((2, 1, W), F32), pltpu.VMEM((2, 1, W), F32)],
        compiler_params=_cp("parallel", "arbitrary"),
        name="mlstm",
    )(z_ml, z_ml, z_ml, z_ml, z_ml, z_ml, z_ml, z_ml, bias, c0, n0, m0)


def _outproj_kernel(x_ref, pool_ref, attn_ref, hf_ref, hb_ref, op_ref, mlg_ref, hs_ref, hst_ref,
                    wp_ref, wa_ref, wm_ref, ga_ref, o_ref):
    h = hf_ref[0] + hb_ref[0]
    ss = jnp.dot(h * h, hs_ref[...], precision=HIGHEST, preferred_element_type=F32) * (1.0 / ML_DK)
    inv = jnp.dot(lax.rsqrt(ss + EPS), hst_ref[...], precision=HIGHEST, preferred_element_type=F32)
    op = op_ref[0]
    ml = h * inv * mlg_ref[...] * (1.0 / (1.0 + jnp.exp(-op)))
    mix = jnp.dot(pool_ref[0].astype(BF16), wp_ref[...], preferred_element_type=F32)
    mix += jnp.dot(attn_ref[0], wa_ref[...], preferred_element_type=F32)
    mix += jnp.dot(ml.astype(BF16), wm_ref[...], preferred_element_type=F32)
    o_ref[0] = x_ref[0] + ga_ref[0] * mix


def out_proj(x, pool, attn, hf, hb, z_ml, mlg, hs, hst, wp, wa, wm, gate):
    B, L, D = x.shape
    T = min(512, L)
    W = ML_WIDTH
    tile = lambda w, j=0: pl.BlockSpec((1, T, w), lambda b, i: (b, i, j))
    full = lambda shape: pl.BlockSpec(shape, lambda b, i: (0,) * len(shape))
    return pl.pallas_call(
        _outproj_kernel,
        grid=(B, L // T),
        in_specs=[tile(D), tile(W), tile(MLA_HEADS * MLA_V), tile(W), tile(W), tile(W, 3),
                  full((1, W)), full(hs.shape), full(hst.shape),
                  full(wp.shape), full(wa.shape), full(wm.shape),
                  pl.BlockSpec((1, 1, D), lambda b, i: (b, 0, 0))],
        out_specs=tile(D),
        out_shape=jax.ShapeDtypeStruct((B, L, D), F32),
        compiler_params=_cp("parallel", "parallel"),
        name="out_proj",
    )(x, pool, attn, hf, hb, z_ml, mlg, hs, hst, wp, wa, wm, gate)


def _top16_rows(s, row):
    big = float(s.shape[0])
    vals, idxs = [], []
    for _ in range(PEER_TOPK):
        m = jnp.max(s, axis=0, keepdims=True)
        idx = jnp.min(jnp.where(s == m, row, big), axis=0, keepdims=True)
        vals.append(m)
        idxs.append(idx)
        s = jnp.where(row == idx, -jnp.inf, s)
    return jnp.concatenate(vals, axis=0), jnp.concatenate(idxs, axis=0)


def _candidate_pairs():
    return [(k1, k2) for k1 in range(PEER_TOPK) for k2 in range(PEER_TOPK) if (k1 + 1) * (k2 + 1) <= PEER_TOPK]


def _topk_kernel(qp_ref, sk_ref, g1_ref, g2_ref, ids_ref, gate_ref):
    T = 256
    row = lax.broadcasted_iota(jnp.int32, (PEER_KEYS, T), 0).astype(F32)
    pick = lambda g_ref, a: jnp.dot(g_ref[...], a, precision=HIGHEST, preferred_element_type=F32)
    ncand = g1_ref.shape[0]
    crow = lax.broadcasted_iota(jnp.int32, (ncand, T), 0).astype(F32)

    def tile(lt, _):
        cols = pl.ds(pl.multiple_of(lt * T, T), T)
        tops = []
        for c in range(2):
            s = jnp.dot(sk_ref[c], qp_ref[c, :, cols], precision=HIGHEST, preferred_element_type=F32)
            tops.append(_top16_rows(s, row))
        cand = pick(g1_ref, tops[0][0]) + pick(g2_ref, tops[1][0])
        cand = jnp.where(crow < float(len(_candidate_pairs())), cand, -jnp.inf)
        expert = pick(g1_ref, tops[0][1]) * float(PEER_KEYS) + pick(g2_ref, tops[1][1])
        best, eids = [], []
        for _ in range(PEER_TOPK):
            m = jnp.max(cand, axis=0, keepdims=True)
            ci = jnp.min(jnp.where(cand == m, crow, float(ncand)), axis=0, keepdims=True)
            sel = crow == ci
            best.append(m)
            eids.append(jnp.sum(jnp.where(sel, expert, 0.0), axis=0, keepdims=True))
            cand = jnp.where(sel, -jnp.inf, cand)
        best = jnp.concatenate(best, axis=0)
        p = jnp.exp(best - best[0:1, :])
        gate_ref[:, cols] = p / jnp.sum(p, axis=0, keepdims=True)
        ids_ref[:, cols] = jnp.concatenate(eids, axis=0).astype(jnp.int32)
        return 0

    lax.fori_loop(0, qp_ref.shape[2] // T, tile, 0)


def peer_topk(qp_t, sk, g1, g2):
    N = qp_t.shape[2]
    T = 512
    return pl.pallas_call(
        _topk_kernel,
        grid=(N // T, PEER_HEADS),
        in_specs=[pl.BlockSpec((2, PEER_KEYS, T), lambda i, h: (h, 0, i)),
                  pl.BlockSpec((2, PEER_KEYS, PEER_KEYS), lambda i, h: (h, 0, 0)),
                  pl.BlockSpec(g1.shape, lambda i, h: (0, 0)),
                  pl.BlockSpec(g2.shape, lambda i, h: (0, 0))],
        out_specs=[pl.BlockSpec((PEER_TOPK, T), lambda i, h: (h, i)),
                   pl.BlockSpec((PEER_TOPK, T), lambda i, h: (h, i))],
        out_shape=[jax.ShapeDtypeStruct((PEER_SLOTS, N), jnp.int32),
                   jax.ShapeDtypeStruct((PEER_SLOTS, N), F32)],
        compiler_params=_cp("parallel", "parallel"),
        name="peer_topk",
    )(qp_t, sk, g1, g2)


def _unpack_rows(x):
    lo = pltpu.bitcast(x << 16, F32)
    hi = pltpu.bitcast(x & jnp.uint32(0xFFFF0000), F32)
    return jnp.concatenate([lo, hi], axis=-1).astype(BF16)


def _token_pipeline(ids_ref, tab_ref, ring, sems, consume, finish_group=None):
    TB, R = PEER_TB, PEER_RING

    def group_copy(k, half):
        first = pl.multiple_of(k * R, R)
        return pltpu.make_async_copy(ids_ref.at[0, pl.ds(first, R)], ring.at[half], sems.at[half])

    group_copy(0, 0).start()

    def body(i, _):
        for half in range(2):
            k = 2 * i + half
            group_copy(k, half).wait()
            group_copy(k + 1, 1 - half).start()
            carry = None
            for s in range(R):
                rows = [tab_ref[pl.ds(ring[half, s, j], ROW_SUB), :] for j in range(PEER_SLOTS)]
                carry = consume(k * R + s, jnp.concatenate(rows, axis=0), carry)
            if finish_group is not None:
                finish_group(carry)
        return 0

    lax.fori_loop(0, TB // (2 * R), body, 0)
    group_copy(TB // R, 0).wait()


def _peer_u_kernel(ids_ref, h_ref, tab_ref, group_ref, at_ref, ring, sems, acc_s):
    ones = jnp.ones((8, 2 * 128), BF16)

    def consume(t, rows, acc):
        h = h_ref[pl.ds(t, 1), :]
        lo = pltpu.bitcast(rows << 16, F32).reshape(-1, 8, 128)
        hi = pltpu.bitcast(rows & jnp.uint32(0xFFFF0000), F32).reshape(-1, 8, 128)
        chunk = lambda s, half: h[:, s * 256 + half * 128:s * 256 + half * 128 + 128]
        h_lo = jnp.concatenate([chunk(s, 0) for s in range(ROW_SUB)] * 2, axis=0)
        h_hi = jnp.concatenate([chunk(s, 1) for s in range(ROW_SUB)] * 2, axis=0)
        prod = jnp.concatenate([(lo * h_lo[None]).reshape(-1, 128), (hi * h_hi[None]).reshape(-1, 128)],
                               axis=-1).astype(BF16)
        sums = lax.dot_general(ones, prod, (((1,), (1,)), ((), ())), preferred_element_type=F32)
        acc_s[pl.ds(t, 1), :] = sums[0:1, :]
        return acc

    _token_pipeline(ids_ref, tab_ref, ring, sems, consume)
    a_t = lax.dot_general(acc_s[...], group_ref[...], (((1,), (1,)), ((), ())), precision=HIGHEST,
                          preferred_element_type=F32)
    at_ref[...] = a_t.T


def _peer_specs(TB):
    R = PEER_SLOTS * ROW_SUB
    ids_spec = pl.BlockSpec((1, TB + PEER_RING, PEER_SLOTS), lambda i: (i, 0, 0))
    table_spec = pl.BlockSpec(memory_space=pltpu.VMEM)
    ring = [pltpu.SMEM((2, PEER_RING, PEER_SLOTS), jnp.int32), pltpu.SemaphoreType.DMA((2,))]
    return ids_spec, table_spec, ring


def peer_scores(ids_blocks, h, tab, group):
    N = h.shape[0]
    TB = PEER_TB
    R = PEER_SLOTS * ROW_SUB
    ids_spec, table_spec, ring = _peer_specs(TB)
    return pl.pallas_call(
        _peer_u_kernel,
        grid=(N // TB,),
        in_specs=[ids_spec, pl.BlockSpec((TB, D_MODEL), lambda i: (i, 0)), table_spec,
                  pl.BlockSpec(group.shape, lambda i: (0, 0))],
        out_specs=pl.BlockSpec((PEER_SLOTS, TB), lambda i: (0, i)),
        out_shape=jax.ShapeDtypeStruct((PEER_SLOTS, N), F32),
        scratch_shapes=ring + [pltpu.VMEM((TB, R), F32)],
        compiler_params=_cp("arbitrary"),
        name="peer_scores",
    )(ids_blocks, h, tab, group)


def _peer_v_kernel(ids_ref, at_ref, gt_ref, tab_ref, rep_ref, x_ref, res_gate_ref, o_ref, ring, sems, wexp):
    a = at_ref[...]
    w = gt_ref[...] * (0.5 * a * (1.0 + lax.erf(a * (1.0 / math.sqrt(2.0)))))
    wexp[...] = jnp.dot(w.T, rep_ref[...], precision=HIGHEST, preferred_element_type=F32)
    R = PEER_SLOTS * ROW_SUB
    own = (lax.broadcasted_iota(jnp.int32, (8, R), 0) ==
           lax.broadcasted_iota(jnp.int32, (8, R), 1) % ROW_SUB)

    def consume(t, rows, carry):
        x = _unpack_rows(rows)
        wsel = jnp.where(own, wexp[pl.ds(t, 1), :], 0.0).astype(BF16)
        y = jnp.dot(wsel, x, preferred_element_type=F32)
        for s in range(ROW_SUB):
            cols = slice(s * 256, (s + 1) * 256)
            o_ref[pl.ds(t, 1), cols] = x_ref[pl.ds(t, 1), cols] + res_gate_ref[0, :, cols] * y[s:s + 1, :]
        return carry

    _token_pipeline(ids_ref, tab_ref, ring, sems, consume)


def peer_combine(ids_flat, at, gt, tab, rep, x, res_gate):
    B, L, D = x.shape
    N = at.shape[1]
    TB = PEER_TB
    R = PEER_SLOTS * ROW_SUB
    ids_spec, table_spec, ring = _peer_specs(TB)
    slot_tile = pl.BlockSpec((PEER_SLOTS, TB), lambda i: (0, i))
    row_tile = pl.BlockSpec((TB, D), lambda i: (i, 0))
    out = pl.pallas_call(
        _peer_v_kernel,
        grid=(N // TB,),
        in_specs=[ids_spec, slot_tile, slot_tile, table_spec, pl.BlockSpec(rep.shape, lambda i: (0, 0)),
                  row_tile, pl.BlockSpec((1, 1, D), lambda i: ((i * TB) // L, 0, 0))],
        out_specs=row_tile,
        out_shape=jax.ShapeDtypeStruct((N, D), F32),
        scratch_shapes=ring + [pltpu.VMEM((TB, R), F32)],
        compiler_params=_cp("arbitrary"),
        name="peer_combine",
    )(ids_flat, at, gt, tab, rep, x.reshape(N, D), res_gate)
    return out.reshape(B, L, D)


def _pad_last(a, n):
    return jnp.pad(a, [(0, 0)] * (a.ndim - 1) + [(0, n - a.shape[-1])])


def _pack_kernel(x_ref, o_ref):
    n = x_ref.shape[0]
    bits = pltpu.bitcast(x_ref[...].astype(BF16).astype(F32), jnp.uint32)
    for s in range(ROW_SUB):
        lo = bits[:, s * 256:s * 256 + 128] >> 16
        hi = bits[:, s * 256 + 128:(s + 1) * 256] & jnp.uint32(0xFFFF0000)
        o_ref[pl.ds(s, n, stride=ROW_SUB), :] = lo | hi


def _pack_table(tab):
    E, D = tab.shape
    te = 512
    return pl.pallas_call(
        _pack_kernel,
        grid=(E // te,),
        in_specs=[pl.BlockSpec((te, D), lambda i: (i, 0))],
        out_specs=pl.BlockSpec((te * ROW_SUB, 128), lambda i: (i, 0)),
        out_shape=jax.ShapeDtypeStruct((E * ROW_SUB, 128), jnp.uint32),
        compiler_params=_cp("parallel"),
        name="pack_table",
    )(tab)


def _layer_weights(l, norm1_gain, norm2_gain, w_in, pool_w, pool_scale, mla_q_norm, mla_kv_norm, w_uq, w_ukv,
                   q_norm, k_norm, ml_gate_bias, ml_out_norm, w_out, peer_wq, peer_subkeys, peer_u, peer_v):
    parts, start = [], 0
    for size in IN_SIZES:
        parts.append(w_in[l][:, start:start + size])
        start += size
    w_in_r = jnp.concatenate([parts[0], parts[1], parts[2], _pad_last(parts[3], 128), parts[4], parts[5],
                              parts[6], parts[7], _pad_last(parts[8], 128)], axis=1).astype(BF16)
    wbd = jnp.zeros((POOL_WIDTH, POOL_WIDTH), F32)
    for g in range(len(POOL_WINDOWS)):
        sl = slice(g * POOL_GROUP, (g + 1) * POOL_GROUP)
        wbd = wbd.at[sl, sl].set(pool_w[l, g])
    wq = _pad_last(w_uq[l].reshape(MLA_Q_RANK, MLA_HEADS, MLA_QK), HEAD_PAD).reshape(MLA_Q_RANK, -1)
    ukv = w_ukv[l].reshape(MLA_KV_RANK, MLA_HEADS, MLA_NOPE + MLA_V)
    wk = _pad_last(ukv[..., :MLA_NOPE], HEAD_PAD).reshape(MLA_KV_RANK, -1)
    wv = _pad_last(ukv[..., MLA_NOPE:], HEAD_PAD).reshape(MLA_KV_RANK, -1)
    r = jnp.arange(128)[:, None]
    cidx = jnp.arange(MLA_HEADS * HEAD_PAD)[None, :]
    pk = ((r < MLA_ROPE) & (cidx % HEAD_PAD == MLA_NOPE + r)).astype(F32)
    lane = jnp.arange(ML_WIDTH)
    hs = (lane[:, None] // ML_DK == jnp.arange(128)[None, :]).astype(F32)
    wo = w_out[l]
    wa = wo[POOL_WIDTH:POOL_WIDTH + MLA_HEADS * MLA_V]
    return dict(
        n1=norm1_gain[l], n2=norm2_gain[l], w_in=w_in_r,
        wbd=wbd.astype(BF16), pool_scale=pool_scale[l],
        qan=mla_q_norm[l], kvan=mla_kv_norm[l],
        wq=wq.astype(BF16), wk=wk.astype(BF16), wv=wv.astype(BF16), pk=pk,
        qg=_pad_last(q_norm[l], HEAD_PAD).reshape(1, HEAD_PAD), kg=_pad_last(k_norm[l], HEAD_PAD).reshape(1, HEAD_PAD),
        gate_bias=_pad_last(ml_gate_bias[l], 128).reshape(1, 128),
        mlg=ml_out_norm[l].reshape(1, ML_WIDTH), hs=hs, hst=hs.T,
        wp=wo[:POOL_WIDTH].astype(BF16), wa=wa.astype(BF16),
        wm=wo[POOL_WIDTH + MLA_HEADS * MLA_V:].astype(BF16),
        peer_wq=peer_wq[l].astype(BF16),
        sk=peer_subkeys[l].reshape(2 * PEER_HEADS, PEER_KEYS, PEER_KEYS),
        u_tab=_pack_table(peer_u[l]), v_tab=_pack_table(peer_v[l]),
    )


def _peer_constants():
    R = PEER_SLOTS * ROW_SUB
    pairs = _candidate_pairs()
    ncand = -(-len(pairs) // 8) * 8
    k1 = jnp.array([p[0] for p in pairs] + [-1] * (ncand - len(pairs)))[:, None]
    k2 = jnp.array([p[1] for p in pairs] + [-1] * (ncand - len(pairs)))[:, None]
    rank = jnp.arange(PEER_TOPK)[None, :]
    g1 = (k1 == rank).astype(F32)
    g2 = (k2 == rank).astype(F32)
    rep = (jnp.arange(PEER_SLOTS)[:, None] == jnp.arange(R)[None, :] // ROW_SUB).astype(F32)
    return g1, g2, rep


def _rope_tables(n):
    n_rows = n // GRID_W
    row = jnp.repeat(jnp.arange(n_rows), GRID_W, total_repeat_length=n).astype(F32)
    col = (jnp.arange(n) % GRID_W).astype(F32)
    per_axis = MLA_ROPE // 2
    freqs = ROPE_BASE ** (-jnp.arange(0, per_axis, 2, dtype=F32) / per_axis)
    ang = jnp.concatenate([row[:, None] * freqs, col[:, None] * freqs], axis=-1)
    c, s = jnp.cos(ang), jnp.sin(ang)
    cos = jnp.concatenate([jnp.ones((n, MLA_NOPE), F32), c, c, jnp.ones((n, HEAD_PAD - MLA_QK), F32)], axis=-1)
    sin = jnp.concatenate([jnp.zeros((n, MLA_NOPE), F32), -s, s, jnp.zeros((n, HEAD_PAD - MLA_QK), F32)], axis=-1)
    return cos, sin


def _staged_ids(offs):
    blk = offs.reshape(-1, PEER_TB, PEER_SLOTS)
    spare = jnp.broadcast_to(blk[:, :1], (blk.shape[0], PEER_RING, PEER_SLOTS))
    return jnp.concatenate([blk, spare], axis=1)


def _peer_ffn(x, w, shift, scale, gate, consts):
    B, L, D = x.shape
    g1, g2, rep = consts
    qp_t, h = peer_query(x, w["n2"], shift, scale, w["peer_wq"])
    ids_t, gates_t = peer_topk(qp_t, w["sk"], g1, g2)
    ids_blocks = _staged_ids(ids_t.T * ROW_SUB)
    a_t = peer_scores(ids_blocks, h, w["u_tab"], rep)
    return peer_combine(ids_blocks, a_t, gates_t, w["v_tab"], rep, x, gate)


def kernel(x, c, ctx, c_ctx, norm1_gain, norm2_gain, w_ada, b_ada, w_in, pool_w, pool_scale, mla_q_norm,
           mla_kv_norm, w_uq, w_ukv, q_norm, k_norm, ml_gate_bias, ml_out_norm, w_out, peer_wq, peer_subkeys,
           peer_u, peer_v):
    B, S, D = x.shape
    Lc_ctx = ctx.shape[1]
    depth = w_in.shape[0]
    cos_x, sin_x = _rope_tables(S)
    cos_c = jnp.ones((Lc_ctx, HEAD_PAD), F32)
    sin_c = jnp.zeros((Lc_ctx, HEAD_PAD), F32)
    consts = _peer_constants()
    cond8 = jnp.zeros((8, D), F32).at[:B].set(c).at[B].set(c_ctx)
    W = ML_WIDTH
    zero_state = (jnp.zeros((B, 2, W, W), F32), jnp.zeros((B, 2, 1, W), F32),
                  jnp.full((B, 2, 1, W), NEG_INIT, F32))
    x_ctx = ctx
    for l in range(depth):
        last = l == depth - 1
        w = _layer_weights(l, norm1_gain, norm2_gain, w_in, pool_w, pool_scale, mla_q_norm, mla_kv_norm, w_uq,
                           w_ukv, q_norm, k_norm, ml_gate_bias, ml_out_norm, w_out, peer_wq, peer_subkeys,
                           peer_u, peer_v)
        mods = ada_mod(cond8, w_ada[l], b_ada[l])
        m_x = [mods[:B, i * D:(i + 1) * D].reshape(B, 1, D) for i in range(6)]
        m_c = [jnp.broadcast_to(mods[B, i * D:(i + 1) * D].reshape(1, 1, D), (B, 1, D)) for i in range(6)]

        zp_c, zm_c, zl_c = in_proj(x_ctx, w["n1"], m_c[0], m_c[1], w["w_in"])
        qc, kc, vc = mla_qkv(zm_c, w["qan"], w["kvan"], w["wq"], w["wk"], w["wv"], w["pk"], w["qg"], w["kg"],
                             cos_c, sin_c)
        hf_c, hb_c, cT, nT, mT = mlstm(zl_c, w["gate_bias"], zero_state)
        if not last:
            pool_c = pool_mixer(zp_c, w["wbd"], w["pool_scale"])
            attn_c = attention(qc, kc, vc)
            xc = out_proj(x_ctx, pool_c, attn_c, hf_c, hb_c, zl_c, w["mlg"], w["hs"], w["hst"],
                          w["wp"], w["wa"], w["wm"], m_c[2])
            x_ctx_new = _peer_ffn(xc, w, m_c[3], m_c[4], m_c[5], consts)

        zp, zm, zl = in_proj(x, w["n1"], m_x[0], m_x[1], w["w_in"])
        pool_x = pool_mixer(zp, w["wbd"], w["pool_scale"])
        q, k, v = mla_qkv(zm, w["qan"], w["kvan"], w["wq"], w["wk"], w["wv"], w["pk"], w["qg"], w["kg"],
                          cos_x, sin_x)
        attn_x = attention(q, kc, vc, k, v)
        hf, hb, _, _, _ = mlstm(zl, w["gate_bias"], (cT, nT, mT))
        x = out_proj(x, pool_x, attn_x, hf, hb, zl, w["mlg"], w["hs"], w["hst"],
                     w["wp"], w["wa"], w["wm"], m_x[2])
        x = _peer_ffn(x, w, m_x[3], m_x[4], m_x[5], consts)
        if not last:
            x_ctx = x_ctx_new
    return x
```

```python
import functools
import math

import jax
import jax.numpy as jnp
from jax import lax
from jax.experimental import pallas as pl
from jax.experimental.pallas import tpu as pltpu

F32 = jnp.float32
BF16 = jnp.bfloat16
HIGHEST = lax.Precision.HIGHEST

EPS = 1e-6
D_MODEL = 1024
GRID_W = 64
ROPE_BASE = 10000.0
POOL_WINDOWS = (2, 4, 8, 16)
POOL_GROUP = 64
POOL_WIDTH = 256
MLA_HEADS = 8
MLA_NOPE = 64
MLA_ROPE = 32
MLA_QK = 96
MLA_V = 64
MLA_Q_RANK = 384
MLA_KV_RANK = 256
MLA_SCALE = MLA_QK ** -0.5
Q_PRESCALE = MLA_SCALE * math.log2(math.e)
HEAD_PAD = 128
ATT_TQ = 512
ATT_TC = 512
VT_ROWS = 80
ML_HEADS = 4
ML_DK = 64
ML_WIDTH = 256
ML_CHUNK = 128
NEG_INIT = -1e30
IN_SIZES = (256, 384, 256, 32, 256, 256, 256, 256, 16)
PEER_HEADS = 8
PEER_KEYS = 128
PEER_TOPK = 16
PEER_SLOTS = PEER_HEADS * PEER_TOPK
PEER_TB = 256
PEER_RING = 8
ROW_SUB = 4

Z_POOL, Z_MLA, Z_ML = 256, 768, 1152
VMEM_LIMIT = 56 * 1024 * 1024


def _cp(*sem, vmem=VMEM_LIMIT):
    return pltpu.CompilerParams(dimension_semantics=sem, vmem_limit_bytes=vmem)


def _ada_kernel(c_ref, w_ref, b_ref, o_ref):
    c = c_ref[...]
    s = c * (1.0 / (1.0 + jnp.exp(-c)))
    o_ref[...] = jnp.dot(s, w_ref[...], precision=HIGHEST, preferred_element_type=F32) + b_ref[...]


def ada_mod(cond8, w, b):
    n = w.shape[1]
    tn = n // 4
    return pl.pallas_call(
        _ada_kernel,
        grid=(n // tn,),
        in_specs=[pl.BlockSpec((8, D_MODEL), lambda j: (0, 0)),
                  pl.BlockSpec((D_MODEL, tn), lambda j: (0, j)),
                  pl.BlockSpec((1, tn), lambda j: (0, j))],
        out_specs=pl.BlockSpec((8, tn), lambda j: (0, j)),
        out_shape=jax.ShapeDtypeStruct((8, n), F32),
        compiler_params=_cp("arbitrary"),
        name="ada_mod",
    )(cond8, w, b.reshape(1, n))


def _modulated(x, gain, shift, scale):
    ms = jnp.mean(x * x, axis=-1, keepdims=True)
    y = x * lax.rsqrt(ms + EPS) * gain
    return y * (1.0 + scale) + shift


def _inproj_kernel(x_ref, gain_ref, shift_ref, scale_ref, w_ref, zp_ref, zm_ref, zl_ref):
    h = _modulated(x_ref[0], gain_ref[...], shift_ref[0], scale_ref[0])
    res = jnp.dot(h.astype(BF16), w_ref[...], preferred_element_type=F32)
    zp_ref[0] = res[:, :Z_POOL]
    zm_ref[0] = res[:, Z_POOL:Z_POOL + Z_MLA]
    zl_ref[0] = res[:, Z_POOL + Z_MLA:]


def in_proj(x, gain, shift, scale, w):
    B, L, D = x.shape
    tm = min(512, L)
    n = w.shape[1]
    vec = pl.BlockSpec((1, 1, D), lambda b, i: (b, 0, 0))
    return pl.pallas_call(
        _inproj_kernel,
        grid=(B, L // tm),
        in_specs=[pl.BlockSpec((1, tm, D), lambda b, i: (b, i, 0)),
                  pl.BlockSpec((1, D), lambda b, i: (0, 0)),
                  vec, vec,
                  pl.BlockSpec((D, n), lambda b, i: (0, 0))],
        out_specs=[pl.BlockSpec((1, tm, Z_POOL), lambda b, i: (b, i, 0)),
                   pl.BlockSpec((1, tm, Z_MLA), lambda b, i: (b, i, 0)),
                   pl.BlockSpec((1, tm, Z_ML), lambda b, i: (b, i, 0))],
        out_shape=[jax.ShapeDtypeStruct((B, L, Z_POOL), F32),
                   jax.ShapeDtypeStruct((B, L, Z_MLA), F32),
                   jax.ShapeDtypeStruct((B, L, Z_ML), F32)],
        compiler_params=_cp("parallel", "parallel"),
        name="in_proj",
    )(x, gain.reshape(1, D), shift, scale, w)


def _peerq_kernel(x_ref, gain_ref, shift_ref, scale_ref, w_ref, qp_ref, h_ref):
    h = _modulated(x_ref[0], gain_ref[...], shift_ref[0], scale_ref[0])
    hb = h.astype(BF16)
    h_ref[...] = hb.astype(F32)
    res = jnp.dot(hb, w_ref[...], preferred_element_type=F32)
    for g in range(2 * PEER_HEADS):
        qp_ref[g] = res[:, g * PEER_KEYS:(g + 1) * PEER_KEYS].T


def peer_query(x, gain, shift, scale, w):
    B, L, D = x.shape
    tm = min(512, L)
    nb = L // tm
    n = w.shape[1]
    vec = pl.BlockSpec((1, 1, D), lambda b, i: (b, 0, 0))
    return pl.pallas_call(
        _peerq_kernel,
        grid=(B, nb),
        in_specs=[pl.BlockSpec((1, tm, D), lambda b, i: (b, i, 0)),
                  pl.BlockSpec((1, D), lambda b, i: (0, 0)),
                  vec, vec,
                  pl.BlockSpec((D, n), lambda b, i: (0, 0))],
        out_specs=[pl.BlockSpec((2 * PEER_HEADS, PEER_KEYS, tm), lambda b, i: (0, 0, b * nb + i)),
                   pl.BlockSpec((tm, D), lambda b, i: (b * nb + i, 0))],
        out_shape=[jax.ShapeDtypeStruct((2 * PEER_HEADS, PEER_KEYS, B * L), F32),
                   jax.ShapeDtypeStruct((B * L, D), F32)],
        compiler_params=_cp("parallel", "parallel"),
        name="peer_query",
    )(x, gain.reshape(1, D), shift, scale, w)


def _pool_kernel(p_ref, c_ref, n_ref, wbd_ref, sc_ref, o_ref, *, L, T):
    i = pl.program_id(1)
    cur = c_ref[0]
    u3 = jnp.concatenate([p_ref[0], cur, n_ref[0]], axis=0).astype(BF16)
    t = i * T + lax.broadcasted_iota(jnp.int32, (T, 3 * T), 0)
    s = (i - 1) * T + lax.broadcasted_iota(jnp.int32, (T, 3 * T), 1)
    lane = lax.broadcasted_iota(jnp.int32, (T, POOL_WIDTH), 1)
    trow = i * T + lax.broadcasted_iota(jnp.int32, (T, POOL_WIDTH), 0)
    win = jnp.zeros((T, POOL_WIDTH), F32)
    for g, w in enumerate(POOL_WINDOWS):
        lo = jnp.maximum(t - w // 2, 0)
        hi = jnp.minimum(t + w // 2, L)
        band = jnp.where((s >= lo) & (s < hi), 1.0, 0.0).astype(BF16)
        ws = jnp.dot(band, u3, preferred_element_type=F32)
        cnt = (jnp.minimum(trow + w // 2, L) - jnp.maximum(trow - w // 2, 0)).astype(F32)
        in_group = (lane >= g * POOL_GROUP) & (lane < (g + 1) * POOL_GROUP)
        win = jnp.where(in_group, ws / cnt, win)
    d = win - cur
    y = jnp.dot(d.astype(BF16), wbd_ref[...], preferred_element_type=F32)
    o_ref[0] = y * sc_ref[...]


def pool_mixer(z_pool, wbd, scale):
    B, L, C = z_pool.shape
    T = 256
    nb = L // T
    return pl.pallas_call(
        functools.partial(_pool_kernel, L=L, T=T),
        grid=(B, nb),
        in_specs=[pl.BlockSpec((1, T, C), lambda b, i: (b, jnp.maximum(i - 1, 0), 0)),
                  pl.BlockSpec((1, T, C), lambda b, i: (b, i, 0)),
                  pl.BlockSpec((1, T, C), lambda b, i: (b, jnp.minimum(i + 1, nb - 1), 0)),
                  pl.BlockSpec((C, C), lambda b, i: (0, 0)),
                  pl.BlockSpec((1, C), lambda b, i: (0, 0))],
        out_specs=pl.BlockSpec((1, T, C), lambda b, i: (b, i, 0)),
        out_shape=jax.ShapeDtypeStruct((B, L, C), F32),
        compiler_params=_cp("parallel", "parallel"),
        name="pool_mixer",
    )(z_pool, z_pool, z_pool, wbd, scale.reshape(1, C))


def _rms(x, gain, n):
    ss = jnp.sum(x * x, axis=-1, keepdims=True) * (1.0 / n)
    return x * lax.rsqrt(ss + EPS) * gain


def _mla_kernel(z_ref, qan_ref, kvan_ref, wq_ref, wk_ref, wv_ref, pk_ref, qg_ref, kg_ref,
                cos_ref, sin_ref, q_out, k_out, v_out):
    z = z_ref[0]
    zq = z[:, :MLA_Q_RANK]
    zkv = z[:, MLA_Q_RANK:MLA_Q_RANK + MLA_KV_RANK]
    zkr = z[:, MLA_Q_RANK + MLA_KV_RANK:]
    nq = _rms(zq, qan_ref[...], MLA_Q_RANK).astype(BF16)
    nkv = _rms(zkv, kvan_ref[...], MLA_KV_RANK).astype(BF16)
    qp = jnp.dot(nq, wq_ref[...], preferred_element_type=F32)
    kp = jnp.dot(nkv, wk_ref[...], preferred_element_type=F32)
    kp = kp + jnp.dot(zkr, pk_ref[...], precision=HIGHEST, preferred_element_type=F32)
    vp = jnp.dot(nkv, wv_ref[...], preferred_element_type=F32)
    cos = cos_ref[...]
    sin = sin_ref[...]
    lane = lax.broadcasted_iota(jnp.int32, cos.shape, 1)
    first_half = lane < MLA_NOPE + MLA_ROPE // 2
    extra = (VT_ROWS - MLA_V, z.shape[0])
    ones_rows = jnp.where(lax.broadcasted_iota(jnp.int32, extra, 0) == 0, 1.0, 0.0)
    for h in range(MLA_HEADS):
        sl = slice(h * HEAD_PAD, (h + 1) * HEAD_PAD)
        for src, gain_ref, out, mult in ((qp, qg_ref, q_out, Q_PRESCALE), (kp, kg_ref, k_out, 1.0)):
            xn = _rms(src[:, sl], gain_ref[...], MLA_QK)
            partner = jnp.where(first_half, pltpu.roll(xn, HEAD_PAD - MLA_ROPE // 2, 1),
                                pltpu.roll(xn, MLA_ROPE // 2, 1))
            xr = xn * cos + partner * sin
            out[0, h] = (xr * mult).astype(BF16)
        v_out[0, h] = jnp.concatenate([vp[:, sl].T[:MLA_V, :], ones_rows], axis=0).astype(BF16)


def mla_qkv(z_mla, qan, kvan, wq, wk, wv, pk, qg, kg, cos, sin):
    B, L, _ = z_mla.shape
    T = 256
    full = lambda shape: pl.BlockSpec(shape, lambda b, i: (0,) * len(shape))
    head_out = pl.BlockSpec((1, MLA_HEADS, T, HEAD_PAD), lambda b, i: (b, 0, i, 0))
    out_sds = jax.ShapeDtypeStruct((B, MLA_HEADS, L, HEAD_PAD), BF16)
    vt_out = pl.BlockSpec((1, MLA_HEADS, VT_ROWS, T), lambda b, i: (b, 0, 0, i))
    vt_sds = jax.ShapeDtypeStruct((B, MLA_HEADS, VT_ROWS, L), BF16)
    return pl.pallas_call(
        _mla_kernel,
        grid=(B, L // T),
        in_specs=[pl.BlockSpec((1, T, Z_MLA), lambda b, i: (b, i, 0)),
                  full((1, MLA_Q_RANK)), full((1, MLA_KV_RANK)),
                  full(wq.shape), full(wk.shape), full(wv.shape), full(pk.shape),
                  full((1, HEAD_PAD)), full((1, HEAD_PAD)),
                  pl.BlockSpec((T, HEAD_PAD), lambda b, i: (i, 0)),
                  pl.BlockSpec((T, HEAD_PAD), lambda b, i: (i, 0))],
        out_specs=[head_out, head_out, vt_out],
        out_shape=[out_sds, out_sds, vt_sds],
        compiler_params=_cp("parallel", "parallel"),
        name="mla_qkv",
    )(z_mla, qan.reshape(1, -1), kvan.reshape(1, -1), wq, wk, wv, pk, qg, kg, cos, sin)


def _flash_kernel(q_ref, kc_ref, vct_ref, *rest, nchunks):
    if nchunks:
        k_ref, vt_ref, o_ref, s_s, m_s, acc_s = rest
    else:
        o_ref, s_s, m_s, acc_s = rest
    nc = kc_ref.shape[2]

    def scores(h, kblk):
        n = kblk.shape[0]
        s_s[h, :n] = lax.dot_general(kblk, q_ref[0, h], (((1,), (1,)), ((), ())), preferred_element_type=F32)

    def update(h, n, vtblk):
        st = s_s[h, :n]
        m = m_s[h]
        m_new = jnp.maximum(m, jnp.max(st, axis=0, keepdims=True))
        p = jnp.exp2(st - m_new).astype(BF16)
        m_s[h] = m_new
        acc_s[h] = jnp.exp2(m - m_new) * acc_s[h] + jnp.dot(vtblk, p, preferred_element_type=F32)

    m_s[...] = jnp.full(m_s.shape, -jnp.inf, F32)
    acc_s[...] = jnp.zeros(acc_s.shape, F32)
    scores(0, kc_ref[0, 0])
    scores(1, kc_ref[0, 1])
    update(0, nc, vct_ref[0, 0])
    if nchunks:
        scores(0, k_ref[0, 0, pl.ds(0, ATT_TC), :])
    update(1, nc, vct_ref[0, 1])
    if nchunks:
        def body(c, _):
            off = pl.multiple_of(c * ATT_TC, ATT_TC)
            nxt = pl.multiple_of(jnp.minimum(c + 1, nchunks - 1) * ATT_TC, ATT_TC)
            scores(1, k_ref[0, 1, pl.ds(off, ATT_TC), :])
            update(0, ATT_TC, vt_ref[0, 0, :, pl.ds(off, ATT_TC)])
            scores(0, k_ref[0, 0, pl.ds(nxt, ATT_TC), :])
            update(1, ATT_TC, vt_ref[0, 1, :, pl.ds(off, ATT_TC)])
            return 0
        lax.fori_loop(0, nchunks, body, 0, unroll=16)
    o = jnp.concatenate([acc_s[h, :MLA_V] / acc_s[h, MLA_V:MLA_V + 1] for h in range(2)], axis=0)
    o_ref[0] = o.T.astype(BF16)


def attention(q, kc, vct, k=None, vt=None):
    B, H, Lq, _ = q.shape
    Lc = kc.shape[2]
    tq = min(ATT_TQ, Lq)
    pair4 = lambda n, d: pl.BlockSpec((1, 2, n, d), lambda b, h, i: (b, h, 0, 0))
    in_specs = [pl.BlockSpec((1, 2, tq, HEAD_PAD), lambda b, h, i: (b, h, i, 0)),
                pair4(Lc, HEAD_PAD), pair4(VT_ROWS, Lc)]
    args = [q, kc, vct]
    nchunks = 0
    if k is not None:
        Lk = k.shape[2]
        nchunks = Lk // ATT_TC
        in_specs += [pair4(Lk, HEAD_PAD), pair4(VT_ROWS, Lk)]
        args += [k, vt]
    return pl.pallas_call(
        functools.partial(_flash_kernel, nchunks=nchunks),
        grid=(B, H // 2, Lq // tq),
        in_specs=in_specs,
        out_specs=pl.BlockSpec((1, tq, 2 * MLA_V), lambda b, h, i: (b, i, h)),
        out_shape=jax.ShapeDtypeStruct((B, Lq, H * MLA_V), BF16),
        scratch_shapes=[pltpu.VMEM((2, max(ATT_TC, Lc), tq), F32), pltpu.VMEM((2, 1, tq), F32),
                        pltpu.VMEM((2, VT_ROWS, tq), F32)],
        compiler_params=_cp("parallel", "parallel", "arbitrary"),
        name="attention",
    )(*args)


def _log_sigmoid(x):
    return jnp.minimum(x, 0.0) - jnp.log(1.0 + jnp.exp(-jnp.abs(x)))


def _mlstm_direction(d, q, k, v, g, C_s, n_s, m_s):
    Lc = q.shape[0]
    row = lax.broadcasted_iota(jnp.int32, (Lc, Lc), 0)
    col = lax.broadcasted_iota(jnp.int32, (Lc, Lc), 1)
    tri = (col <= row) if d == 0 else (col >= row)
    logf = _log_sigmoid(g)
    bcol = jnp.dot(jnp.where(tri, 1.0, 0.0), logf, precision=HIGHEST, preferred_element_type=F32)
    bT = bcol.T
    gT = g.T
    lane = lax.broadcasted_iota(jnp.int32, (1, ML_WIDTH), 1)
    kb = k.astype(BF16)
    vb = v.astype(BF16)
    Cst = C_s[d]
    nst = n_s[d]
    mst = m_s[d]
    qc = jnp.dot(q.astype(BF16), Cst.astype(BF16), preferred_element_type=F32)
    out = jnp.zeros((Lc, ML_WIDTH), F32)
    ws_all = jnp.zeros((Lc, ML_WIDTH), F32)
    wprev_all = jnp.zeros((1, ML_WIDTH), F32)
    mnew_all = jnp.zeros((1, ML_WIDTH), F32)
    for h in range(ML_HEADS):
        il = 8 * d + h
        fl = 8 * d + 4 + h
        head = (lane >= h * ML_DK) & (lane < (h + 1) * ML_DK)
        bc = bcol[:, fl:fl + 1]
        br = bT[fl:fl + 1, :]
        ir = gT[il:il + 1, :]
        ic = g[:, il:il + 1]
        mprev = mst[:, h * ML_DK:h * ML_DK + 1]
        dmat = jnp.where(tri, bc - br + ir, -jnp.inf)
        inter = bc + mprev
        mj = jnp.maximum(inter, jnp.max(dmat, axis=-1, keepdims=True))
        w_inter = jnp.exp(inter - mj)
        qh = jnp.where(head, q, 0.0)
        s = lax.dot_general(qh.astype(BF16), kb, (((1,), (1,)), ((), ())), preferred_element_type=F32)
        qk = s * jnp.exp(dmat - mj)
        pv = jnp.dot(qk.astype(BF16), vb, preferred_element_type=F32)
        qn = jnp.sum(qh * nst, axis=-1, keepdims=True)
        den = jnp.sum(qk, axis=-1, keepdims=True) + w_inter * qn
        denom = jnp.maximum(jnp.abs(den), jnp.exp(-mj))
        out = jnp.where(head, (pv + qc * w_inter) / denom, out)
        blast = bc[Lc - 1:Lc, :] if d == 0 else bc[0:1, :]
        dec = blast - bc + ic
        mnew = jnp.maximum(blast + mprev, jnp.max(dec, axis=0, keepdims=True))
        wprev = jnp.exp(blast + mprev - mnew)
        ws = jnp.exp(dec - mnew)
        ws_all = jnp.where(head, ws, ws_all)
        wprev_all = jnp.where(head, wprev, wprev_all)
        mnew_all = jnp.where(head, mnew, mnew_all)
    kw = k * ws_all
    upd = jnp.dot(kw.T.astype(BF16), vb, preferred_element_type=F32)
    r2 = lax.broadcasted_iota(jnp.int32, (ML_WIDTH, ML_WIDTH), 0) // ML_DK
    c2 = lax.broadcasted_iota(jnp.int32, (ML_WIDTH, ML_WIDTH), 1) // ML_DK
    C_s[d] = Cst * wprev_all + jnp.where(r2 == c2, upd, 0.0)
    n_s[d] = nst * wprev_all + jnp.sum(kw, axis=0, keepdims=True)
    m_s[d] = mnew_all
    return out


def _mlstm_kernel(qf_ref, kf_ref, vf_ref, gf_ref, qb_ref, kb_ref, vb_ref, gb_ref, bias_ref,
                  c0_ref, n0_ref, m0_ref, hf_ref, hb_ref, cT_ref, nT_ref, mT_ref, C_s, n_s, m_s):
    c = pl.program_id(1)

    @pl.when(c == 0)
    def _():
        C_s[...] = c0_ref[0]
        n_s[...] = n0_ref[0]
        m_s[...] = m0_ref[0]

    scale = ML_DK ** -0.5
    hf_ref[0] = _mlstm_direction(0, qf_ref[0] * scale, kf_ref[0], vf_ref[0],
                                 gf_ref[0] + bias_ref[...], C_s, n_s, m_s)
    hb_ref[0] = _mlstm_direction(1, qb_ref[0] * scale, kb_ref[0], vb_ref[0],
                                 gb_ref[0] + bias_ref[...], C_s, n_s, m_s)

    @pl.when(c == pl.num_programs(1) - 1)
    def _():
        cT_ref[0] = C_s[...]
        nT_ref[0] = n_s[...]
        mT_ref[0] = m_s[...]


def mlstm(z_ml, bias, state):
    B, L, _ = z_ml.shape
    Lc = ML_CHUNK
    nc = L // Lc
    c0, n0, m0 = state
    W = ML_WIDTH
    fwd = lambda j: pl.BlockSpec((1, Lc, W), lambda b, c: (b, c, j))
    bwd = lambda j: pl.BlockSpec((1, Lc, W), lambda b, c: (b, nc - 1 - c, j))
    gcol = 4 * W // 128
    st_c = pl.BlockSpec((1, 2, W, W), lambda b, c: (b, 0, 0, 0))
    st_v = pl.BlockSpec((1, 2, 1, W), lambda b, c: (b, 0, 0, 0))
    return pl.pallas_call(
        _mlstm_kernel,
        grid=(B, nc),
        in_specs=[fwd(0), fwd(1), fwd(2), pl.BlockSpec((1, Lc, 128), lambda b, c: (b, c, gcol)),
                  bwd(0), bwd(1), bwd(2), pl.BlockSpec((1, Lc, 128), lambda b, c: (b, nc - 1 - c, gcol)),
                  pl.BlockSpec((1, 128), lambda b, c: (0, 0)),
                  st_c, st_v, st_v],
        out_specs=[pl.BlockSpec((1, Lc, W), lambda b, c: (b, c, 0)),
                   pl.BlockSpec((1, Lc, W), lambda b, c: (b, nc - 1 - c, 0)),
                   st_c, st_v, st_v],
        out_shape=[jax.ShapeDtypeStruct((B, L, W), F32), jax.ShapeDtypeStruct((B, L, W), F32),
                   jax.ShapeDtypeStruct((B, 2, W, W), F32), jax.ShapeDtypeStruct((B, 2, 1, W), F32),
                   jax.ShapeDtypeStruct((B, 2, 1, W), F32)],
        scratch_shapes=[pltpu.VMEM((2, W, W), F32), pltpu.VMEM((2, 1, W), F32), pltpu.VMEM((2, 1, W), F32)],
        compiler_params=_cp("parallel", "arbitrary"),
        name="mlstm",
    )(z_ml, z_ml, z_ml, z_ml, z_ml, z_ml, z_ml, z_ml, bias, c0, n0, m0)


def _outproj_kernel(x_ref, pool_ref, attn_ref, hf_ref, hb_ref, op_ref, mlg_ref, hs_ref, hst_ref,
                    wp_ref, wa_ref, wm_ref, ga_ref, o_ref):
    h = hf_ref[0] + hb_ref[0]
    ss = jnp.dot(h * h, hs_ref[...], precision=HIGHEST, preferred_element_type=F32) * (1.0 / ML_DK)
    inv = jnp.dot(lax.rsqrt(ss + EPS), hst_ref[...], precision=HIGHEST, preferred_element_type=F32)
    op = op_ref[0]
    ml = h * inv * mlg_ref[...] * (1.0 / (1.0 + jnp.exp(-op)))
    mix = jnp.dot(pool_ref[0].astype(BF16), wp_ref[...], preferred_element_type=F32)
    mix += jnp.dot(attn_ref[0], wa_ref[...], preferred_element_type=F32)
    mix += jnp.dot(ml.astype(BF16), wm_ref[...], preferred_element_type=F32)
    o_ref[0] = x_ref[0] + ga_ref[0] * mix


def out_proj(x, pool, attn, hf, hb, z_ml, mlg, hs, hst, wp, wa, wm, gate):
    B, L, D = x.shape
    T = min(512, L)
    W = ML_WIDTH
    tile = lambda w, j=0: pl.BlockSpec((1, T, w), lambda b, i: (b, i, j))
    full = lambda shape: pl.BlockSpec(shape, lambda b, i: (0,) * len(shape))
    return pl.pallas_call(
        _outproj_kernel,
        grid=(B, L // T),
        in_specs=[tile(D), tile(W), tile(MLA_HEADS * MLA_V), tile(W), tile(W), tile(W, 3),
                  full((1, W)), full(hs.shape), full(hst.shape),
                  full(wp.shape), full(wa.shape), full(wm.shape),
                  pl.BlockSpec((1, 1, D), lambda b, i: (b, 0, 0))],
        out_specs=tile(D),
        out_shape=jax.ShapeDtypeStruct((B, L, D), F32),
        compiler_params=_cp("parallel", "parallel"),
        name="out_proj",
    )(x, pool, attn, hf, hb, z_ml, mlg, hs, hst, wp, wa, wm, gate)


def _top16_rows(s, row):
    big = float(s.shape[0])
    vals, idxs = [], []
    for _ in range(PEER_TOPK):
        m = jnp.max(s, axis=0, keepdims=True)
        idx = jnp.min(jnp.where(s == m, row, big), axis=0, keepdims=True)
        vals.append(m)
        idxs.append(idx)
        s = jnp.where(row == idx, -jnp.inf, s)
    return jnp.concatenate(vals, axis=0), jnp.concatenate(idxs, axis=0)


def _candidate_pairs():
    return [(k1, k2) for k1 in range(PEER_TOPK) for k2 in range(PEER_TOPK) if (k1 + 1) * (k2 + 1) <= PEER_TOPK]


def _topk_kernel(qp_ref, sk_ref, g1_ref, g2_ref, ids_ref, gate_ref):
    T = 256
    row = lax.broadcasted_iota(jnp.int32, (PEER_KEYS, T), 0).astype(F32)
    pick = lambda g_ref, a: jnp.dot(g_ref[...], a, precision=HIGHEST, preferred_element_type=F32)
    ncand = g1_ref.shape[0]
    crow = lax.broadcasted_iota(jnp.int32, (ncand, T), 0).astype(F32)

    def tile(lt, _):
        cols = pl.ds(pl.multiple_of(lt * T, T), T)
        tops = []
        for c in range(2):
            s = jnp.dot(sk_ref[c], qp_ref[c, :, cols], precision=HIGHEST, preferred_element_type=F32)
            tops.append(_top16_rows(s, row))
        cand = pick(g1_ref, tops[0][0]) + pick(g2_ref, tops[1][0])
        cand = jnp.where(crow < float(len(_candidate_pairs())), cand, -jnp.inf)
        expert = pick(g1_ref, tops[0][1]) * float(PEER_KEYS) + pick(g2_ref, tops[1][1])
        best, eids = [], []
        for _ in range(PEER_TOPK):
            m = jnp.max(cand, axis=0, keepdims=True)
            ci = jnp.min(jnp.where(cand == m, crow, float(ncand)), axis=0, keepdims=True)
            sel = crow == ci
            best.append(m)
            eids.append(jnp.sum(jnp.where(sel, expert, 0.0), axis=0, keepdims=True))
            cand = jnp.where(sel, -jnp.inf, cand)
        best = jnp.concatenate(best, axis=0)
        p = jnp.exp(best - best[0:1, :])
        gate_ref[:, cols] = p / jnp.sum(p, axis=0, keepdims=True)
        ids_ref[:, cols] = jnp.concatenate(eids, axis=0).astype(jnp.int32)
        return 0

    lax.fori_loop(0, qp_ref.shape[2] // T, tile, 0)


def peer_topk(qp_t, sk, g1, g2):
    N = qp_t.shape[2]
    T = 512
    return pl.pallas_call(
        _topk_kernel,
        grid=(N // T, PEER_HEADS),
        in_specs=[pl.BlockSpec((2, PEER_KEYS, T), lambda i, h: (h, 0, i)),
                  pl.BlockSpec((2, PEER_KEYS, PEER_KEYS), lambda i, h: (h, 0, 0)),
                  pl.BlockSpec(g1.shape, lambda i, h: (0, 0)),
                  pl.BlockSpec(g2.shape, lambda i, h: (0, 0))],
        out_specs=[pl.BlockSpec((PEER_TOPK, T), lambda i, h: (h, i)),
                   pl.BlockSpec((PEER_TOPK, T), lambda i, h: (h, i))],
        out_shape=[jax.ShapeDtypeStruct((PEER_SLOTS, N), jnp.int32),
                   jax.ShapeDtypeStruct((PEER_SLOTS, N), F32)],
        compiler_params=_cp("parallel", "parallel"),
        name="peer_topk",
    )(qp_t, sk, g1, g2)


def _unpack_rows(x):
    lo = pltpu.bitcast(x << 16, F32)
    hi = pltpu.bitcast(x & jnp.uint32(0xFFFF0000), F32)
    return jnp.concatenate([lo, hi], axis=-1).astype(BF16)


def _token_pipeline(ids_ref, tab_ref, ring, sems, consume, finish_group=None):
    TB, R = PEER_TB, PEER_RING

    def group_copy(k, half):
        first = pl.multiple_of(k * R, R)
        return pltpu.make_async_copy(ids_ref.at[0, pl.ds(first, R)], ring.at[half], sems.at[half])

    group_copy(0, 0).start()

    def body(i, _):
        for half in range(2):
            k = 2 * i + half
            group_copy(k, half).wait()
            group_copy(k + 1, 1 - half).start()
            carry = None
            for s in range(R):
                rows = [tab_ref[pl.ds(ring[half, s, j], ROW_SUB), :] for j in range(PEER_SLOTS)]
                carry = consume(k * R + s, jnp.concatenate(rows, axis=0), carry)
            if finish_group is not None:
                finish_group(carry)
        return 0

    lax.fori_loop(0, TB // (2 * R), body, 0)
    group_copy(TB // R, 0).wait()


def _peer_u_kernel(ids_ref, h_ref, tab_ref, group_ref, at_ref, ring, sems, acc_s):
    ones = jnp.ones((8, 2 * 128), BF16)

    def consume(t, rows, acc):
        h = h_ref[pl.ds(t, 1), :]
        lo = pltpu.bitcast(rows << 16, F32).reshape(-1, 8, 128)
        hi = pltpu.bitcast(rows & jnp.uint32(0xFFFF0000), F32).reshape(-1, 8, 128)
        chunk = lambda s, half: h[:, s * 256 + half * 128:s * 256 + half * 128 + 128]
        h_lo = jnp.concatenate([chunk(s, 0) for s in range(ROW_SUB)] * 2, axis=0)
        h_hi = jnp.concatenate([chunk(s, 1) for s in range(ROW_SUB)] * 2, axis=0)
        prod = jnp.concatenate([(lo * h_lo[None]).reshape(-1, 128), (hi * h_hi[None]).reshape(-1, 128)],
                               axis=-1).astype(BF16)
        sums = lax.dot_general(ones, prod, (((1,), (1,)), ((), ())), preferred_element_type=F32)
        acc_s[pl.ds(t, 1), :] = sums[0:1, :]
        return acc

    _token_pipeline(ids_ref, tab_ref, ring, sems, consume)
    a_t = lax.dot_general(acc_s[...], group_ref[...], (((1,), (1,)), ((), ())), precision=HIGHEST,
                          preferred_element_type=F32)
    at_ref[...] = a_t.T


def _peer_specs(TB):
    R = PEER_SLOTS * ROW_SUB
    ids_spec = pl.BlockSpec((1, TB + PEER_RING, PEER_SLOTS), lambda i: (i, 0, 0))
    table_spec = pl.BlockSpec(memory_space=pltpu.VMEM)
    ring = [pltpu.SMEM((2, PEER_RING, PEER_SLOTS), jnp.int32), pltpu.SemaphoreType.DMA((2,))]
    return ids_spec, table_spec, ring


def peer_scores(ids_blocks, h, tab, group):
    N = h.shape[0]
    TB = PEER_TB
    R = PEER_SLOTS * ROW_SUB
    ids_spec, table_spec, ring = _peer_specs(TB)
    return pl.pallas_call(
        _peer_u_kernel,
        grid=(N // TB,),
        in_specs=[ids_spec, pl.BlockSpec((TB, D_MODEL), lambda i: (i, 0)), table_spec,
                  pl.BlockSpec(group.shape, lambda i: (0, 0))],
        out_specs=pl.BlockSpec((PEER_SLOTS, TB), lambda i: (0, i)),
        out_shape=jax.ShapeDtypeStruct((PEER_SLOTS, N), F32),
        scratch_shapes=ring + [pltpu.VMEM((TB, R), F32)],
        compiler_params=_cp("arbitrary"),
        name="peer_scores",
    )(ids_blocks, h, tab, group)


def _peer_v_kernel(ids_ref, at_ref, gt_ref, tab_ref, rep_ref, x_ref, res_gate_ref, o_ref, ring, sems, wexp):
    a = at_ref[...]
    w = gt_ref[...] * (0.5 * a * (1.0 + lax.erf(a * (1.0 / math.sqrt(2.0)))))
    wexp[...] = jnp.dot(w.T, rep_ref[...], precision=HIGHEST, preferred_element_type=F32)
    R = PEER_SLOTS * ROW_SUB
    own = (lax.broadcasted_iota(jnp.int32, (8, R), 0) ==
           lax.broadcasted_iota(jnp.int32, (8, R), 1) % ROW_SUB)

    def consume(t, rows, carry):
        x = _unpack_rows(rows)
        wsel = jnp.where(own, wexp[pl.ds(t, 1), :], 0.0).astype(BF16)
        y = jnp.dot(wsel, x, preferred_element_type=F32)
        for s in range(ROW_SUB):
            cols = slice(s * 256, (s + 1) * 256)
            o_ref[pl.ds(t, 1), cols] = x_ref[pl.ds(t, 1), cols] + res_gate_ref[0, :, cols] * y[s:s + 1, :]
        return carry

    _token_pipeline(ids_ref, tab_ref, ring, sems, consume)


def peer_combine(ids_flat, at, gt, tab, rep, x, res_gate):
    B, L, D = x.shape
    N = at.shape[1]
    TB = PEER_TB
    R = PEER_SLOTS * ROW_SUB
    ids_spec, table_spec, ring = _peer_specs(TB)
    slot_tile = pl.BlockSpec((PEER_SLOTS, TB), lambda i: (0, i))
    row_tile = pl.BlockSpec((TB, D), lambda i: (i, 0))
    out = pl.pallas_call(
        _peer_v_kernel,
        grid=(N // TB,),
        in_specs=[ids_spec, slot_tile, slot_tile, table_spec, pl.BlockSpec(rep.shape, lambda i: (0, 0)),
                  row_tile, pl.BlockSpec((1, 1, D), lambda i: ((i * TB) // L, 0, 0))],
        out_specs=row_tile,
        out_shape=jax.ShapeDtypeStruct((N, D), F32),
        scratch_shapes=ring + [pltpu.VMEM((TB, R), F32)],
        compiler_params=_cp("arbitrary"),
        name="peer_combine",
    )(ids_flat, at, gt, tab, rep, x.reshape(N, D), res_gate)
    return out.reshape(B, L, D)


def _pad_last(a, n):
    return jnp.pad(a, [(0, 0)] * (a.ndim - 1) + [(0, n - a.shape[-1])])


def _pack_kernel(x_ref, o_ref):
    n = x_ref.shape[0]
    bits = pltpu.bitcast(x_ref[...].astype(BF16).astype(F32), jnp.uint32)
    for s in range(ROW_SUB):
        lo = bits[:, s * 256:s * 256 + 128] >> 16
        hi = bits[:, s * 256 + 128:(s + 1) * 256] & jnp.uint32(0xFFFF0000)
        o_ref[pl.ds(s, n, stride=ROW_SUB), :] = lo | hi


def _pack_table(tab):
    E, D = tab.shape
    te = 512
    return pl.pallas_call(
        _pack_kernel,
        grid=(E // te,),
        in_specs=[pl.BlockSpec((te, D), lambda i: (i, 0))],
        out_specs=pl.BlockSpec((te * ROW_SUB, 128), lambda i: (i, 0)),
        out_shape=jax.ShapeDtypeStruct((E * ROW_SUB, 128), jnp.uint32),
        compiler_params=_cp("parallel"),
        name="pack_table",
    )(tab)


def _layer_weights(l, norm1_gain, norm2_gain, w_in, pool_w, pool_scale, mla_q_norm, mla_kv_norm, w_uq, w_ukv,
                   q_norm, k_norm, ml_gate_bias, ml_out_norm, w_out, peer_wq, peer_subkeys, peer_u, peer_v):
    parts, start = [], 0
    for size in IN_SIZES:
        parts.append(w_in[l][:, start:start + size])
        start += size
    w_in_r = jnp.concatenate([parts[0], parts[1], parts[2], _pad_last(parts[3], 128), parts[4], parts[5],
                              parts[6], parts[7], _pad_last(parts[8], 128)], axis=1).astype(BF16)
    wbd = jnp.zeros((POOL_WIDTH, POOL_WIDTH), F32)
    for g in range(len(POOL_WINDOWS)):
        sl = slice(g * POOL_GROUP, (g + 1) * POOL_GROUP)
        wbd = wbd.at[sl, sl].set(pool_w[l, g])
    wq = _pad_last(w_uq[l].reshape(MLA_Q_RANK, MLA_HEADS, MLA_QK), HEAD_PAD).reshape(MLA_Q_RANK, -1)
    ukv = w_ukv[l].reshape(MLA_KV_RANK, MLA_HEADS, MLA_NOPE + MLA_V)
    wk = _pad_last(ukv[..., :MLA_NOPE], HEAD_PAD).reshape(MLA_KV_RANK, -1)
    wv = _pad_last(ukv[..., MLA_NOPE:], HEAD_PAD).reshape(MLA_KV_RANK, -1)
    r = jnp.arange(128)[:, None]
    cidx = jnp.arange(MLA_HEADS * HEAD_PAD)[None, :]
    pk = ((r < MLA_ROPE) & (cidx % HEAD_PAD == MLA_NOPE + r)).astype(F32)
    lane = jnp.arange(ML_WIDTH)
    hs = (lane[:, None] // ML_DK == jnp.arange(128)[None, :]).astype(F32)
    wo = w_out[l]
    wa = wo[POOL_WIDTH:POOL_WIDTH + MLA_HEADS * MLA_V]
    return dict(
        n1=norm1_gain[l], n2=norm2_gain[l], w_in=w_in_r,
        wbd=wbd.astype(BF16), pool_scale=pool_scale[l],
        qan=mla_q_norm[l], kvan=mla_kv_norm[l],
        wq=wq.astype(BF16), wk=wk.astype(BF16), wv=wv.astype(BF16), pk=pk,
        qg=_pad_last(q_norm[l], HEAD_PAD).reshape(1, HEAD_PAD), kg=_pad_last(k_norm[l], HEAD_PAD).reshape(1, HEAD_PAD),
        gate_bias=_pad_last(ml_gate_bias[l], 128).reshape(1, 128),
        mlg=ml_out_norm[l].reshape(1, ML_WIDTH), hs=hs, hst=hs.T,
        wp=wo[:POOL_WIDTH].astype(BF16), wa=wa.astype(BF16),
        wm=wo[POOL_WIDTH + MLA_HEADS * MLA_V:].astype(BF16),
        peer_wq=peer_wq[l].astype(BF16),
        sk=peer_subkeys[l].reshape(2 * PEER_HEADS, PEER_KEYS, PEER_KEYS),
        u_tab=_pack_table(peer_u[l]), v_tab=_pack_table(peer_v[l]),
    )


def _peer_constants():
    R = PEER_SLOTS * ROW_SUB
    pairs = _candidate_pairs()
    ncand = -(-len(pairs) // 8) * 8
    k1 = jnp.array([p[0] for p in pairs] + [-1] * (ncand - len(pairs)))[:, None]
    k2 = jnp.array([p[1] for p in pairs] + [-1] * (ncand - len(pairs)))[:, None]
    rank = jnp.arange(PEER_TOPK)[None, :]
    g1 = (k1 == rank).astype(F32)
    g2 = (k2 == rank).astype(F32)
    rep = (jnp.arange(PEER_SLOTS)[:, None] == jnp.arange(R)[None, :] // ROW_SUB).astype(F32)
    return g1, g2, rep


def _rope_tables(n):
    n_rows = n // GRID_W
    row = jnp.repeat(jnp.arange(n_rows), GRID_W, total_repeat_length=n).astype(F32)
    col = (jnp.arange(n) % GRID_W).astype(F32)
    per_axis = MLA_ROPE // 2
    freqs = ROPE_BASE ** (-jnp.arange(0, per_axis, 2, dtype=F32) / per_axis)
    ang = jnp.concatenate([row[:, None] * freqs, col[:, None] * freqs], axis=-1)
    c, s = jnp.cos(ang), jnp.sin(ang)
    cos = jnp.concatenate([jnp.ones((n, MLA_NOPE), F32), c, c, jnp.ones((n, HEAD_PAD - MLA_QK), F32)], axis=-1)
    sin = jnp.concatenate([jnp.zeros((n, MLA_NOPE), F32), -s, s, jnp.zeros((n, HEAD_PAD - MLA_QK), F32)], axis=-1)
    return cos, sin


def _staged_ids(offs):
    blk = offs.reshape(-1, PEER_TB, PEER_SLOTS)
    spare = jnp.broadcast_to(blk[:, :1], (blk.shape[0], PEER_RING, PEER_SLOTS))
    return jnp.concatenate([blk, spare], axis=1)


def _peer_ffn(x, w, shift, scale, gate, consts):
    B, L, D = x.shape
    g1, g2, rep = consts
    qp_t, h = peer_query(x, w["n2"], shift, scale, w["peer_wq"])
    ids_t, gates_t = peer_topk(qp_t, w["sk"], g1, g2)
    ids_blocks = _staged_ids(ids_t.T * ROW_SUB)
    a_t = peer_scores(ids_blocks, h, w["u_tab"], rep)
    return peer_combine(ids_blocks, a_t, gates_t, w["v_tab"], rep, x, gate)


def kernel(x, c, ctx, c_ctx, norm1_gain, norm2_gain, w_ada, b_ada, w_in, pool_w, pool_scale, mla_q_norm,
           mla_kv_norm, w_uq, w_ukv, q_norm, k_norm, ml_gate_bias, ml_out_norm, w_out, peer_wq, peer_subkeys,
           peer_u, peer_v):
    B, S, D = x.shape
    Lc_ctx = ctx.shape[1]
    depth = w_in.shape[0]
    cos_x, sin_x = _rope_tables(S)
    cos_c = jnp.ones((Lc_ctx, HEAD_PAD), F32)
    sin_c = jnp.zeros((Lc_ctx, HEAD_PAD), F32)
    consts = _peer_constants()
    cond8 = jnp.zeros((8, D), F32).at[:B].set(c).at[B].set(c_ctx)
    W = ML_WIDTH
    zero_state = (jnp.zeros((B, 2, W, W), F32), jnp.zeros((B, 2, 1, W), F32),
                  jnp.full((B, 2, 1, W), NEG_INIT, F32))
    x_ctx = ctx
    for l in range(depth):
        last = l == depth - 1
        w = _layer_weights(l, norm1_gain, norm2_gain, w_in, pool_w, pool_scale, mla_q_norm, mla_kv_norm, w_uq,
                           w_ukv, q_norm, k_norm, ml_gate_bias, ml_out_norm, w_out, peer_wq, peer_subkeys,
                           peer_u, peer_v)
        mods = ada_mod(cond8, w_ada[l], b_ada[l])
        m_x = [mods[:B, i * D:(i + 1) * D].reshape(B, 1, D) for i in range(6)]
        m_c = [jnp.broadcast_to(mods[B, i * D:(i + 1) * D].reshape(1, 1, D), (B, 1, D)) for i in range(6)]

        zp_c, zm_c, zl_c = in_proj(x_ctx, w["n1"], m_c[0], m_c[1], w["w_in"])
        qc, kc, vc = mla_qkv(zm_c, w["qan"], w["kvan"], w["wq"], w["wk"], w["wv"], w["pk"], w["qg"], w["kg"],
                             cos_c, sin_c)
        hf_c, hb_c, cT, nT, mT = mlstm(zl_c, w["gate_bias"], zero_state)
        if not last:
            pool_c = pool_mixer(zp_c, w["wbd"], w["pool_scale"])
            attn_c = attention(qc, kc, vc)
            xc = out_proj(x_ctx, pool_c, attn_c, hf_c, hb_c, zl_c, w["mlg"], w["hs"], w["hst"],
                          w["wp"], w["wa"], w["wm"], m_c[2])
            x_ctx_new = _peer_ffn(xc, w, m_c[3], m_c[4], m_c[5], consts)

        zp, zm, zl = in_proj(x, w["n1"], m_x[0], m_x[1], w["w_in"])
        pool_x = pool_mixer(zp, w["wbd"], w["pool_scale"])
        q, k, v = mla_qkv(zm, w["qan"], w["kvan"], w["wq"], w["wk"], w["wv"], w["pk"], w["qg"], w["kg"],
                          cos_x, sin_x)
        attn_x = attention(q, kc, vc, k, v)
        hf, hb, _, _, _ = mlstm(zl, w["gate_bias"], (cT, nT, mT))
        x = out_proj(x, pool_x, attn_x, hf, hb, zl, w["mlg"], w["hs"], w["hst"],
                     w["wp"], w["wa"], w["wm"], m_x[2])
        x = _peer_ffn(x, w, m_x[3], m_x[4], m_x[5], consts)
        if not last:
            x_ctx = x_ctx_new
    return x
```
